```python
import jax, jax.numpy as jnp
from jax import lax
import numpy as np

D_MODEL = 2048
BATCH = 1
SEQ = 16384
DEPTH = 1
DEC_BATCH = 128
DEC_SEQ = 1
PAST_LEN = 16384
PAGE_SIZE = 128

HEAD_DIM = 64
N_HEADS = (D_MODEL // 2) // HEAD_DIM
N_KV_HEADS = 4
GQA_GROUP = N_HEADS // N_KV_HEADS
ATTN_DIM = N_HEADS * HEAD_DIM
KV_DIM = N_KV_HEADS * HEAD_DIM
WINDOW = 128
Q_BLOCK = WINDOW
ROPE_THETA = 10000.0
CONV_DIM = D_MODEL - ATTN_DIM
CONV_WIDTH = 3
SPLIT_POINTS = (ATTN_DIM, ATTN_DIM + KV_DIM, ATTN_DIM + 2 * KV_DIM,
                ATTN_DIM + 2 * KV_DIM + CONV_DIM, ATTN_DIM + 2 * KV_DIM + 2 * CONV_DIM)
IN_DIM = ATTN_DIM + 2 * KV_DIM + 3 * CONV_DIM
N_EXPERTS = 32
TOP_K = 4
D_FF = D_MODEL
SWIGLU_LIMIT = 7.0
SWIGLU_ALPHA = 1.702
EXPERT_BLOCK = 128
DEEPNORM_ALPHA = (2.0 * DEPTH) ** 0.25
DEEPNORM_BETA = (8.0 * DEPTH) ** -0.25
LN_EPS = 1e-5
RMS_EPS = 1e-6

kernel_name = "hymba_swa_sink_shortconv_moe_decode_step"


def _standardise(x):
    xf = x.astype(jnp.float32)
    mu = jnp.mean(xf, axis=-1, keepdims=True)
    var = jnp.mean(jnp.square(xf - mu), axis=-1, keepdims=True)
    return (xf - mu) * lax.rsqrt(var + LN_EPS)


def layer_norm(x, g, b):
    return (_standardise(x) * g + b).astype(x.dtype)


def rms_norm(x, g):
    xf = x.astype(jnp.float32)
    return (xf * lax.rsqrt(jnp.mean(xf * xf, axis=-1, keepdims=True) + RMS_EPS) * g).astype(x.dtype)


def modulate(x, shift, scale):
    return (_standardise(x) * (1.0 + scale) + shift).astype(x.dtype)


def rope(x, pos):
    half = HEAD_DIM // 2
    inv_freq = ROPE_THETA ** (-jnp.arange(half, dtype=jnp.float32) / half)
    ang = pos.astype(jnp.float32)[..., None, None] * inv_freq
    cos, sin = jnp.cos(ang), jnp.sin(ang)
    xf = x.astype(jnp.float32)
    x1, x2 = xf[..., :half], xf[..., half:]
    return jnp.concatenate([x1 * cos - x2 * sin, x2 * cos + x1 * sin], axis=-1).astype(x.dtype)


def window_attention(q, k_prev, v_prev, k_blk, v_blk, q_pos, k_pos, sinks):
    B, N, L = q.shape[:3]
    k = jnp.concatenate([k_prev, k_blk], axis=2)
    v = jnp.concatenate([v_prev, v_blk], axis=2)
    qg = q.reshape(B, N, L, N_KV_HEADS, GQA_GROUP, HEAD_DIM)
    s = jnp.einsum('bnqkgd,bnskd->bnkgqs', qg, k, preferred_element_type=jnp.float32) * (HEAD_DIM ** -0.5)
    rel = q_pos[:, :, None] - k_pos[:, None, :]
    allowed = (rel >= 0) & (rel <= WINDOW) & (k_pos[:, None, :] >= 0)
    s = jnp.where(allowed[None, :, None, None, :, :], s, -jnp.inf)
    sink = sinks.astype(jnp.float32).reshape(N_KV_HEADS, GQA_GROUP)[None, None, :, :, None, None]
    sink = jnp.broadcast_to(sink, s.shape[:-1] + (1,))
    p = jax.nn.softmax(jnp.concatenate([s, sink], axis=-1), axis=-1)[..., :-1]
    o = jnp.einsum('bnkgqs,bnskd->bnqkgd', p.astype(v.dtype), v)
    return o.reshape(B, N, L, ATTN_DIM)


def short_conv(u, u_hist, conv_w):
    full = jnp.concatenate([u_hist, u], axis=1)
    T = u.shape[1]
    y = conv_w[0] * full[:, 0:T]
    for tap in range(1, CONV_WIDTH):
        y = y + conv_w[tap] * full[:, tap:tap + T]
    return y, full[:, -(CONV_WIDTH - 1):]


def moe_ffn(h, w_router, b_router, w_up, b_up, w_down, b_down):
    T = h.shape[0]
    logits = jnp.dot(h, w_router, preferred_element_type=jnp.float32) + b_router.astype(jnp.float32)
    top_val, top_idx = lax.top_k(logits, TOP_K)
    gate = jax.nn.softmax(top_val, axis=-1)
    n_assign = T * TOP_K
    flat_e = top_idx.reshape(-1)
    flat_tok = jnp.arange(n_assign, dtype=jnp.int32) // TOP_K
    order = jnp.argsort(flat_e)
    sorted_e = flat_e[order]
    sorted_tok = flat_tok[order]
    sorted_gate = gate.reshape(-1)[order]
    counts = jnp.bincount(flat_e, length=N_EXPERTS)
    padded = (counts + EXPERT_BLOCK - 1) // EXPERT_BLOCK * EXPERT_BLOCK
    start = jnp.cumsum(counts) - counts
    pad_end = jnp.cumsum(padded)
    pad_start = pad_end - padded
    dest = pad_start[sorted_e] + (jnp.arange(n_assign, dtype=jnp.int32) - start[sorted_e])
    n_rows = (n_assign + EXPERT_BLOCK - 1) // EXPERT_BLOCK * EXPERT_BLOCK + N_EXPERTS * EXPERT_BLOCK
    n_blocks = n_rows // EXPERT_BLOCK
    row_tok = jnp.full((n_rows,), T, jnp.int32).at[dest].set(sorted_tok)
    row_gate = jnp.zeros((n_rows,), jnp.float32).at[dest].set(sorted_gate)
    block_e = jnp.clip(jnp.searchsorted(pad_end, jnp.arange(n_blocks, dtype=jnp.int32) * EXPERT_BLOCK, side='right'),
                       0, N_EXPERTS - 1)
    h_pad = jnp.concatenate([h, jnp.zeros((1, h.shape[1]), h.dtype)], axis=0)
    xb = h_pad[row_tok].reshape(n_blocks, EXPERT_BLOCK, h.shape[1])

    def run_block(args):
        xe, e = args
        z = xe @ w_up[e] + b_up[e]
        glu = jnp.minimum(z[:, :D_FF], SWIGLU_LIMIT)
        lin = jnp.clip(z[:, D_FF:], -SWIGLU_LIMIT, SWIGLU_LIMIT)
        act = glu * jax.nn.sigmoid(SWIGLU_ALPHA * glu) * (lin + 1.0)
        return act @ w_down[e] + b_down[e]

    yb = lax.map(run_block, (xb, block_e)).reshape(n_rows, h.shape[1])
    out = jax.ops.segment_sum(yb * row_gate[:, None].astype(yb.dtype), row_tok, num_segments=T + 1)
    return out[:T]


def decoder_layer(x, c, pos0, k_hist, v_hist, conv_hist,
                  w_ada, b_ada, w_in, b_in, conv_w, sinks, norm_attn_g, norm_conv_g, w_out, b_out,
                  ln1_g, ln1_b, w_router, b_router, w_up, b_up, w_down, b_down, ln2_g, ln2_b):
    B, T, _ = x.shape
    L = Q_BLOCK if T % Q_BLOCK == 0 else T
    N = T // L
    mod = (jax.nn.silu(c) @ w_ada + b_ada)[:, None, :]
    shift1, scale1, gate1, shift2, scale2, gate2 = jnp.split(mod, 6, axis=-1)

    u = modulate(x, shift1, scale1)
    z = u @ w_in + b_in
    q, k, v, gate_b, gate_c, xc = jnp.split(z, SPLIT_POINTS, axis=-1)

    q_pos = pos0 + jnp.arange(T, dtype=jnp.int32).reshape(N, L)
    prev_pos = (pos0 - WINDOW + L * jnp.arange(N, dtype=jnp.int32)[:, None]
                + jnp.arange(WINDOW, dtype=jnp.int32)[None, :])
    k_pos = jnp.concatenate([prev_pos, q_pos], axis=1)
    qb = rope(q.reshape(B, N, L, N_HEADS, HEAD_DIM), q_pos)
    kb = rope(k.reshape(B, N, L, N_KV_HEADS, HEAD_DIM), q_pos)
    vb = v.reshape(B, N, L, N_KV_HEADS, HEAD_DIM)
    if N == 1:
        k_prev = k_hist[:, None]
        v_prev = v_hist[:, None]
    else:
        k_prev = jnp.concatenate([k_hist[:, None], kb[:, :-1]], axis=1)
        v_prev = jnp.concatenate([v_hist[:, None], vb[:, :-1]], axis=1)
    attn = window_attention(qb, k_prev, v_prev, kb, vb, q_pos, k_pos, sinks).reshape(B, T, ATTN_DIM)

    conv_out, conv_new = short_conv(gate_c * xc, conv_hist, conv_w)
    conv_y = gate_b * conv_out

    mixed = jnp.concatenate([rms_norm(attn, norm_attn_g), rms_norm(conv_y, norm_conv_g)], axis=-1) @ w_out + b_out
    x = layer_norm(DEEPNORM_ALPHA * x + gate1 * mixed, ln1_g, ln1_b)

    u2 = modulate(x, shift2, scale2)
    ffn = moe_ffn(u2.reshape(B * T, D_MODEL), w_router, b_router, w_up, b_up, w_down, b_down).reshape(B, T, D_MODEL)
    x = layer_norm(DEEPNORM_ALPHA * x + gate2 * ffn, ln2_g, ln2_b)

    k_new = jnp.concatenate([k_hist, kb.reshape(B, T, N_KV_HEADS, HEAD_DIM)], axis=1)[:, -WINDOW:]
    v_new = jnp.concatenate([v_hist, vb.reshape(B, T, N_KV_HEADS, HEAD_DIM)], axis=1)[:, -WINDOW:]
    return x, k_new, v_new, conv_new


def setup_inputs(seed: int = 0) -> dict:
    key = jax.random.key(seed)
    ks = jax.random.split(key, 32)

    def nrm(k, shape, s):
        return jax.random.normal(k, shape, jnp.float32) * s

    Ld = DEPTH
    return {
        "x_prompt": nrm(ks[0], (BATCH, SEQ, D_MODEL), 1.0),
        "x_sample": nrm(ks[1], (DEC_BATCH, DEC_SEQ, D_MODEL), 1.0),
        "cache_k": nrm(ks[2], (DEPTH, DEC_BATCH, WINDOW, N_KV_HEADS, HEAD_DIM), 1.0),
        "cache_v": nrm(ks[3], (DEPTH, DEC_BATCH, WINDOW, N_KV_HEADS, HEAD_DIM), 1.0),
        "state_conv": nrm(ks[4], (DEPTH, DEC_BATCH, CONV_WIDTH - 1, CONV_DIM), 1.0),
        "c_prompt": nrm(ks[5], (BATCH, D_MODEL), 1.0),
        "c_sample": nrm(ks[6], (DEC_BATCH, D_MODEL), 1.0),
        "w_ada": nrm(ks[7], (Ld, D_MODEL, 6 * D_MODEL), 0.5 * D_MODEL ** -0.5),
        "b_ada": nrm(ks[8], (Ld, 6 * D_MODEL), 0.02),
        "w_in": nrm(ks[9], (Ld, D_MODEL, IN_DIM), D_MODEL ** -0.5),
        "b_in": nrm(ks[10], (Ld, IN_DIM), 0.02),
        "conv_w": nrm(ks[11], (Ld, CONV_WIDTH, CONV_DIM), CONV_WIDTH ** -0.5),
        "sinks": nrm(ks[12], (Ld, N_HEADS), 1.0),
        "norm_attn_g": 1.0 + nrm(ks[13], (Ld, ATTN_DIM), 0.02),
        "norm_conv_g": 1.0 + nrm(ks[14], (Ld, CONV_DIM), 0.02),
        "w_out": nrm(ks[15], (Ld, D_MODEL, D_MODEL), DEEPNORM_BETA * D_MODEL ** -0.5),
        "b_out": nrm(ks[16], (Ld, D_MODEL), 0.02),
        "ln1_g": 1.0 + nrm(ks[17], (Ld, D_MODEL), 0.02),
        "ln1_b": nrm(ks[18], (Ld, D_MODEL), 0.02),
        "w_router": nrm(ks[19], (Ld, D_MODEL, N_EXPERTS), D_MODEL ** -0.5),
        "b_router": nrm(ks[20], (Ld, N_EXPERTS), 0.01),
        "w_up": nrm(ks[21], (Ld, N_EXPERTS, D_MODEL, 2 * D_FF), D_MODEL ** -0.5),
        "b_up": nrm(ks[22], (Ld, N_EXPERTS, 2 * D_FF), 0.02),
        "w_down": nrm(ks[23], (Ld, N_EXPERTS, D_FF, D_MODEL), DEEPNORM_BETA * D_FF ** -0.5),
        "b_down": nrm(ks[24], (Ld, N_EXPERTS, D_MODEL), 0.02),
        "ln2_g": 1.0 + nrm(ks[25], (Ld, D_MODEL), 0.02),
        "ln2_b": nrm(ks[26], (Ld, D_MODEL), 0.02),
    }


def reference(x_prompt, x_sample, cache_k, cache_v, state_conv, c_prompt, c_sample,
              w_ada, b_ada, w_in, b_in, conv_w, sinks, norm_attn_g, norm_conv_g, w_out, b_out,
              ln1_g, ln1_b, w_router, b_router, w_up, b_up, w_down, b_down, ln2_g, ln2_b):
    layer_weights = (w_ada, b_ada, w_in, b_in, conv_w, sinks, norm_attn_g, norm_conv_g, w_out, b_out,
                     ln1_g, ln1_b, w_router, b_router, w_up, b_up, w_down, b_down, ln2_g, ln2_b)
    B = x_prompt.shape[0]
    zero_kv = jnp.zeros((B, WINDOW, N_KV_HEADS, HEAD_DIM), x_prompt.dtype)
    zero_conv = jnp.zeros((B, CONV_WIDTH - 1, CONV_DIM), x_prompt.dtype)
    y_prompt, y_sample = x_prompt, x_sample
    kp, vp, cp, ksm, vsm, csm = [], [], [], [], [], []
    for layer in range(DEPTH):
        lw = [w[layer] for w in layer_weights]
        y_prompt, k_a, v_a, c_a = decoder_layer(y_prompt, c_prompt, 0, zero_kv, zero_kv, zero_conv, *lw)
        y_sample, k_b, v_b, c_b = decoder_layer(y_sample, c_sample, PAST_LEN, cache_k[layer], cache_v[layer],
                                                state_conv[layer], *lw)
        kp.append(k_a); vp.append(v_a); cp.append(c_a)
        ksm.append(k_b); vsm.append(v_b); csm.append(c_b)
    new_k_prompt = jnp.stack(kp)
    new_v_prompt = jnp.stack(vp)
    new_conv_prompt = jnp.stack(cp)
    new_k_sample = jnp.stack(ksm)
    new_v_sample = jnp.stack(vsm)
    new_conv_sample = jnp.stack(csm)
    return (y_prompt, y_sample, new_k_prompt, new_v_prompt, new_conv_prompt, new_k_sample, new_v_sample, new_conv_sample)
```

```python
import functools

import jax
import jax.numpy as jnp
from jax import lax
from jax.experimental import pallas as pl
from jax.experimental.pallas import tpu as pltpu

F32 = jnp.float32
BF16 = jnp.bfloat16

D_MODEL = 2048
HEAD_DIM = 64
N_HEADS = 16
N_KV_HEADS = 4
GQA_GROUP = N_HEADS // N_KV_HEADS
ATTN_DIM = N_HEADS * HEAD_DIM
KV_DIM = N_KV_HEADS * HEAD_DIM
CONV_DIM = D_MODEL - ATTN_DIM
CONV_WIDTH = 3
IN_DIM = ATTN_DIM + 2 * KV_DIM + 3 * CONV_DIM
WINDOW = 128
PAST_LEN = 16384
ROPE_THETA = 10000.0
N_EXPERTS = 32
TOP_K = 4
D_FF = D_MODEL
SWIGLU_LIMIT = 7.0
SWIGLU_ALPHA = 1.702
DEPTH = 1
DEEPNORM_ALPHA = (2.0 * DEPTH) ** 0.25
LN_EPS = 1e-5
RMS_EPS = 1e-6
COL_K = ATTN_DIM
COL_V = COL_K + KV_DIM
COL_B = COL_V + KV_DIM
COL_C = COL_B + CONV_DIM
COL_X = COL_C + CONV_DIM

LANES = 128
SUBLANES = 8
VMEM_LIMIT_BYTES = 60 * 1024 * 1024

TOKEN_TILE = 128
ADA_TILE_N = 1024
INPROJ_TILE_M = 512
INPROJ_CHUNK_N = 512
MIX_TILE_M = 256
DEC_TILE_B = 16
ROW_BLOCK = 256
ROW_WINDOW = 1024
FF_TILE = 256
NEG_BIG = -1e30


def _params(n_axes, vmem=VMEM_LIMIT_BYTES):
    return pltpu.CompilerParams(dimension_semantics=("arbitrary",) * n_axes, vmem_limit_bytes=vmem)


def _standardise(x):
    mu = jnp.mean(x, axis=-1, keepdims=True)
    xc = x - mu
    var = jnp.mean(xc * xc, axis=-1, keepdims=True)
    return xc * lax.rsqrt(var + LN_EPS)


def _rms(x, g):
    return x * lax.rsqrt(jnp.mean(x * x, axis=-1, keepdims=True) + RMS_EPS) * g


def _ada_kernel(c_ref, w_ref, b_ref, o_ref):
    c = c_ref[...]
    s = (c * jax.nn.sigmoid(c)).astype(BF16)
    o_ref[...] = jnp.dot(s, w_ref[...].astype(BF16), preferred_element_type=F32) + b_ref[...]


def _ada(c_all, w_ada, b_ada):
    rows = c_all.shape[0]
    n_out = w_ada.shape[1]
    return pl.pallas_call(
        _ada_kernel,
        grid=(n_out // ADA_TILE_N,),
        in_specs=[
            pl.BlockSpec((rows, D_MODEL), lambda j: (0, 0)),
            pl.BlockSpec((D_MODEL, ADA_TILE_N), lambda j: (0, j)),
            pl.BlockSpec((1, ADA_TILE_N), lambda j: (0, j)),
        ],
        out_specs=pl.BlockSpec((rows, ADA_TILE_N), lambda j: (0, j)),
        out_shape=jax.ShapeDtypeStruct((rows, n_out), F32),
        compiler_params=_params(1),
        name="ada",
    )(c_all, w_ada, b_ada.reshape(1, n_out))


def _inproj_kernel(x_ref, shift_ref, scale_ref, cos_ref, sin_ref, w_ref, b_ref,
                   q_ref, k_ref, v_ref, gb_ref, uc_ref):
    u = (_standardise(x_ref[...]) * (1.0 + scale_ref[...]) + shift_ref[...]).astype(BF16)
    cos = cos_ref[...]
    sin = sin_ref[...]
    lane = lax.broadcasted_iota(jnp.int32, (1, LANES), 1)
    first_half = (lane % HEAD_DIM) < (HEAD_DIM // 2)

    def rope(z):
        partner = jnp.where(first_half, pltpu.roll(z, LANES - HEAD_DIM // 2, axis=1),
                            pltpu.roll(z, HEAD_DIM // 2, axis=1))
        return z * cos + partner * sin

    def proj(c0):
        w = w_ref[:, c0:c0 + INPROJ_CHUNK_N]
        return jnp.dot(u, w, preferred_element_type=F32) + b_ref[:, c0:c0 + INPROJ_CHUNK_N]

    groups = INPROJ_CHUNK_N // LANES
    for j in range(ATTN_DIM // INPROJ_CHUNK_N):
        z = proj(j * INPROJ_CHUNK_N)
        for g in range(groups):
            c0 = j * INPROJ_CHUNK_N + g * LANES
            q_ref[:, c0:c0 + LANES] = (rope(z[:, g * LANES:(g + 1) * LANES]) * (HEAD_DIM ** -0.5)).astype(BF16)
    z = proj(COL_K)
    for g in range(KV_DIM // LANES):
        k_ref[:, g * LANES:(g + 1) * LANES] = rope(z[:, g * LANES:(g + 1) * LANES])
    v_ref[...] = z[:, KV_DIM:2 * KV_DIM]
    for j in range(CONV_DIM // INPROJ_CHUNK_N):
        sl = slice(j * INPROJ_CHUNK_N, (j + 1) * INPROJ_CHUNK_N)
        gb_ref[:, sl] = proj(COL_B + j * INPROJ_CHUNK_N)
        uc_ref[:, sl] = proj(COL_C + j * INPROJ_CHUNK_N) * proj(COL_X + j * INPROJ_CHUNK_N)


def _inproj(x, shift, scale, cos, sin, w_in_bf, b_in, tm):
    T = x.shape[0]
    per_row_mod = shift.shape[0] != 1
    per_row_pos = cos.shape[0] != 1
    mod_spec = pl.BlockSpec((tm, D_MODEL), lambda i: (i, 0)) if per_row_mod else pl.BlockSpec((1, D_MODEL), lambda i: (0, 0))
    pos_spec = pl.BlockSpec((tm, LANES), lambda i: (i, 0)) if per_row_pos else pl.BlockSpec((1, LANES), lambda i: (0, 0))

    def row_spec(width):
        return pl.BlockSpec((tm, width), lambda i: (i, 0))

    return pl.pallas_call(
        _inproj_kernel,
        grid=(T // tm,),
        in_specs=[
            row_spec(D_MODEL), mod_spec, mod_spec, pos_spec, pos_spec,
            pl.BlockSpec((D_MODEL, IN_DIM), lambda i: (0, 0), pipeline_mode=pl.Buffered(1)),
            pl.BlockSpec((1, IN_DIM), lambda i: (0, 0)),
        ],
        out_specs=[row_spec(ATTN_DIM), row_spec(KV_DIM), row_spec(KV_DIM), row_spec(CONV_DIM), row_spec(CONV_DIM)],
        out_shape=[
            jax.ShapeDtypeStruct((T, ATTN_DIM), BF16),
            jax.ShapeDtypeStruct((T, KV_DIM), F32),
            jax.ShapeDtypeStruct((T, KV_DIM), F32),
            jax.ShapeDtypeStruct((T, CONV_DIM), F32),
            jax.ShapeDtypeStruct((T, CONV_DIM), F32),
        ],
        compiler_params=_params(1),
        name="inproj",
    )(x, shift, scale, cos, sin, w_in_bf, b_in.reshape(1, IN_DIM))


def _attn_kernel(q_ref, kp_ref, kc_ref, vp_ref, vc_ref, sink_ref, g_ref, o_ref):
    n = pl.program_id(0)
    q = q_ref[...]
    k = jnp.concatenate([kp_ref[...], kc_ref[...]], axis=0).astype(BF16)
    v = jnp.concatenate([vp_ref[...], vc_ref[...]], axis=0).astype(BF16)
    qi = lax.broadcasted_iota(jnp.int32, (WINDOW, 2 * WINDOW), 0)
    ks = lax.broadcasted_iota(jnp.int32, (WINDOW, 2 * WINDOW), 1)
    allowed = (ks >= qi) & (ks <= qi + WINDOW) & ((ks >= WINDOW) | (n > 0))
    outs = []
    for h in range(N_HEADS):
        g = h // GQA_GROUP
        qh = q[:, h * HEAD_DIM:(h + 1) * HEAD_DIM]
        kg = k[:, g * HEAD_DIM:(g + 1) * HEAD_DIM]
        vg = v[:, g * HEAD_DIM:(g + 1) * HEAD_DIM]
        s = lax.dot_general(qh, kg, (((1,), (1,)), ((), ())), preferred_element_type=F32)
        s = jnp.where(allowed, s, NEG_BIG)
        sink = sink_ref[:, h:h + 1]
        m = jnp.maximum(jnp.max(s, axis=1, keepdims=True), sink)
        e = jnp.exp(s - m)
        den = jnp.sum(e, axis=1, keepdims=True) + jnp.exp(sink - m)
        p = (e / den).astype(BF16)
        outs.append(jnp.dot(p, vg, preferred_element_type=F32))
    o = jnp.concatenate(outs, axis=1)
    o_ref[...] = _rms(o, g_ref[...]).astype(BF16)


def _attn_prompt(q, k, v, sinks, norm_g):
    T = q.shape[0]

    def cur(width):
        return pl.BlockSpec((WINDOW, width), lambda n: (n, 0))

    def prev(width):
        return pl.BlockSpec((WINDOW, width), lambda n: (jnp.maximum(n - 1, 0), 0))

    return pl.pallas_call(
        _attn_kernel,
        grid=(T // WINDOW,),
        in_specs=[
            cur(ATTN_DIM), prev(KV_DIM), cur(KV_DIM), prev(KV_DIM), cur(KV_DIM),
            pl.BlockSpec((1, N_HEADS), lambda n: (0, 0)),
            pl.BlockSpec((1, ATTN_DIM), lambda n: (0, 0)),
        ],
        out_specs=cur(ATTN_DIM),
        out_shape=jax.ShapeDtypeStruct((T, ATTN_DIM), BF16),
        compiler_params=_params(1),
        name="attn",
    )(q, k, k, v, v, sinks.reshape(1, N_HEADS), norm_g.reshape(1, ATTN_DIM))


def _attn_dec_kernel(q_ref, kn_ref, vn_ref, ck_ref, cv_ref, sink_ref, g_ref, o_ref, nk_ref, nv_ref):
    tb = q_ref.shape[0]
    q = q_ref[...].astype(F32)
    lane_group = lax.broadcasted_iota(jnp.int32, (1, N_HEADS, KV_DIM), 2) // HEAD_DIM
    head_group = lax.broadcasted_iota(jnp.int32, (1, N_HEADS, KV_DIM), 1) // GQA_GROUP
    own = lane_group == head_group
    qe = jnp.where(own, jnp.concatenate([q] * N_KV_HEADS, axis=2), 0.0)
    ck = ck_ref[...]
    cv = cv_ref[...]
    kn = kn_ref[...]
    vn = vn_ref[...]
    s = jnp.einsum("bhc,bwc->bhw", qe.astype(BF16), ck.astype(BF16), preferred_element_type=F32)
    s_new = jnp.sum(qe.astype(BF16).astype(F32) * kn.astype(BF16).astype(F32), axis=2, keepdims=True)
    sink = sink_ref[...]
    m = jnp.maximum(jnp.maximum(jnp.max(s, axis=2, keepdims=True), s_new), sink)
    e = jnp.exp(s - m)
    e_new = jnp.exp(s_new - m)
    den = jnp.sum(e, axis=2, keepdims=True) + e_new + jnp.exp(sink - m)
    p = (e / den).astype(BF16)
    p_new = (e_new / den).astype(BF16).astype(F32)
    o = jnp.einsum("bhw,bwc->bhc", p, cv.astype(BF16), preferred_element_type=F32)
    o = o + p_new * vn.astype(BF16).astype(F32)
    o = jnp.where(own, o, 0.0)
    oh = o[:, :, 0:HEAD_DIM]
    for g in range(1, N_KV_HEADS):
        oh = oh + o[:, :, g * HEAD_DIM:(g + 1) * HEAD_DIM]
    ms = jnp.sum(jnp.sum(oh * oh, axis=2, keepdims=True), axis=1, keepdims=True) / ATTN_DIM
    o_ref[...] = (oh * lax.rsqrt(ms + RMS_EPS) * g_ref[...]).astype(BF16)
    row = lax.broadcasted_iota(jnp.int32, (1, WINDOW, 1), 1)
    nk_ref[...] = jnp.where(row == WINDOW - 1, kn, pltpu.roll(ck, WINDOW - 1, axis=1))
    nv_ref[...] = jnp.where(row == WINDOW - 1, vn, pltpu.roll(cv, WINDOW - 1, axis=1))


def _attn_decode(q, k_new, v_new, cache_k, cache_v, sinks, norm_g):
    B = q.shape[0]
    tb = DEC_TILE_B

    def b3(d1, d2):
        return pl.BlockSpec((tb, d1, d2), lambda i: (i, 0, 0))

    return pl.pallas_call(
        _attn_dec_kernel,
        grid=(B // tb,),
        in_specs=[
            b3(N_HEADS, HEAD_DIM), b3(1, KV_DIM), b3(1, KV_DIM), b3(WINDOW, KV_DIM), b3(WINDOW, KV_DIM),
            pl.BlockSpec((1, N_HEADS, 1), lambda i: (0, 0, 0)),
            pl.BlockSpec((1, N_HEADS, HEAD_DIM), lambda i: (0, 0, 0)),
        ],
        out_specs=[b3(N_HEADS, HEAD_DIM), b3(WINDOW, KV_DIM), b3(WINDOW, KV_DIM)],
        out_shape=[
            jax.ShapeDtypeStruct((B, N_HEADS, HEAD_DIM), BF16),
            jax.ShapeDtypeStruct((B, WINDOW, KV_DIM), F32),
            jax.ShapeDtypeStruct((B, WINDOW, KV_DIM), F32),
        ],
        compiler_params=_params(1),
        name="attn_dec",
    )(q.reshape(B, N_HEADS, HEAD_DIM), k_new.reshape(B, 1, KV_DIM), v_new.reshape(B, 1, KV_DIM),
      cache_k.reshape(B, WINDOW, KV_DIM), cache_v.reshape(B, WINDOW, KV_DIM),
      sinks.reshape(1, N_HEADS, 1), norm_g.reshape(1, N_HEADS, HEAD_DIM))


def _mix_tail(attn, gb, uc, um1, um2, cw_ref, gconv_ref, wout_ref, bout_ref, x_ref, gate1_ref, ln1g_ref, ln1b_ref,
              shift2_ref, scale2_ref, wr_ref, br_ref, x1_ref, u2_ref, lg_ref):
    cw = cw_ref[...]
    conv = cw[0:1, :] * um2 + cw[1:2, :] * um1 + cw[2:3, :] * uc
    conv_n = _rms(gb * conv, gconv_ref[...]).astype(BF16)
    mixed = (jnp.dot(attn, wout_ref[0:ATTN_DIM, :], preferred_element_type=F32)
             + jnp.dot(conv_n, wout_ref[ATTN_DIM:D_MODEL, :], preferred_element_type=F32) + bout_ref[...])
    x1 = _standardise(DEEPNORM_ALPHA * x_ref[...] + gate1_ref[...] * mixed) * ln1g_ref[...] + ln1b_ref[...]
    x1_ref[...] = x1
    u2 = _standardise(x1) * (1.0 + scale2_ref[...]) + shift2_ref[...]
    u2_ref[...] = u2
    lg_ref[...] = lax.dot_general(wr_ref[...], u2.astype(BF16), (((1,), (1,)), ((), ())),
                                  preferred_element_type=F32) + br_ref[...]


def _mix_seq_kernel(attn_ref, gb_ref, uc_ref, halo_ref, hist_ref, *rest):
    i = pl.program_id(0)
    uc = uc_ref[...]
    tm = uc.shape[0]
    above = jnp.where(i == 0, hist_ref[...], halo_ref[...])
    row = lax.broadcasted_iota(jnp.int32, (tm, 1), 0)
    um1 = jnp.where(row == 0, above[7:8, :], pltpu.roll(uc, 1, axis=0))
    um2 = jnp.where(row == 0, above[6:7, :], jnp.where(row == 1, above[7:8, :], pltpu.roll(uc, 2, axis=0)))
    _mix_tail(attn_ref[...], gb_ref[...], uc, um1, um2, *rest)


def _mix_tok_kernel(attn_ref, gb_ref, uc_ref, um1_ref, um2_ref, *rest):
    _mix_tail(attn_ref[...], gb_ref[...], uc_ref[...], um1_ref[...], um2_ref[...], *rest)


def _mix(attn_n, gb, uc, conv_prev, conv_w, norm_conv_g, w_out_bf, b_out, x, gate1, ln1_g, ln1_b, shift2, scale2,
         w_router_t_bf, b_router, tm, sequential):
    T = x.shape[0]
    per_row_mod = gate1.shape[0] != 1

    def row_spec(width):
        return pl.BlockSpec((tm, width), lambda i: (i, 0))

    def const_spec(rows, width):
        return pl.BlockSpec((rows, width), lambda i: (0, 0))

    mod_spec = row_spec(D_MODEL) if per_row_mod else const_spec(1, D_MODEL)
    if sequential:
        hist8 = jnp.concatenate([jnp.zeros((SUBLANES - 2, CONV_DIM), F32), conv_prev], axis=0)
        halo_blocks = tm // SUBLANES
        conv_specs = [pl.BlockSpec((SUBLANES, CONV_DIM), lambda i: (jnp.maximum(i * halo_blocks - 1, 0), 0)),
                      const_spec(SUBLANES, CONV_DIM)]
        conv_args = (uc, hist8)
        body = _mix_seq_kernel
    else:
        conv_specs = [row_spec(CONV_DIM), row_spec(CONV_DIM)]
        conv_args = conv_prev
        body = _mix_tok_kernel
    return pl.pallas_call(
        body,
        grid=(T // tm,),
        in_specs=[row_spec(ATTN_DIM), row_spec(CONV_DIM), row_spec(CONV_DIM), *conv_specs,
                  const_spec(CONV_WIDTH, CONV_DIM), const_spec(1, CONV_DIM),
                  const_spec(D_MODEL, D_MODEL), const_spec(1, D_MODEL),
                  row_spec(D_MODEL), mod_spec, const_spec(1, D_MODEL), const_spec(1, D_MODEL),
                  mod_spec, mod_spec,
                  const_spec(N_EXPERTS, D_MODEL), const_spec(N_EXPERTS, 1)],
        out_specs=[row_spec(D_MODEL), row_spec(D_MODEL), pl.BlockSpec((N_EXPERTS, tm), lambda i: (0, i))],
        out_shape=[jax.ShapeDtypeStruct((T, D_MODEL), F32), jax.ShapeDtypeStruct((T, D_MODEL), F32),
                   jax.ShapeDtypeStruct((N_EXPERTS, T), F32)],
        compiler_params=_params(1),
        name="mix_seq" if sequential else "mix_tok",
    )(attn_n, gb, uc, *conv_args, conv_w, norm_conv_g.reshape(1, CONV_DIM), w_out_bf, b_out.reshape(1, D_MODEL),
      x, gate1, ln1_g.reshape(1, D_MODEL), ln1_b.reshape(1, D_MODEL), shift2, scale2,
      w_router_t_bf, b_router.reshape(N_EXPERTS, 1))


def _route_kernel(lg_ref, eidx_ref, gate_ref, rank_ref, cnt_ref):
    n_tok = lg_ref.shape[1]
    r = lax.broadcasted_iota(jnp.int32, (TOKEN_TILE, TOKEN_TILE), 0)
    c = lax.broadcasted_iota(jnp.int32, (TOKEN_TILE, TOKEN_TILE), 1)
    earlier = (r < c).astype(BF16)
    expert = lax.broadcasted_iota(jnp.int32, (N_EXPERTS, TOKEN_TILE), 0).astype(F32)

    def body(ci, count):
        off = pl.multiple_of(ci * TOKEN_TILE, TOKEN_TILE)
        l = lg_ref[:, pl.ds(off, TOKEN_TILE)]
        vals, idxs, sels = [], [], []
        for _ in range(TOP_K):
            m = jnp.max(l, axis=0, keepdims=True)
            idx = jnp.min(jnp.where(l == m, expert, float(N_EXPERTS)), axis=0, keepdims=True)
            sel = expert == idx
            vals.append(m)
            idxs.append(idx)
            sels.append(sel)
            l = jnp.where(sel, -jnp.inf, l)
        chosen = jnp.where(sels[0] | sels[1] | sels[2] | sels[3], 1.0, 0.0)
        before = jnp.dot(chosen.astype(BF16), earlier, preferred_element_type=F32) + count
        ranks = [jnp.sum(jnp.where(s, before, 0.0), axis=0, keepdims=True) for s in sels]
        es = [jnp.exp(v - vals[0]) for v in vals]
        den = es[0] + es[1] + es[2] + es[3]
        eidx_ref[:, pl.ds(off, TOKEN_TILE)] = jnp.concatenate(idxs, axis=0).astype(jnp.int32)
        rank_ref[:, pl.ds(off, TOKEN_TILE)] = jnp.concatenate(ranks, axis=0).astype(jnp.int32)
        gate_ref[:, pl.ds(off, TOKEN_TILE)] = jnp.concatenate([e / den for e in es], axis=0)
        return count + jnp.sum(chosen, axis=1, keepdims=True)

    count = lax.fori_loop(0, n_tok // TOKEN_TILE, body, jnp.zeros((N_EXPERTS, 1), F32))
    cnt_ref[...] = jnp.broadcast_to(count, (N_EXPERTS, LANES)).astype(jnp.int32)


def _route(logits_t):
    n_tok = logits_t.shape[1]
    return pl.pallas_call(
        _route_kernel,
        out_shape=[jax.ShapeDtypeStruct((TOP_K, n_tok), jnp.int32), jax.ShapeDtypeStruct((TOP_K, n_tok), F32),
                   jax.ShapeDtypeStruct((TOP_K, n_tok), jnp.int32), jax.ShapeDtypeStruct((N_EXPERTS, LANES), jnp.int32)],
        compiler_params=pltpu.CompilerParams(vmem_limit_bytes=VMEM_LIMIT_BYTES),
        name="route",
    )(logits_t)


def _dest_kernel(start_ref, eidx_ref, rank_ref, dest_ref):
    e = eidx_ref[...]
    base = jnp.zeros(e.shape, jnp.int32)
    for x in range(N_EXPERTS):
        base = jnp.where(e == x, start_ref[x], base)
    dest_ref[...] = base + rank_ref[...]


def _dest(group_start, eidx, rank):
    return pl.pallas_call(
        _dest_kernel,
        in_specs=[pl.BlockSpec(memory_space=pltpu.SMEM), pl.BlockSpec(memory_space=pltpu.VMEM),
                  pl.BlockSpec(memory_space=pltpu.VMEM)],
        out_specs=pl.BlockSpec(memory_space=pltpu.VMEM),
        out_shape=jax.ShapeDtypeStruct(eidx.shape, jnp.int32),
        name="dest",
    )(group_start, eidx, rank)


def _row_wait(src_ref, dst_ref, sem, n_tiles):
    for _ in range(n_tiles):
        pltpu.make_async_copy(src_ref.at[pl.ds(0, TOKEN_TILE), :], dst_ref.at[pl.ds(0, TOKEN_TILE), :], sem).wait()


def _dispatch_kernel(n_prompt_tiles, gend_ref, gsize_ref, dest_ref, up_ref, us_ref, xs_ref, zero_buf, zsem, sem):
    i = pl.program_id(0)

    @pl.when(i == 0)
    def _():
        zero_buf[...] = jnp.zeros_like(zero_buf)
        for e in range(N_EXPERTS):
            @pl.when(gsize_ref[e] > 0)
            def _():
                r0 = pl.multiple_of(gend_ref[e] - ROW_BLOCK, ROW_BLOCK)
                pltpu.make_async_copy(zero_buf, xs_ref.at[pl.ds(r0, ROW_BLOCK), :], zsem).start()
        for e in range(N_EXPERTS):
            @pl.when(gsize_ref[e] > 0)
            def _():
                pltpu.make_async_copy(zero_buf, xs_ref.at[pl.ds(0, ROW_BLOCK), :], zsem).wait()

    def scatter(src_ref):
        def body(j, carry):
            t = jnp.bitwise_and(j, TOKEN_TILE - 1)
            d = dest_ref[0, 0, j]
            pltpu.make_async_copy(src_ref.at[pl.ds(t, 1), :], xs_ref.at[pl.ds(d, 1), :], sem).start()
            return carry
        lax.fori_loop(0, TOP_K * TOKEN_TILE, body, 0, unroll=8)
        _row_wait(src_ref, xs_ref, sem, TOP_K)

    @pl.when(i < n_prompt_tiles)
    def _():
        scatter(up_ref)

    @pl.when(i >= n_prompt_tiles)
    def _():
        scatter(us_ref)


def _dispatch(group_end, group_size, dest_tiles, u2_p, u2_s, n_rows):
    n_p = u2_p.shape[0] // TOKEN_TILE
    n_s = u2_s.shape[0] // TOKEN_TILE
    grid_spec = pltpu.PrefetchScalarGridSpec(
        num_scalar_prefetch=2,
        grid=(n_p + n_s,),
        in_specs=[
            pl.BlockSpec((1, 1, TOP_K * TOKEN_TILE), lambda i, *_: (i, 0, 0), memory_space=pltpu.SMEM),
            pl.BlockSpec((TOKEN_TILE, D_MODEL), lambda i, *_: (jnp.minimum(i, n_p - 1), 0)),
            pl.BlockSpec((TOKEN_TILE, D_MODEL), lambda i, *_: (jnp.maximum(i - n_p, 0), 0)),
        ],
        out_specs=pl.BlockSpec(memory_space=pl.ANY),
        scratch_shapes=[pltpu.VMEM((ROW_BLOCK, D_MODEL), F32), pltpu.SemaphoreType.DMA(()), pltpu.SemaphoreType.DMA(())],
    )
    return pl.pallas_call(
        functools.partial(_dispatch_kernel, n_p),
        grid_spec=grid_spec,
        out_shape=jax.ShapeDtypeStruct((n_rows, D_MODEL), F32),
        compiler_params=_params(1),
        name="dispatch",
    )(group_end, group_size, dest_tiles, u2_p, u2_s)


def _experts_kernel(win_ref, blk0_ref, nblk_ref, exp_ref, fidx_ref, x_ref, wg_ref, wl_ref, wd_ref, bg_ref, bl_ref, bd_ref,
                    o_ref, xb, wgb, wlb, wdb):
    w = pl.program_id(0)
    f = pl.program_id(1)
    nblk = nblk_ref[w]
    blk0 = blk0_ref[w]

    @pl.when(nblk > 0)
    def _():
        wgb[...] = wg_ref[...].astype(BF16)
        wlb[...] = wl_ref[...].astype(BF16)
        wdb[...] = wd_ref[...].astype(BF16)

        def body(j, carry):
            r = pl.multiple_of((blk0 + j) * ROW_BLOCK, ROW_BLOCK)
            rows = pl.ds(r, ROW_BLOCK)

            @pl.when(f == 0)
            def _():
                xb[rows, :] = x_ref[rows, :].astype(BF16)

            x = xb[rows, :]
            glu = jnp.minimum(jnp.dot(x, wgb[...], preferred_element_type=F32) + bg_ref[...], SWIGLU_LIMIT)
            lin = jnp.clip(jnp.dot(x, wlb[...], preferred_element_type=F32) + bl_ref[...], -SWIGLU_LIMIT, SWIGLU_LIMIT)
            act = glu * jax.nn.sigmoid(SWIGLU_ALPHA * glu) * (lin + 1.0)
            y = jnp.dot(act.astype(BF16), wdb[...], preferred_element_type=F32)

            @pl.when(f == 0)
            def _():
                o_ref[rows, :] = y + bd_ref[...]

            @pl.when(f > 0)
            def _():
                o_ref[rows, :] += y

            return carry

        lax.fori_loop(0, nblk, body, 0)


def _experts(items, xs, w_up, b_up, w_down, b_down):
    win, blk0, nblk, exp, fidx = items
    n_items = win.shape[0]
    n_rows = xs.shape[0]
    n_f = D_FF // FF_TILE

    def fi(w, f, fidx_ref):
        return fidx_ref[w * n_f + f]

    grid_spec = pltpu.PrefetchScalarGridSpec(
        num_scalar_prefetch=5,
        grid=(n_items, n_f),
        in_specs=[
            pl.BlockSpec((ROW_WINDOW, D_MODEL), lambda w, f, win, b0, nb, ex, fx: (win[w], 0)),
            pl.BlockSpec((None, D_MODEL, FF_TILE), lambda w, f, win, b0, nb, ex, fx: (ex[w], 0, fi(w, f, fx))),
            pl.BlockSpec((None, D_MODEL, FF_TILE), lambda w, f, win, b0, nb, ex, fx: (ex[w], 0, n_f + fi(w, f, fx))),
            pl.BlockSpec((None, FF_TILE, D_MODEL), lambda w, f, win, b0, nb, ex, fx: (ex[w], fi(w, f, fx), 0)),
            pl.BlockSpec((None, 1, FF_TILE), lambda w, f, win, b0, nb, ex, fx: (ex[w], 0, fi(w, f, fx))),
            pl.BlockSpec((None, 1, FF_TILE), lambda w, f, win, b0, nb, ex, fx: (ex[w], 0, n_f + fi(w, f, fx))),
            pl.BlockSpec((None, 1, D_MODEL), lambda w, f, win, b0, nb, ex, fx: (ex[w], 0, 0)),
        ],
        out_specs=pl.BlockSpec((ROW_WINDOW, D_MODEL), lambda w, f, win, b0, nb, ex, fx: (win[w], 0)),
        scratch_shapes=[pltpu.VMEM((ROW_WINDOW, D_MODEL), BF16), pltpu.VMEM((D_MODEL, FF_TILE), BF16),
                        pltpu.VMEM((D_MODEL, FF_TILE), BF16), pltpu.VMEM((FF_TILE, D_MODEL), BF16)],
    )
    return pl.pallas_call(
        _experts_kernel,
        grid_spec=grid_spec,
        out_shape=jax.ShapeDtypeStruct((n_rows, D_MODEL), F32),
        compiler_params=_params(2),
        name="experts",
    )(win, blk0, nblk, exp, fidx, xs, w_up, w_up, w_down,
      b_up.reshape(N_EXPERTS, 1, 2 * D_FF), b_up.reshape(N_EXPERTS, 1, 2 * D_FF), b_down.reshape(N_EXPERTS, 1, D_MODEL))


def _combine_kernel(n_prompt_tiles, dcur_ref, dnxt_ref, gates_ref, x1p_ref, x1s_ref, g2p_ref, g2s_ref, lng_ref, lnb_ref,
                    ys_ref, yp_ref, ysm_ref, buf, sem):
    i = pl.program_id(0)
    n = pl.num_programs(0)
    slot = i % 2

    def gather(idx_ref, s):
        def body(j, carry):
            k = lax.shift_right_logical(j, TOKEN_TILE.bit_length() - 1)
            t = jnp.bitwise_and(j, TOKEN_TILE - 1)
            d = idx_ref[0, 0, j]
            pltpu.make_async_copy(ys_ref.at[pl.ds(d, 1), :], buf.at[s, k, pl.ds(t, 1), :], sem.at[s]).start()
            return carry
        lax.fori_loop(0, TOP_K * TOKEN_TILE, body, 0, unroll=8)

    @pl.when(i == 0)
    def _():
        gather(dcur_ref, 0)

    @pl.when(i + 1 < n)
    def _():
        gather(dnxt_ref, 1 - slot)

    for k in range(TOP_K):
        pltpu.make_async_copy(ys_ref.at[pl.ds(0, TOKEN_TILE), :], buf.at[slot, k], sem.at[slot]).wait()

    gates = gates_ref[...]
    ffn = buf[slot, 0] * gates[:, 0:1]
    for k in range(1, TOP_K):
        ffn = ffn + buf[slot, k] * gates[:, k:k + 1]

    def finish(x1, gate2, out_ref):
        out_ref[...] = _standardise(DEEPNORM_ALPHA * x1 + gate2 * ffn) * lng_ref[...] + lnb_ref[...]

    @pl.when(i < n_prompt_tiles)
    def _():
        finish(x1p_ref[...], g2p_ref[...], yp_ref)

    @pl.when(i >= n_prompt_tiles)
    def _():
        finish(x1s_ref[...], g2s_ref[...], ysm_ref)


def _combine(dest_tiles, gates_tok, x1_p, x1_s, gate2_p, gate2_s, ln2_g, ln2_b, ys):
    n_p = x1_p.shape[0] // TOKEN_TILE
    n_s = x1_s.shape[0] // TOKEN_TILE
    n = n_p + n_s

    def p_idx(i):
        return jnp.minimum(i, n_p - 1)

    def s_idx(i):
        return jnp.maximum(i - n_p, 0)

    smem_tile = (1, 1, TOP_K * TOKEN_TILE)
    return pl.pallas_call(
        functools.partial(_combine_kernel, n_p),
        grid=(n,),
        in_specs=[
            pl.BlockSpec(smem_tile, lambda i: (i, 0, 0), memory_space=pltpu.SMEM),
            pl.BlockSpec(smem_tile, lambda i: (jnp.minimum(i + 1, n - 1), 0, 0), memory_space=pltpu.SMEM),
            pl.BlockSpec((TOKEN_TILE, TOP_K), lambda i: (i, 0)),
            pl.BlockSpec((TOKEN_TILE, D_MODEL), lambda i: (p_idx(i), 0)),
            pl.BlockSpec((TOKEN_TILE, D_MODEL), lambda i: (s_idx(i), 0)),
            pl.BlockSpec((1, D_MODEL), lambda i: (0, 0)),
            pl.BlockSpec((TOKEN_TILE, D_MODEL), lambda i: (s_idx(i), 0)),
            pl.BlockSpec((1, D_MODEL), lambda i: (0, 0)),
            pl.BlockSpec((1, D_MODEL), lambda i: (0, 0)),
            pl.BlockSpec(memory_space=pl.ANY),
        ],
        out_specs=[pl.BlockSpec((TOKEN_TILE, D_MODEL), lambda i: (p_idx(i), 0)),
                   pl.BlockSpec((TOKEN_TILE, D_MODEL), lambda i: (s_idx(i), 0))],
        out_shape=[jax.ShapeDtypeStruct(x1_p.shape, F32), jax.ShapeDtypeStruct(x1_s.shape, F32)],
        scratch_shapes=[pltpu.VMEM((2, TOP_K, TOKEN_TILE, D_MODEL), F32), pltpu.SemaphoreType.DMA((2,))],
        compiler_params=_params(1),
        name="combine",
    )(dest_tiles, dest_tiles, gates_tok, x1_p, x1_s, gate2_p, gate2_s,
      ln2_g.reshape(1, D_MODEL), ln2_b.reshape(1, D_MODEL), ys)


def _rope_tables(pos):
    half = HEAD_DIM // 2
    inv_freq = ROPE_THETA ** (-jnp.arange(half, dtype=F32) / half)
    ang = pos.astype(F32)[:, None] * inv_freq
    cos, sin = jnp.cos(ang), jnp.sin(ang)
    reps = LANES // HEAD_DIM
    return (jnp.tile(jnp.concatenate([cos, cos], axis=1), (1, reps)),
            jnp.tile(jnp.concatenate([-sin, sin], axis=1), (1, reps)))


def _work_items(group_start, group_size, n_rows):
    n_win = n_rows // ROW_WINDOW
    n_items = n_win + N_EXPERTS
    group_end = group_start + group_size
    total = group_end[-1]
    cand = jnp.concatenate([jnp.arange(n_win, dtype=jnp.int32) * ROW_WINDOW, group_end,
                            jnp.full((1,), n_rows, jnp.int32)])
    ids = jnp.arange(n_items + 1, dtype=jnp.int32)
    below = (cand[None, :] < cand[:, None]) | ((cand[None, :] == cand[:, None]) & (ids[None, :] < ids[:, None]))
    order = jnp.sum(below.astype(jnp.int32), axis=1)
    cuts = jnp.sum(jnp.where(order[None, :] == ids[:, None], cand[None, :], 0), axis=1)
    cuts = jnp.minimum(cuts, total)
    start, end = cuts[:-1], cuts[1:]
    nblk = (end - start) // ROW_BLOCK
    live = nblk > 0
    item = ids[:n_items]
    prev_live = jnp.max(jnp.where(live[None, :] & (item[None, :] <= item[:, None]), item[None, :], -1), axis=1)
    first_live = jnp.min(jnp.where(live, item, n_items - 1))
    src = jnp.where(prev_live < 0, first_live, prev_live)
    start = jnp.sum(jnp.where(src[:, None] == item[None, :], start[None, :], 0), axis=1)
    exp = jnp.minimum(jnp.sum((group_end[None, :] <= start[:, None]).astype(jnp.int32), axis=1), N_EXPERTS - 1)
    win = jnp.minimum(start // ROW_WINDOW, n_win - 1).astype(jnp.int32)
    blk0 = ((start - win * ROW_WINDOW) // ROW_BLOCK).astype(jnp.int32)
    n_f = D_FF // FF_TILE
    fidx = jnp.where(live[:, None], jnp.arange(n_f, dtype=jnp.int32)[None, :], n_f - 1).reshape(-1).astype(jnp.int32)
    return win, blk0, nblk.astype(jnp.int32), exp, fidx


def _moe(u2_p, u2_s, logits_t, x1_p, x1_s, gate2_p, gate2_s, ln2_g, ln2_b, w_up, b_up, w_down, b_down):
    n_tok = logits_t.shape[1]
    eidx, gates, rank, counts = _route(logits_t)
    count = counts[:, 0]
    group_size = (count + ROW_BLOCK - 1) // ROW_BLOCK * ROW_BLOCK
    ex = jnp.arange(N_EXPERTS, dtype=jnp.int32)
    group_end = jnp.sum(jnp.where(ex[None, :] <= ex[:, None], group_size[None, :], 0), axis=1)
    group_start = group_end - group_size
    dest = _dest(group_start, eidx, rank)
    n_tiles = n_tok // TOKEN_TILE
    dest_tiles = dest.reshape(TOP_K, n_tiles, TOKEN_TILE).transpose(1, 0, 2).reshape(n_tiles, 1, TOP_K * TOKEN_TILE)
    max_rows = n_tok * TOP_K + N_EXPERTS * (ROW_BLOCK - 1)
    n_rows = (max_rows + ROW_WINDOW - 1) // ROW_WINDOW * ROW_WINDOW
    xs = _dispatch(group_end, group_size, dest_tiles, u2_p, u2_s, n_rows)
    ys = _experts(_work_items(group_start, group_size, n_rows), xs, w_up, b_up, w_down, b_down)
    return _combine(dest_tiles, gates.T, x1_p, x1_s, gate2_p, gate2_s, ln2_g, ln2_b, ys)


def _layer(x_p, x_s, cache_k, cache_v, state_conv, c_p, c_s,
           w_ada, b_ada, w_in, b_in, conv_w, sinks, norm_attn_g, norm_conv_g, w_out, b_out,
           ln1_g, ln1_b, w_router, b_router, w_up, b_up, w_down, b_down, ln2_g, ln2_b, past_len):
    T = x_p.shape[0]
    B = x_s.shape[0]
    n_c = 1 + B
    pad_c = (-n_c) % SUBLANES
    c_all = jnp.concatenate([c_p, c_s, jnp.zeros((pad_c, D_MODEL), F32)], axis=0)
    mod = _ada(c_all, w_ada, b_ada)
    shift1, scale1, gate1, shift2, scale2, gate2 = [mod[:, j * D_MODEL:(j + 1) * D_MODEL] for j in range(6)]

    def prompt(a):
        return a[0:1]

    def sample(a):
        return a[1:n_c]

    w_in_bf = w_in.astype(BF16)
    w_out_bf = w_out.astype(BF16)
    w_router_t_bf = w_router.T.astype(BF16)

    cos_p, sin_p = _rope_tables(jnp.arange(T, dtype=jnp.int32))
    q_p, k_p, v_p, gb_p, uc_p = _inproj(x_p, prompt(shift1), prompt(scale1), cos_p, sin_p, w_in_bf, b_in,
                                        min(INPROJ_TILE_M, T))
    attn_p = _attn_prompt(q_p, k_p, v_p, sinks, norm_attn_g)
    x1_p, u2_p, lg_p = _mix(attn_p, gb_p, uc_p, jnp.zeros((CONV_WIDTH - 1, CONV_DIM), F32), conv_w, norm_conv_g,
                            w_out_bf, b_out, x_p, prompt(gate1), ln1_g, ln1_b, prompt(shift2), prompt(scale2),
                            w_router_t_bf, b_router, min(MIX_TILE_M, T), True)

    cos_s, sin_s = _rope_tables(jnp.full((1,), past_len, jnp.int32))
    q_s, k_s, v_s, gb_s, uc_s = _inproj(x_s, sample(shift1), sample(scale1), cos_s, sin_s, w_in_bf, b_in, B)
    attn_s, new_k_s, new_v_s = _attn_decode(q_s, k_s, v_s, cache_k, cache_v, sinks, norm_attn_g)
    x1_s, u2_s, lg_s = _mix(attn_s.reshape(B, ATTN_DIM), gb_s, uc_s, (state_conv[:, 1], state_conv[:, 0]), conv_w,
                            norm_conv_g, w_out_bf, b_out, x_s, sample(gate1), ln1_g, ln1_b, sample(shift2),
                            sample(scale2), w_router_t_bf, b_router, B, False)

    y_p, y_s = _moe(u2_p, u2_s, jnp.concatenate([lg_p, lg_s], axis=1), x1_p, x1_s, prompt(gate2), sample(gate2),
                    ln2_g, ln2_b, w_up, b_up, w_down, b_down)

    new_k_p = k_p[T - WINDOW:].reshape(WINDOW, N_KV_HEADS, HEAD_DIM)
    new_v_p = v_p[T - WINDOW:].reshape(WINDOW, N_KV_HEADS, HEAD_DIM)
    new_conv_p = uc_p[T - (CONV_WIDTH - 1):]
    new_conv_s = jnp.stack([state_conv[:, 1], uc_s], axis=1)
    return (y_p, y_s, new_k_p, new_v_p, new_conv_p,
            new_k_s.reshape(B, WINDOW, N_KV_HEADS, HEAD_DIM), new_v_s.reshape(B, WINDOW, N_KV_HEADS, HEAD_DIM), new_conv_s)


def kernel(x_prompt, x_sample, cache_k, cache_v, state_conv, c_prompt, c_sample, w_ada, b_ada, w_in, b_in, conv_w, sinks, norm_attn_g, norm_conv_g, w_out, b_out, ln1_g, ln1_b, w_router, b_router, w_up, b_up, w_down, b_down, ln2_g, ln2_b):
    assert x_prompt.shape[0] == 1 and x_sample.shape[1] == 1 and w_ada.shape[0] == DEPTH == 1
    B = x_sample.shape[0]
    (y_p, y_s, nk_p, nv_p, nc_p, nk_s, nv_s, nc_s) = _layer(
        x_prompt[0], x_sample[:, 0], cache_k[0], cache_v[0], state_conv[0], c_prompt, c_sample,
        w_ada[0], b_ada[0], w_in[0], b_in[0], conv_w[0], sinks[0], norm_attn_g[0], norm_conv_g[0], w_out[0], b_out[0],
        ln1_g[0], ln1_b[0], w_router[0], b_router[0], w_up[0], b_up[0], w_down[0], b_down[0], ln2_g[0], ln2_b[0],
        PAST_LEN)
    return (y_p[None], y_s.reshape(B, 1, D_MODEL), nk_p[None, None], nv_p[None, None], nc_p[None, None],
            nk_s[None], nv_s[None], nc_s[None])
```

```python
import functools

import jax
import jax.numpy as jnp
from jax import lax
from jax.experimental import pallas as pl
from jax.experimental.pallas import tpu as pltpu

F32 = jnp.float32
BF16 = jnp.bfloat16

D_MODEL = 2048
HEAD_DIM = 64
N_HEADS = 16
N_KV_HEADS = 4
GQA_GROUP = N_HEADS // N_KV_HEADS
ATTN_DIM = N_HEADS * HEAD_DIM
KV_DIM = N_KV_HEADS * HEAD_DIM
CONV_DIM = D_MODEL - ATTN_DIM
CONV_WIDTH = 3
IN_DIM = ATTN_DIM + 2 * KV_DIM + 3 * CONV_DIM
WINDOW = 128
PAST_LEN = 16384
ROPE_THETA = 10000.0
N_EXPERTS = 32
TOP_K = 4
D_FF = D_MODEL
SWIGLU_LIMIT = 7.0
SWIGLU_ALPHA = 1.702
DEPTH = 1
DEEPNORM_ALPHA = (2.0 * DEPTH) ** 0.25
LN_EPS = 1e-5
RMS_EPS = 1e-6
COL_K = ATTN_DIM
COL_V = COL_K + KV_DIM
COL_B = COL_V + KV_DIM
COL_C = COL_B + CONV_DIM
COL_X = COL_C + CONV_DIM

LANES = 128
SUBLANES = 8
VMEM_LIMIT_BYTES = 60 * 1024 * 1024

TOKEN_TILE = 128
ADA_TILE_N = 1024
INPROJ_TILE_M = 512
INPROJ_CHUNK_N = 512
MIX_TILE_M = 256
DEC_TILE_B = 16
ROW_BLOCK = 256
ROW_WINDOW = 1024
FF_TILE = 256
NEG_BIG = -1e30


def _params(n_axes, vmem=VMEM_LIMIT_BYTES):
    return pltpu.CompilerParams(dimension_semantics=("arbitrary",) * n_axes, vmem_limit_bytes=vmem)


def _standardise(x):
    mu = jnp.mean(x, axis=-1, keepdims=True)
    xc = x - mu
    var = jnp.mean(xc * xc, axis=-1, keepdims=True)
    return xc * lax.rsqrt(var + LN_EPS)


def _rms(x, g):
    return x * lax.rsqrt(jnp.mean(x * x, axis=-1, keepdims=True) + RMS_EPS) * g


def _ada_kernel(c_ref, w_ref, b_ref, o_ref):
    c = c_ref[...]
    s = (c * jax.nn.sigmoid(c)).astype(BF16)
    o_ref[...] = jnp.dot(s, w_ref[...].astype(BF16), preferred_element_type=F32) + b_ref[...]


def _ada(c_all, w_ada, b_ada):
    rows = c_all.shape[0]
    n_out = w_ada.shape[1]
    return pl.pallas_call(
        _ada_kernel,
        grid=(n_out // ADA_TILE_N,),
        in_specs=[
            pl.BlockSpec((rows, D_MODEL), lambda j: (0, 0)),
            pl.BlockSpec((D_MODEL, ADA_TILE_N), lambda j: (0, j)),
            pl.BlockSpec((1, ADA_TILE_N), lambda j: (0, j)),
        ],
        out_specs=pl.BlockSpec((rows, ADA_TILE_N), lambda j: (0, j)),
        out_shape=jax.ShapeDtypeStruct((rows, n_out), F32),
        compiler_params=_params(1),
        name="ada",
    )(c_all, w_ada, b_ada.reshape(1, n_out))


def _inproj_kernel(x_ref, shift_ref, scale_ref, cos_ref, sin_ref, w_ref, b_ref,
                   q_ref, k_ref, v_ref, gb_ref, uc_ref):
    u = (_standardise(x_ref[...]) * (1.0 + scale_ref[...]) + shift_ref[...]).astype(BF16)
    cos = cos_ref[...]
    sin = sin_ref[...]
    lane = lax.broadcasted_iota(jnp.int32, (1, LANES), 1)
    first_half = (lane % HEAD_DIM) < (HEAD_DIM // 2)

    def rope(z):
        partner = jnp.where(first_half, pltpu.roll(z, LANES - HEAD_DIM // 2, axis=1),
                            pltpu.roll(z, HEAD_DIM // 2, axis=1))
        return z * cos + partner * sin

    def proj(c0):
        w = w_ref[:, c0:c0 + INPROJ_CHUNK_N]
        return jnp.dot(u, w, preferred_element_type=F32) + b_ref[:, c0:c0 + INPROJ_CHUNK_N]

    groups = INPROJ_CHUNK_N // LANES
    for j in range(ATTN_DIM // INPROJ_CHUNK_N):
        z = proj(j * INPROJ_CHUNK_N)
        for g in range(groups):
            c0 = j * INPROJ_CHUNK_N + g * LANES
            q_ref[:, c0:c0 + LANES] = (rope(z[:, g * LANES:(g + 1) * LANES]) * (HEAD_DIM ** -0.5)).astype(BF16)
    z = proj(COL_K)
    for g in range(KV_DIM // LANES):
        k_ref[:, g * LANES:(g + 1) * LANES] = rope(z[:, g * LANES:(g + 1) * LANES])
    v_ref[...] = z[:, KV_DIM:2 * KV_DIM]
    for j in range(CONV_DIM // INPROJ_CHUNK_N):
        sl = slice(j * INPROJ_CHUNK_N, (j + 1) * INPROJ_CHUNK_N)
        gb_ref[:, sl] = proj(COL_B + j * INPROJ_CHUNK_N)
        uc_ref[:, sl] = proj(COL_C + j * INPROJ_CHUNK_N) * proj(COL_X + j * INPROJ_CHUNK_N)


def _inproj(x, shift, scale, cos, sin, w_in_bf, b_in, tm):
    T = x.shape[0]
    per_row_mod = shift.shape[0] != 1
    per_row_pos = cos.shape[0] != 1
    mod_spec = pl.BlockSpec((tm, D_MODEL), lambda i: (i, 0)) if per_row_mod else pl.BlockSpec((1, D_MODEL), lambda i: (0, 0))
    pos_spec = pl.BlockSpec((tm, LANES), lambda i: (i, 0)) if per_row_pos else pl.BlockSpec((1, LANES), lambda i: (0, 0))

    def row_spec(width):
        return pl.BlockSpec((tm, width), lambda i: (i, 0))

    return pl.pallas_call(
        _inproj_kernel,
        grid=(T // tm,),
        in_specs=[
            row_spec(D_MODEL), mod_spec, mod_spec, pos_spec, pos_spec,
            pl.BlockSpec((D_MODEL, IN_DIM), lambda i: (0, 0), pipeline_mode=pl.Buffered(1)),
            pl.BlockSpec((1, IN_DIM), lambda i: (0, 0)),
        ],
        out_specs=[row_spec(ATTN_DIM), row_spec(KV_DIM), row_spec(KV_DIM), row_spec(CONV_DIM), row_spec(CONV_DIM)],
        out_shape=[
            jax.ShapeDtypeStruct((T, ATTN_DIM), BF16),
            jax.ShapeDtypeStruct((T, KV_DIM), F32),
            jax.ShapeDtypeStruct((T, KV_DIM), F32),
            jax.ShapeDtypeStruct((T, CONV_DIM), F32),
            jax.ShapeDtypeStruct((T, CONV_DIM), F32),
        ],
        compiler_params=_params(1),
        name="inproj",
    )(x, shift, scale, cos, sin, w_in_bf, b_in.reshape(1, IN_DIM))


def _attn_kernel(q_ref, kp_ref, kc_ref, vp_ref, vc_ref, bias_ref, sink_ref, g_ref, o_ref):
    q = q_ref[...]
    k = jnp.concatenate([kp_ref[...], kc_ref[...]], axis=0).astype(BF16)
    v = jnp.concatenate([vp_ref[...], vc_ref[...]], axis=0).astype(BF16)
    bias = bias_ref[...]
    outs = []
    for g in range(N_KV_HEADS):
        heads = range(g * GQA_GROUP, (g + 1) * GQA_GROUP)
        qg = jnp.concatenate([q[:, h * HEAD_DIM:(h + 1) * HEAD_DIM] for h in heads], axis=0)
        kg = k[:, g * HEAD_DIM:(g + 1) * HEAD_DIM]
        vg = v[:, g * HEAD_DIM:(g + 1) * HEAD_DIM]
        sink = sink_ref[g]
        s = lax.dot_general(kg, qg, (((1,), (1,)), ((), ())), preferred_element_type=F32) + bias
        m = jnp.maximum(jnp.max(s, axis=0, keepdims=True), sink)
        e = jnp.exp(s - m)
        den = jnp.sum(e, axis=0, keepdims=True) + jnp.exp(sink - m)
        o = lax.dot_general(vg, e.astype(BF16), (((0,), (0,)), ((), ())), preferred_element_type=F32)
        o = o / den
        outs.extend(o[:, j * WINDOW:(j + 1) * WINDOW] for j in range(GQA_GROUP))
    o = jnp.concatenate(outs, axis=0).T
    o_ref[...] = _rms(o, g_ref[...]).astype(BF16)


def _attn_mask_bias():
    qi = jnp.arange(GQA_GROUP * WINDOW, dtype=jnp.int32)[None, :] % WINDOW
    ks = jnp.arange(2 * WINDOW, dtype=jnp.int32)[:, None]
    later = (ks >= qi) & (ks <= qi + WINDOW)
    first = later & (ks >= WINDOW)
    return jnp.where(jnp.stack([first, later]), 0.0, NEG_BIG).astype(F32)


def _attn_prompt(q, k, v, sinks, norm_g):
    T = q.shape[0]

    def cur(width):
        return pl.BlockSpec((WINDOW, width), lambda n: (n, 0))

    def prev(width):
        return pl.BlockSpec((WINDOW, width), lambda n: (jnp.maximum(n - 1, 0), 0))

    return pl.pallas_call(
        _attn_kernel,
        grid=(T // WINDOW,),
        in_specs=[
            cur(ATTN_DIM), prev(KV_DIM), cur(KV_DIM), prev(KV_DIM), cur(KV_DIM),
            pl.BlockSpec((None, 2 * WINDOW, GQA_GROUP * WINDOW), lambda n: (jnp.minimum(n, 1), 0, 0)),
            pl.BlockSpec((N_KV_HEADS, 1, GQA_GROUP * WINDOW), lambda n: (0, 0, 0)),
            pl.BlockSpec((1, ATTN_DIM), lambda n: (0, 0)),
        ],
        out_specs=cur(ATTN_DIM),
        out_shape=jax.ShapeDtypeStruct((T, ATTN_DIM), BF16),
        compiler_params=_params(1),
        name="attn",
    )(q, k, k, v, v, _attn_mask_bias(),
      jnp.repeat(sinks, WINDOW).reshape(N_KV_HEADS, 1, GQA_GROUP * WINDOW),
      norm_g.reshape(1, ATTN_DIM))


def _attn_dec_kernel(q_ref, kn_ref, vn_ref, ck_ref, cv_ref, sink_ref, g_ref, o_ref, nk_ref, nv_ref):
    tb = q_ref.shape[0]
    q = q_ref[...].astype(F32)
    lane_group = lax.broadcasted_iota(jnp.int32, (1, N_HEADS, KV_DIM), 2) // HEAD_DIM
    head_group = lax.broadcasted_iota(jnp.int32, (1, N_HEADS, KV_DIM), 1) // GQA_GROUP
    own = lane_group == head_group
    qe = jnp.where(own, jnp.concatenate([q] * N_KV_HEADS, axis=2), 0.0)
    ck = ck_ref[...]
    cv = cv_ref[...]
    kn = kn_ref[...]
    vn = vn_ref[...]
    s = jnp.einsum("bhc,bwc->bhw", qe.astype(BF16), ck.astype(BF16), preferred_element_type=F32)
    s_new = jnp.sum(qe.astype(BF16).astype(F32) * kn.astype(BF16).astype(F32), axis=2, keepdims=True)
    sink = sink_ref[...]
    m = jnp.maximum(jnp.maximum(jnp.max(s, axis=2, keepdims=True), s_new), sink)
    e = jnp.exp(s - m)
    e_new = jnp.exp(s_new - m)
    den = jnp.sum(e, axis=2, keepdims=True) + e_new + jnp.exp(sink - m)
    p = (e / den).astype(BF16)
    p_new = (e_new / den).astype(BF16).astype(F32)
    o = jnp.einsum("bhw,bwc->bhc", p, cv.astype(BF16), preferred_element_type=F32)
    o = o + p_new * vn.astype(BF16).astype(F32)
    o = jnp.where(own, o, 0.0)
    oh = o[:, :, 0:HEAD_DIM]
    for g in range(1, N_KV_HEADS):
        oh = oh + o[:, :, g * HEAD_DIM:(g + 1) * HEAD_DIM]
    ms = jnp.sum(jnp.sum(oh * oh, axis=2, keepdims=True), axis=1, keepdims=True) / ATTN_DIM
    o_ref[...] = (oh * lax.rsqrt(ms + RMS_EPS) * g_ref[...]).astype(BF16)
    row = lax.broadcasted_iota(jnp.int32, (1, WINDOW, 1), 1)
    nk_ref[...] = jnp.where(row == WINDOW - 1, kn, pltpu.roll(ck, WINDOW - 1, axis=1))
    nv_ref[...] = jnp.where(row == WINDOW - 1, vn, pltpu.roll(cv, WINDOW - 1, axis=1))


def _attn_decode(q, k_new, v_new, cache_k, cache_v, sinks, norm_g):
    B = q.shape[0]
    tb = DEC_TILE_B

    def b3(d1, d2):
        return pl.BlockSpec((tb, d1, d2), lambda i: (i, 0, 0))

    return pl.pallas_call(
        _attn_dec_kernel,
        grid=(B // tb,),
        in_specs=[
            b3(N_HEADS, HEAD_DIM), b3(1, KV_DIM), b3(1, KV_DIM), b3(WINDOW, KV_DIM), b3(WINDOW, KV_DIM),
            pl.BlockSpec((1, N_HEADS, 1), lambda i: (0, 0, 0)),
            pl.BlockSpec((1, N_HEADS, HEAD_DIM), lambda i: (0, 0, 0)),
        ],
        out_specs=[b3(N_HEADS, HEAD_DIM), b3(WINDOW, KV_DIM), b3(WINDOW, KV_DIM)],
        out_shape=[
            jax.ShapeDtypeStruct((B, N_HEADS, HEAD_DIM), BF16),
            jax.ShapeDtypeStruct((B, WINDOW, KV_DIM), F32),
            jax.ShapeDtypeStruct((B, WINDOW, KV_DIM), F32),
        ],
        compiler_params=_params(1),
        name="attn_dec",
    )(q.reshape(B, N_HEADS, HEAD_DIM), k_new.reshape(B, 1, KV_DIM), v_new.reshape(B, 1, KV_DIM),
      cache_k.reshape(B, WINDOW, KV_DIM), cache_v.reshape(B, WINDOW, KV_DIM),
      sinks.reshape(1, N_HEADS, 1), norm_g.reshape(1, N_HEADS, HEAD_DIM))


def _mix_tail(attn, gb, uc, um1, um2, cw_ref, gconv_ref, wout_ref, bout_ref, x_ref, gate1_ref, ln1g_ref, ln1b_ref,
              shift2_ref, scale2_ref, wr_ref, br_ref, x1_ref, u2_ref, lg_ref):
    cw = cw_ref[...]
    conv = cw[0:1, :] * um2 + cw[1:2, :] * um1 + cw[2:3, :] * uc
    conv_n = _rms(gb * conv, gconv_ref[...]).astype(BF16)
    mixed = (jnp.dot(attn, wout_ref[0:ATTN_DIM, :], preferred_element_type=F32)
             + jnp.dot(conv_n, wout_ref[ATTN_DIM:D_MODEL, :], preferred_element_type=F32) + bout_ref[...])
    x1 = _standardise(DEEPNORM_ALPHA * x_ref[...] + gate1_ref[...] * mixed) * ln1g_ref[...] + ln1b_ref[...]
    x1_ref[...] = x1
    u2 = _standardise(x1) * (1.0 + scale2_ref[...]) + shift2_ref[...]
    u2_ref[...] = u2
    lg_ref[...] = lax.dot_general(wr_ref[...], u2.astype(BF16), (((1,), (1,)), ((), ())),
                                  preferred_element_type=F32) + br_ref[...]


def _mix_seq_kernel(attn_ref, gb_ref, uc_ref, halo_ref, hist_ref, *rest):
    i = pl.program_id(0)
    uc = uc_ref[...]
    tm = uc.shape[0]
    above = jnp.where(i == 0, hist_ref[...], halo_ref[...])
    row = lax.broadcasted_iota(jnp.int32, (tm, 1), 0)
    um1 = jnp.where(row == 0, above[7:8, :], pltpu.roll(uc, 1, axis=0))
    um2 = jnp.where(row == 0, above[6:7, :], jnp.where(row == 1, above[7:8, :], pltpu.roll(uc, 2, axis=0)))
    _mix_tail(attn_ref[...], gb_ref[...], uc, um1, um2, *rest)


def _mix_tok_kernel(attn_ref, gb_ref, uc_ref, um1_ref, um2_ref, *rest):
    _mix_tail(attn_ref[...], gb_ref[...], uc_ref[...], um1_ref[...], um2_ref[...], *rest)


def _mix(attn_n, gb, uc, conv_prev, conv_w, norm_conv_g, w_out_bf, b_out, x, gate1, ln1_g, ln1_b, shift2, scale2,
         w_router_t_bf, b_router, tm, sequential):
    T = x.shape[0]
    per_row_mod = gate1.shape[0] != 1

    def row_spec(width):
        return pl.BlockSpec((tm, width), lambda i: (i, 0))

    def const_spec(rows, width):
        return pl.BlockSpec((rows, width), lambda i: (0, 0))

    mod_spec = row_spec(D_MODEL) if per_row_mod else const_spec(1, D_MODEL)
    if sequential:
        hist8 = jnp.concatenate([jnp.zeros((SUBLANES - 2, CONV_DIM), F32), conv_prev], axis=0)
        halo_blocks = tm // SUBLANES
        conv_specs = [pl.BlockSpec((SUBLANES, CONV_DIM), lambda i: (jnp.maximum(i * halo_blocks - 1, 0), 0)),
                      const_spec(SUBLANES, CONV_DIM)]
        conv_args = (uc, hist8)
        body = _mix_seq_kernel
    else:
        conv_specs = [row_spec(CONV_DIM), row_spec(CONV_DIM)]
        conv_args = conv_prev
        body = _mix_tok_kernel
    return pl.pallas_call(
        body,
        grid=(T // tm,),
        in_specs=[row_spec(ATTN_DIM), row_spec(CONV_DIM), row_spec(CONV_DIM), *conv_specs,
                  const_spec(CONV_WIDTH, CONV_DIM), const_spec(1, CONV_DIM),
                  const_spec(D_MODEL, D_MODEL), const_spec(1, D_MODEL),
                  row_spec(D_MODEL), mod_spec, const_spec(1, D_MODEL), const_spec(1, D_MODEL),
                  mod_spec, mod_spec,
                  const_spec(N_EXPERTS, D_MODEL), const_spec(N_EXPERTS, 1)],
        out_specs=[row_spec(D_MODEL), row_spec(D_MODEL), pl.BlockSpec((N_EXPERTS, tm), lambda i: (0, i))],
        out_shape=[jax.ShapeDtypeStruct((T, D_MODEL), F32), jax.ShapeDtypeStruct((T, D_MODEL), F32),
                   jax.ShapeDtypeStruct((N_EXPERTS, T), F32)],
        compiler_params=_params(1),
        name="mix_seq" if sequential else "mix_tok",
    )(attn_n, gb, uc, *conv_args, conv_w, norm_conv_g.reshape(1, CONV_DIM), w_out_bf, b_out.reshape(1, D_MODEL),
      x, gate1, ln1_g.reshape(1, D_MODEL), ln1_b.reshape(1, D_MODEL), shift2, scale2,
      w_router_t_bf, b_router.reshape(N_EXPERTS, 1))


def _route_kernel(lg_ref, eidx_ref, gate_ref, rank_ref, cnt_ref):
    n_tok = lg_ref.shape[1]
    r = lax.broadcasted_iota(jnp.int32, (TOKEN_TILE, TOKEN_TILE), 0)
    c = lax.broadcasted_iota(jnp.int32, (TOKEN_TILE, TOKEN_TILE), 1)
    earlier = (r < c).astype(BF16)
    expert = lax.broadcasted_iota(jnp.int32, (N_EXPERTS, TOKEN_TILE), 0).astype(F32)

    def body(ci, count):
        off = pl.multiple_of(ci * TOKEN_TILE, TOKEN_TILE)
        l = lg_ref[:, pl.ds(off, TOKEN_TILE)]
        vals, idxs, sels = [], [], []
        for _ in range(TOP_K):
            m = jnp.max(l, axis=0, keepdims=True)
            idx = jnp.min(jnp.where(l == m, expert, float(N_EXPERTS)), axis=0, keepdims=True)
            sel = expert == idx
            vals.append(m)
            idxs.append(idx)
            sels.append(sel)
            l = jnp.where(sel, -jnp.inf, l)
        chosen = jnp.where(sels[0] | sels[1] | sels[2] | sels[3], 1.0, 0.0)
        before = jnp.dot(chosen.astype(BF16), earlier, preferred_element_type=F32) + count
        ranks = [jnp.sum(jnp.where(s, before, 0.0), axis=0, keepdims=True) for s in sels]
        es = [jnp.exp(v - vals[0]) for v in vals]
        den = es[0] + es[1] + es[2] + es[3]
        eidx_ref[:, pl.ds(off, TOKEN_TILE)] = jnp.concatenate(idxs, axis=0).astype(jnp.int32)
        rank_ref[:, pl.ds(off, TOKEN_TILE)] = jnp.concatenate(ranks, axis=0).astype(jnp.int32)
        gate_ref[:, pl.ds(off, TOKEN_TILE)] = jnp.concatenate([e / den for e in es], axis=0)
        return count + jnp.sum(chosen, axis=1, keepdims=True)

    count = lax.fori_loop(0, n_tok // TOKEN_TILE, body, jnp.zeros((N_EXPERTS, 1), F32))
    cnt_ref[...] = jnp.broadcast_to(count, (N_EXPERTS, LANES)).astype(jnp.int32)


def _route(logits_t):
    n_tok = logits_t.shape[1]
    return pl.pallas_call(
        _route_kernel,
        out_shape=[jax.ShapeDtypeStruct((TOP_K, n_tok), jnp.int32), jax.ShapeDtypeStruct((TOP_K, n_tok), F32),
                   jax.ShapeDtypeStruct((TOP_K, n_tok), jnp.int32), jax.ShapeDtypeStruct((N_EXPERTS, LANES), jnp.int32)],
        compiler_params=pltpu.CompilerParams(vmem_limit_bytes=VMEM_LIMIT_BYTES),
        name="route",
    )(logits_t)


def _dest_kernel(start_ref, eidx_ref, rank_ref, dest_ref):
    e = eidx_ref[...]
    base = jnp.zeros(e.shape, jnp.int32)
    for x in range(N_EXPERTS):
        base = jnp.where(e == x, start_ref[x], base)
    dest_ref[...] = base + rank_ref[...]


def _dest(group_start, eidx, rank):
    return pl.pallas_call(
        _dest_kernel,
        in_specs=[pl.BlockSpec(memory_space=pltpu.SMEM), pl.BlockSpec(memory_space=pltpu.VMEM),
                  pl.BlockSpec(memory_space=pltpu.VMEM)],
        out_specs=pl.BlockSpec(memory_space=pltpu.VMEM),
        out_shape=jax.ShapeDtypeStruct(eidx.shape, jnp.int32),
        name="dest",
    )(group_start, eidx, rank)


def _row_wait(src_ref, dst_ref, sem, n_tiles):
    for _ in range(n_tiles):
        pltpu.make_async_copy(src_ref.at[pl.ds(0, TOKEN_TILE), :], dst_ref.at[pl.ds(0, TOKEN_TILE), :], sem).wait()


def _dispatch_kernel(n_prompt_tiles, gend_ref, gsize_ref, dest_ref, up_ref, us_ref, xs_ref, zero_buf, zsem, sem):
    i = pl.program_id(0)

    @pl.when(i == 0)
    def _():
        zero_buf[...] = jnp.zeros_like(zero_buf)
        for e in range(N_EXPERTS):
            @pl.when(gsize_ref[e] > 0)
            def _():
                r0 = pl.multiple_of(gend_ref[e] - ROW_BLOCK, ROW_BLOCK)
                pltpu.make_async_copy(zero_buf, xs_ref.at[pl.ds(r0, ROW_BLOCK), :], zsem).start()
        for e in range(N_EXPERTS):
            @pl.when(gsize_ref[e] > 0)
            def _():
                pltpu.make_async_copy(zero_buf, xs_ref.at[pl.ds(0, ROW_BLOCK), :], zsem).wait()

    def scatter(src_ref):
        def body(j, carry):
            t = jnp.bitwise_and(j, TOKEN_TILE - 1)
            d = dest_ref[0, 0, j]
            pltpu.make_async_copy(src_ref.at[pl.ds(t, 1), :], xs_ref.at[pl.ds(d, 1), :], sem).start()
            return carry
        lax.fori_loop(0, TOP_K * TOKEN_TILE, body, 0, unroll=8)
        _row_wait(src_ref, xs_ref, sem, TOP_K)

    @pl.when(i < n_prompt_tiles)
    def _():
        scatter(up_ref)

    @pl.when(i >= n_prompt_tiles)
    def _():
        scatter(us_ref)


def _dispatch(group_end, group_size, dest_tiles, u2_p, u2_s, n_rows):
    n_p = u2_p.shape[0] // TOKEN_TILE
    n_s = u2_s.shape[0] // TOKEN_TILE
    grid_spec = pltpu.PrefetchScalarGridSpec(
        num_scalar_prefetch=2,
        grid=(n_p + n_s,),
        in_specs=[
            pl.BlockSpec((1, 1, TOP_K * TOKEN_TILE), lambda i, *_: (i, 0, 0), memory_space=pltpu.SMEM),
            pl.BlockSpec((TOKEN_TILE, D_MODEL), lambda i, *_: (jnp.minimum(i, n_p - 1), 0)),
            pl.BlockSpec((TOKEN_TILE, D_MODEL), lambda i, *_: (jnp.maximum(i - n_p, 0), 0)),
        ],
        out_specs=pl.BlockSpec(memory_space=pl.ANY),
        scratch_shapes=[pltpu.VMEM((ROW_BLOCK, D_MODEL), F32), pltpu.SemaphoreType.DMA(()), pltpu.SemaphoreType.DMA(())],
    )
    return pl.pallas_call(
        functools.partial(_dispatch_kernel, n_p),
        grid_spec=grid_spec,
        out_shape=jax.ShapeDtypeStruct((n_rows, D_MODEL), F32),
        compiler_params=_params(1),
        name="dispatch",
    )(group_end, group_size, dest_tiles, u2_p, u2_s)


def _experts_kernel(win_ref, blk0_ref, nblk_ref, exp_ref, fidx_ref, x_ref, wg_ref, wl_ref, wd_ref, bg_ref, bl_ref, bd_ref,
                    o_ref):
    w = pl.program_id(0)
    f = pl.program_id(1)
    nblk = nblk_ref[w]
    blk0 = blk0_ref[w]

    def block_rows(j):
        return pl.ds(pl.multiple_of((blk0 + j) * ROW_BLOCK, ROW_BLOCK), ROW_BLOCK)

    @pl.when(f == 0)
    def _():
        def init(j, carry):
            o_ref[block_rows(j), :] = jnp.broadcast_to(bd_ref[...], (ROW_BLOCK, D_MODEL))
            return carry
        lax.fori_loop(0, nblk, init, 0)

    def blocks(js):
        wg = wg_ref[...].astype(BF16)
        wl = wl_ref[...].astype(BF16)
        wd = wd_ref[...].astype(BF16)
        for j in js:
            rows = block_rows(j)
            x = x_ref[rows, :].astype(BF16)
            glu = jnp.dot(x, wg, preferred_element_type=F32) + bg_ref[...]
            lin = jnp.dot(x, wl, preferred_element_type=F32) + bl_ref[...]
            glu = jnp.minimum(glu, SWIGLU_LIMIT)
            lin = jnp.clip(lin, -SWIGLU_LIMIT, SWIGLU_LIMIT)
            act = glu * jax.nn.sigmoid(SWIGLU_ALPHA * glu) * (lin + 1.0)
            o_ref[rows, :] += jnp.dot(act.astype(BF16), wd, preferred_element_type=F32)

    def pair(p, carry):
        blocks((2 * p, 2 * p + 1))
        return carry

    lax.fori_loop(0, lax.shift_right_logical(nblk, 1), pair, 0)

    @pl.when(jnp.bitwise_and(nblk, 1) == 1)
    def _():
        blocks((nblk - 1,))


def _experts(items, xs, w_up, b_up, w_down, b_down):
    win, blk0, nblk, exp, fidx = items
    n_items = win.shape[0]
    n_rows = xs.shape[0]
    n_f = D_FF // FF_TILE

    def fi(w, f, fidx_ref):
        return fidx_ref[w * n_f + f]

    grid_spec = pltpu.PrefetchScalarGridSpec(
        num_scalar_prefetch=5,
        grid=(n_items, n_f),
        in_specs=[
            pl.BlockSpec((ROW_WINDOW, D_MODEL), lambda w, f, win, b0, nb, ex, fx: (win[w], 0)),
            pl.BlockSpec((None, D_MODEL, FF_TILE), lambda w, f, win, b0, nb, ex, fx: (ex[w], 0, fi(w, f, fx))),
            pl.BlockSpec((None, D_MODEL, FF_TILE), lambda w, f, win, b0, nb, ex, fx: (ex[w], 0, n_f + fi(w, f, fx))),
            pl.BlockSpec((None, FF_TILE, D_MODEL), lambda w, f, win, b0, nb, ex, fx: (ex[w], fi(w, f, fx), 0)),
            pl.BlockSpec((None, 1, FF_TILE), lambda w, f, win, b0, nb, ex, fx: (ex[w], 0, fi(w, f, fx))),
            pl.BlockSpec((None, 1, FF_TILE), lambda w, f, win, b0, nb, ex, fx: (ex[w], 0, n_f + fi(w, f, fx))),
            pl.BlockSpec((None, 1, D_MODEL), lambda w, f, win, b0, nb, ex, fx: (ex[w], 0, 0)),
        ],
        out_specs=pl.BlockSpec((ROW_WINDOW, D_MODEL), lambda w, f, win, b0, nb, ex, fx: (win[w], 0)),
    )
    return pl.pallas_call(
        _experts_kernel,
        grid_spec=grid_spec,
        out_shape=jax.ShapeDtypeStruct((n_rows, D_MODEL), F32),
        compiler_params=_params(2),
        name="experts",
    )(win, blk0, nblk, exp, fidx, xs, w_up, w_up, w_down,
      b_up.reshape(N_EXPERTS, 1, 2 * D_FF), b_up.reshape(N_EXPERTS, 1, 2 * D_FF), b_down.reshape(N_EXPERTS, 1, D_MODEL))


def _combine_kernel(n_prompt_tiles, dcur_ref, dnxt_ref, gates_ref, x1p_ref, x1s_ref, g2p_ref, g2s_ref, lng_ref, lnb_ref,
                    ys_ref, yp_ref, ysm_ref, buf, sem):
    i = pl.program_id(0)
    n = pl.num_programs(0)
    slot = i % 2

    def gather(idx_ref, s):
        def body(j, carry):
            k = lax.shift_right_logical(j, TOKEN_TILE.bit_length() - 1)
            t = jnp.bitwise_and(j, TOKEN_TILE - 1)
            d = idx_ref[0, 0, j]
            pltpu.make_async_copy(ys_ref.at[pl.ds(d, 1), :], buf.at[s, k, pl.ds(t, 1), :], sem.at[s]).start()
            return carry
        lax.fori_loop(0, TOP_K * TOKEN_TILE, body, 0, unroll=8)

    @pl.when(i == 0)
    def _():
        gather(dcur_ref, 0)

    @pl.when(i + 1 < n)
    def _():
        gather(dnxt_ref, 1 - slot)

    for k in range(TOP_K):
        pltpu.make_async_copy(ys_ref.at[pl.ds(0, TOKEN_TILE), :], buf.at[slot, k], sem.at[slot]).wait()

    gates = gates_ref[...]
    ffn = buf[slot, 0] * gates[:, 0:1]
    for k in range(1, TOP_K):
        ffn = ffn + buf[slot, k] * gates[:, k:k + 1]

    def finish(x1, gate2, out_ref):
        out_ref[...] = _standardise(DEEPNORM_ALPHA * x1 + gate2 * ffn) * lng_ref[...] + lnb_ref[...]

    @pl.when(i < n_prompt_tiles)
    def _():
        finish(x1p_ref[...], g2p_ref[...], yp_ref)

    @pl.when(i >= n_prompt_tiles)
    def _():
        finish(x1s_ref[...], g2s_ref[...], ysm_ref)


def _combine(dest_tiles, gates_tok, x1_p, x1_s, gate2_p, gate2_s, ln2_g, ln2_b, ys):
    n_p = x1_p.shape[0] // TOKEN_TILE
    n_s = x1_s.shape[0] // TOKEN_TILE
    n = n_p + n_s

    def p_idx(i):
        return jnp.minimum(i, n_p - 1)

    def s_idx(i):
        return jnp.maximum(i - n_p, 0)

    smem_tile = (1, 1, TOP_K * TOKEN_TILE)
    return pl.pallas_call(
        functools.partial(_combine_kernel, n_p),
        grid=(n,),
        in_specs=[
            pl.BlockSpec(smem_tile, lambda i: (i, 0, 0), memory_space=pltpu.SMEM),
            pl.BlockSpec(smem_tile, lambda i: (jnp.minimum(i + 1, n - 1), 0, 0), memory_space=pltpu.SMEM),
            pl.BlockSpec((TOKEN_TILE, TOP_K), lambda i: (i, 0)),
            pl.BlockSpec((TOKEN_TILE, D_MODEL), lambda i: (p_idx(i), 0)),
            pl.BlockSpec((TOKEN_TILE, D_MODEL), lambda i: (s_idx(i), 0)),
            pl.BlockSpec((1, D_MODEL), lambda i: (0, 0)),
            pl.BlockSpec((TOKEN_TILE, D_MODEL), lambda i: (s_idx(i), 0)),
            pl.BlockSpec((1, D_MODEL), lambda i: (0, 0)),
            pl.BlockSpec((1, D_MODEL), lambda i: (0, 0)),
            pl.BlockSpec(memory_space=pl.ANY),
        ],
        out_specs=[pl.BlockSpec((TOKEN_TILE, D_MODEL), lambda i: (p_idx(i), 0)),
                   pl.BlockSpec((TOKEN_TILE, D_MODEL), lambda i: (s_idx(i), 0))],
        out_shape=[jax.ShapeDtypeStruct(x1_p.shape, F32), jax.ShapeDtypeStruct(x1_s.shape, F32)],
        scratch_shapes=[pltpu.VMEM((2, TOP_K, TOKEN_TILE, D_MODEL), F32), pltpu.SemaphoreType.DMA((2,))],
        compiler_params=_params(1),
        name="combine",
    )(dest_tiles, dest_tiles, gates_tok, x1_p, x1_s, gate2_p, gate2_s,
      ln2_g.reshape(1, D_MODEL), ln2_b.reshape(1, D_MODEL), ys)


def _rope_tables(pos):
    half = HEAD_DIM // 2
    inv_freq = ROPE_THETA ** (-jnp.arange(half, dtype=F32) / half)
    ang = pos.astype(F32)[:, None] * inv_freq
    cos, sin = jnp.cos(ang), jnp.sin(ang)
    reps = LANES // HEAD_DIM
    return (jnp.tile(jnp.concatenate([cos, cos], axis=1), (1, reps)),
            jnp.tile(jnp.concatenate([-sin, sin], axis=1), (1, reps)))


def _work_items(group_start, group_size, n_rows):
    n_win = n_rows // ROW_WINDOW
    n_items = n_win + N_EXPERTS
    group_end = group_start + group_size
    total = group_end[-1]
    cand = jnp.concatenate([jnp.arange(n_win, dtype=jnp.int32) * ROW_WINDOW, group_end,
                            jnp.full((1,), n_rows, jnp.int32)])
    ids = jnp.arange(n_items + 1, dtype=jnp.int32)
    below = (cand[None, :] < cand[:, None]) | ((cand[None, :] == cand[:, None]) & (ids[None, :] < ids[:, None]))
    order = jnp.sum(below.astype(jnp.int32), axis=1)
    cuts = jnp.sum(jnp.where(order[None, :] == ids[:, None], cand[None, :], 0), axis=1)
    cuts = jnp.minimum(cuts, total)
    start, end = cuts[:-1], cuts[1:]
    nblk = (end - start) // ROW_BLOCK
    live = nblk > 0
    item = ids[:n_items]
    prev_live = jnp.max(jnp.where(live[None, :] & (item[None, :] <= item[:, None]), item[None, :], -1), axis=1)
    first_live = jnp.min(jnp.where(live, item, n_items - 1))
    src = jnp.where(prev_live < 0, first_live, prev_live)
    start = jnp.sum(jnp.where(src[:, None] == item[None, :], start[None, :], 0), axis=1)
    exp = jnp.minimum(jnp.sum((group_end[None, :] <= start[:, None]).astype(jnp.int32), axis=1), N_EXPERTS - 1)
    win = jnp.minimum(start // ROW_WINDOW, n_win - 1).astype(jnp.int32)
    blk0 = ((start - win * ROW_WINDOW) // ROW_BLOCK).astype(jnp.int32)
    n_f = D_FF // FF_TILE
    fidx = jnp.where(live[:, None], jnp.arange(n_f, dtype=jnp.int32)[None, :], n_f - 1).reshape(-1).astype(jnp.int32)
    return win, blk0, nblk.astype(jnp.int32), exp, fidx


def _moe(u2_p, u2_s, logits_t, x1_p, x1_s, gate2_p, gate2_s, ln2_g, ln2_b, w_up, b_up, w_down, b_down):
    n_tok = logits_t.shape[1]
    eidx, gates, rank, counts = _route(logits_t)
    count = counts[:, 0]
    group_size = (count + ROW_BLOCK - 1) // ROW_BLOCK * ROW_BLOCK
    ex = jnp.arange(N_EXPERTS, dtype=jnp.int32)
    group_end = jnp.sum(jnp.where(ex[None, :] <= ex[:, None], group_size[None, :], 0), axis=1)
    group_start = group_end - group_size
    dest = _dest(group_start, eidx, rank)
    n_tiles = n_tok // TOKEN_TILE
    dest_tiles = dest.reshape(TOP_K, n_tiles, TOKEN_TILE).transpose(1, 0, 2).reshape(n_tiles, 1, TOP_K * TOKEN_TILE)
    max_rows = n_tok * TOP_K + N_EXPERTS * (ROW_BLOCK - 1)
    n_rows = (max_rows + ROW_WINDOW - 1) // ROW_WINDOW * ROW_WINDOW
    xs = _dispatch(group_end, group_size, dest_tiles, u2_p, u2_s, n_rows)
    ys = _experts(_work_items(group_start, group_size, n_rows), xs, w_up, b_up, w_down, b_down)
    return _combine(dest_tiles, gates.T, x1_p, x1_s, gate2_p, gate2_s, ln2_g, ln2_b, ys)


def _layer(x_p, x_s, cache_k, cache_v, state_conv, c_p, c_s,
           w_ada, b_ada, w_in, b_in, conv_w, sinks, norm_attn_g, norm_conv_g, w_out, b_out,
           ln1_g, ln1_b, w_router, b_router, w_up, b_up, w_down, b_down, ln2_g, ln2_b, past_len):
    T = x_p.shape[0]
    B = x_s.shape[0]
    n_c = 1 + B
    pad_c = (-n_c) % SUBLANES
    c_all = jnp.concatenate([c_p, c_s, jnp.zeros((pad_c, D_MODEL), F32)], axis=0)
    mod = _ada(c_all, w_ada, b_ada)
    shift1, scale1, gate1, shift2, scale2, gate2 = [mod[:, j * D_MODEL:(j + 1) * D_MODEL] for j in range(6)]

    def prompt(a):
        return a[0:1]

    def sample(a):
        return a[1:n_c]

    w_in_bf = w_in.astype(BF16)
    w_out_bf = w_out.astype(BF16)
    w_router_t_bf = w_router.T.astype(BF16)

    cos_p, sin_p = _rope_tables(jnp.arange(T, dtype=jnp.int32))
    q_p, k_p, v_p, gb_p, uc_p = _inproj(x_p, prompt(shift1), prompt(scale1), cos_p, sin_p, w_in_bf, b_in,
                                        min(INPROJ_TILE_M, T))
    attn_p = _attn_prompt(q_p, k_p, v_p, sinks, norm_attn_g)
    x1_p, u2_p, lg_p = _mix(attn_p, gb_p, uc_p, jnp.zeros((CONV_WIDTH - 1, CONV_DIM), F32), conv_w, norm_conv_g,
                            w_out_bf, b_out, x_p, prompt(gate1), ln1_g, ln1_b, prompt(shift2), prompt(scale2),
                            w_router_t_bf, b_router, min(MIX_TILE_M, T), True)

    cos_s, sin_s = _rope_tables(jnp.full((1,), past_len, jnp.int32))
    q_s, k_s, v_s, gb_s, uc_s = _inproj(x_s, sample(shift1), sample(scale1), cos_s, sin_s, w_in_bf, b_in, B)
    attn_s, new_k_s, new_v_s = _attn_decode(q_s, k_s, v_s, cache_k, cache_v, sinks, norm_attn_g)
    x1_s, u2_s, lg_s = _mix(attn_s.reshape(B, ATTN_DIM), gb_s, uc_s, (state_conv[:, 1], state_conv[:, 0]), conv_w,
                            norm_conv_g, w_out_bf, b_out, x_s, sample(gate1), ln1_g, ln1_b, sample(shift2),
                            sample(scale2), w_router_t_bf, b_router, B, False)

    y_p, y_s = _moe(u2_p, u2_s, jnp.concatenate([lg_p, lg_s], axis=1), x1_p, x1_s, prompt(gate2), sample(gate2),
                    ln2_g, ln2_b, w_up, b_up, w_down, b_down)

    new_k_p = k_p[T - WINDOW:].reshape(WINDOW, N_KV_HEADS, HEAD_DIM)
    new_v_p = v_p[T - WINDOW:].reshape(WINDOW, N_KV_HEADS, HEAD_DIM)
    new_conv_p = uc_p[T - (CONV_WIDTH - 1):]
    new_conv_s = jnp.stack([state_conv[:, 1], uc_s], axis=1)
    return (y_p, y_s, new_k_p, new_v_p, new_conv_p,
            new_k_s.reshape(B, WINDOW, N_KV_HEADS, HEAD_DIM), new_v_s.reshape(B, WINDOW, N_KV_HEADS, HEAD_DIM), new_conv_s)


def kernel(x_prompt, x_sample, cache_k, cache_v, state_conv, c_prompt, c_sample, w_ada, b_ada, w_in, b_in, conv_w, sinks, norm_attn_g, norm_conv_g, w_out, b_out, ln1_g, ln1_b, w_router, b_router, w_up, b_up, w_down, b_down, ln2_g, ln2_b):
    assert x_prompt.shape[0] == 1 and x_sample.shape[1] == 1 and w_ada.shape[0] == DEPTH == 1
    B = x_sample.shape[0]
    (y_p, y_s, nk_p, nv_p, nc_p, nk_s, nv_s, nc_s) = _layer(
        x_prompt[0], x_sample[:, 0], cache_k[0], cache_v[0], state_conv[0], c_prompt, c_sample,
        w_ada[0], b_ada[0], w_in[0], b_in[0], conv_w[0], sinks[0], norm_attn_g[0], norm_conv_g[0], w_out[0], b_out[0],
        ln1_g[0], ln1_b[0], w_router[0], b_router[0], w_up[0], b_up[0], w_down[0], b_down[0], ln2_g[0], ln2_b[0],
        PAST_LEN)
    return (y_p[None], y_s.reshape(B, 1, D_MODEL), nk_p[None, None], nv_p[None, None], nc_p[None, None],
            nk_s[None], nv_s[None], nc_s[None])
```

```python
import functools

import jax
import jax.numpy as jnp
from jax import lax
from jax.experimental import pallas as pl
from jax.experimental.pallas import tpu as pltpu

F32 = jnp.float32
BF16 = jnp.bfloat16

D_MODEL = 2048
HEAD_DIM = 64
N_HEADS = 16
N_KV_HEADS = 4
GQA_GROUP = N_HEADS // N_KV_HEADS
ATTN_DIM = N_HEADS * HEAD_DIM
KV_DIM = N_KV_HEADS * HEAD_DIM
CONV_DIM = D_MODEL - ATTN_DIM
CONV_WIDTH = 3
IN_DIM = ATTN_DIM + 2 * KV_DIM + 3 * CONV_DIM
WINDOW = 128
PAST_LEN = 16384
ROPE_THETA = 10000.0
N_EXPERTS = 32
TOP_K = 4
D_FF = D_MODEL
SWIGLU_LIMIT = 7.0
SWIGLU_ALPHA = 1.702
DEPTH = 1
DEEPNORM_ALPHA = (2.0 * DEPTH) ** 0.25
LN_EPS = 1e-5
RMS_EPS = 1e-6
COL_K = ATTN_DIM
COL_V = COL_K + KV_DIM
COL_B = COL_V + KV_DIM
COL_C = COL_B + CONV_DIM
COL_X = COL_C + CONV_DIM

LANES = 128
SUBLANES = 8
VMEM_LIMIT_BYTES = 60 * 1024 * 1024

TOKEN_TILE = 128
ADA_TILE_N = 1024
INPROJ_TILE_M = 512
INPROJ_CHUNK_N = 512
MIX_TILE_M = 256
DEC_TILE_B = 16
ROW_BLOCK = 256
EXPERT_MAX_BLOCKS = 9
FF_TILE = 256
NEG_BIG = -1e30


def _params(n_axes, vmem=VMEM_LIMIT_BYTES):
    return pltpu.CompilerParams(dimension_semantics=("arbitrary",) * n_axes, vmem_limit_bytes=vmem)


def _standardise(x):
    mu = jnp.mean(x, axis=-1, keepdims=True)
    xc = x - mu
    var = jnp.mean(xc * xc, axis=-1, keepdims=True)
    return xc * lax.rsqrt(var + LN_EPS)


def _rms(x, g):
    return x * lax.rsqrt(jnp.mean(x * x, axis=-1, keepdims=True) + RMS_EPS) * g


def _ada_kernel(c_ref, w_ref, b_ref, o_ref):
    c = c_ref[...]
    s = (c * jax.nn.sigmoid(c)).astype(BF16)
    o_ref[...] = jnp.dot(s, w_ref[...].astype(BF16), preferred_element_type=F32) + b_ref[...]


def _ada(c_all, w_ada, b_ada):
    rows = c_all.shape[0]
    n_out = w_ada.shape[1]
    return pl.pallas_call(
        _ada_kernel,
        grid=(n_out // ADA_TILE_N,),
        in_specs=[
            pl.BlockSpec((rows, D_MODEL), lambda j: (0, 0)),
            pl.BlockSpec((D_MODEL, ADA_TILE_N), lambda j: (0, j)),
            pl.BlockSpec((1, ADA_TILE_N), lambda j: (0, j)),
        ],
        out_specs=pl.BlockSpec((rows, ADA_TILE_N), lambda j: (0, j)),
        out_shape=jax.ShapeDtypeStruct((rows, n_out), F32),
        compiler_params=_params(1),
        name="ada",
    )(c_all, w_ada, b_ada.reshape(1, n_out))


def _inproj_kernel(x_ref, shift_ref, scale_ref, cos_ref, sin_ref, w_ref, b_ref,
                   q_ref, k_ref, v_ref, gb_ref, uc_ref):
    u = (_standardise(x_ref[...]) * (1.0 + scale_ref[...]) + shift_ref[...]).astype(BF16)
    cos = cos_ref[...]
    sin = sin_ref[...]
    lane = lax.broadcasted_iota(jnp.int32, (1, LANES), 1)
    first_half = (lane % HEAD_DIM) < (HEAD_DIM // 2)

    def rope(z):
        partner = jnp.where(first_half, pltpu.roll(z, LANES - HEAD_DIM // 2, axis=1),
                            pltpu.roll(z, HEAD_DIM // 2, axis=1))
        return z * cos + partner * sin

    def proj(c0):
        w = w_ref[:, c0:c0 + INPROJ_CHUNK_N]
        return jnp.dot(u, w, preferred_element_type=F32) + b_ref[:, c0:c0 + INPROJ_CHUNK_N]

    groups = INPROJ_CHUNK_N // LANES
    for j in range(ATTN_DIM // INPROJ_CHUNK_N):
        z = proj(j * INPROJ_CHUNK_N)
        for g in range(groups):
            c0 = j * INPROJ_CHUNK_N + g * LANES
            q_ref[:, c0:c0 + LANES] = (rope(z[:, g * LANES:(g + 1) * LANES]) * (HEAD_DIM ** -0.5)).astype(BF16)
    z = proj(COL_K)
    for g in range(KV_DIM // LANES):
        k_ref[:, g * LANES:(g + 1) * LANES] = rope(z[:, g * LANES:(g + 1) * LANES])
    v_ref[...] = z[:, KV_DIM:2 * KV_DIM]
    for j in range(CONV_DIM // INPROJ_CHUNK_N):
        sl = slice(j * INPROJ_CHUNK_N, (j + 1) * INPROJ_CHUNK_N)
        gb_ref[:, sl] = proj(COL_B + j * INPROJ_CHUNK_N)
        uc_ref[:, sl] = proj(COL_C + j * INPROJ_CHUNK_N) * proj(COL_X + j * INPROJ_CHUNK_N)


def _inproj(x, shift, scale, cos, sin, w_in_bf, b_in, tm):
    T = x.shape[0]
    per_row_mod = shift.shape[0] != 1
    per_row_pos = cos.shape[0] != 1
    mod_spec = pl.BlockSpec((tm, D_MODEL), lambda i: (i, 0)) if per_row_mod else pl.BlockSpec((1, D_MODEL), lambda i: (0, 0))
    pos_spec = pl.BlockSpec((tm, LANES), lambda i: (i, 0)) if per_row_pos else pl.BlockSpec((1, LANES), lambda i: (0, 0))

    def row_spec(width):
        return pl.BlockSpec((tm, width), lambda i: (i, 0))

    return pl.pallas_call(
        _inproj_kernel,
        grid=(T // tm,),
        in_specs=[
            row_spec(D_MODEL), mod_spec, mod_spec, pos_spec, pos_spec,
            pl.BlockSpec((D_MODEL, IN_DIM), lambda i: (0, 0), pipeline_mode=pl.Buffered(1)),
            pl.BlockSpec((1, IN_DIM), lambda i: (0, 0)),
        ],
        out_specs=[row_spec(ATTN_DIM), row_spec(KV_DIM), row_spec(KV_DIM), row_spec(CONV_DIM), row_spec(CONV_DIM)],
        out_shape=[
            jax.ShapeDtypeStruct((T, ATTN_DIM), BF16),
            jax.ShapeDtypeStruct((T, KV_DIM), F32),
            jax.ShapeDtypeStruct((T, KV_DIM), F32),
            jax.ShapeDtypeStruct((T, CONV_DIM), F32),
            jax.ShapeDtypeStruct((T, CONV_DIM), F32),
        ],
        compiler_params=_params(1),
        name="inproj",
    )(x, shift, scale, cos, sin, w_in_bf, b_in.reshape(1, IN_DIM))


def _attn_kernel(q_ref, kp_ref, kc_ref, vp_ref, vc_ref, bias_ref, sink_ref, g_ref, o_ref):
    q = q_ref[...]
    k = jnp.concatenate([kp_ref[...], kc_ref[...]], axis=0).astype(BF16)
    v = jnp.concatenate([vp_ref[...], vc_ref[...]], axis=0).astype(BF16)
    bias = bias_ref[...]
    outs = []
    for g in range(N_KV_HEADS):
        heads = range(g * GQA_GROUP, (g + 1) * GQA_GROUP)
        qg = jnp.concatenate([q[:, h * HEAD_DIM:(h + 1) * HEAD_DIM] for h in heads], axis=0)
        kg = k[:, g * HEAD_DIM:(g + 1) * HEAD_DIM]
        vg = v[:, g * HEAD_DIM:(g + 1) * HEAD_DIM]
        sink = sink_ref[g]
        s = lax.dot_general(kg, qg, (((1,), (1,)), ((), ())), preferred_element_type=F32) + bias
        m = jnp.maximum(jnp.max(s, axis=0, keepdims=True), sink)
        e = jnp.exp(s - m)
        den = jnp.sum(e, axis=0, keepdims=True) + jnp.exp(sink - m)
        o = lax.dot_general(vg, e.astype(BF16), (((0,), (0,)), ((), ())), preferred_element_type=F32)
        o = o / den
        outs.extend(o[:, j * WINDOW:(j + 1) * WINDOW] for j in range(GQA_GROUP))
    o = jnp.concatenate(outs, axis=0).T
    o_ref[...] = _rms(o, g_ref[...]).astype(BF16)


def _attn_mask_bias():
    qi = jnp.arange(GQA_GROUP * WINDOW, dtype=jnp.int32)[None, :] % WINDOW
    ks = jnp.arange(2 * WINDOW, dtype=jnp.int32)[:, None]
    later = (ks >= qi) & (ks <= qi + WINDOW)
    first = later & (ks >= WINDOW)
    return jnp.where(jnp.stack([first, later]), 0.0, NEG_BIG).astype(F32)


def _attn_prompt(q, k, v, sinks, norm_g):
    T = q.shape[0]

    def cur(width):
        return pl.BlockSpec((WINDOW, width), lambda n: (n, 0))

    def prev(width):
        return pl.BlockSpec((WINDOW, width), lambda n: (jnp.maximum(n - 1, 0), 0))

    return pl.pallas_call(
        _attn_kernel,
        grid=(T // WINDOW,),
        in_specs=[
            cur(ATTN_DIM), prev(KV_DIM), cur(KV_DIM), prev(KV_DIM), cur(KV_DIM),
            pl.BlockSpec((None, 2 * WINDOW, GQA_GROUP * WINDOW), lambda n: (jnp.minimum(n, 1), 0, 0)),
            pl.BlockSpec((N_KV_HEADS, 1, GQA_GROUP * WINDOW), lambda n: (0, 0, 0)),
            pl.BlockSpec((1, ATTN_DIM), lambda n: (0, 0)),
        ],
        out_specs=cur(ATTN_DIM),
        out_shape=jax.ShapeDtypeStruct((T, ATTN_DIM), BF16),
        compiler_params=_params(1),
        name="attn",
    )(q, k, k, v, v, _attn_mask_bias(),
      jnp.repeat(sinks, WINDOW).reshape(N_KV_HEADS, 1, GQA_GROUP * WINDOW),
      norm_g.reshape(1, ATTN_DIM))


def _attn_dec_kernel(q_ref, kn_ref, vn_ref, ck_ref, cv_ref, sink_ref, g_ref, o_ref, nk_ref, nv_ref):
    tb = q_ref.shape[0]
    q = q_ref[...].astype(F32)
    lane_group = lax.broadcasted_iota(jnp.int32, (1, N_HEADS, KV_DIM), 2) // HEAD_DIM
    head_group = lax.broadcasted_iota(jnp.int32, (1, N_HEADS, KV_DIM), 1) // GQA_GROUP
    own = lane_group == head_group
    qe = jnp.where(own, jnp.concatenate([q] * N_KV_HEADS, axis=2), 0.0)
    ck = ck_ref[...]
    cv = cv_ref[...]
    kn = kn_ref[...]
    vn = vn_ref[...]
    s = jnp.einsum("bhc,bwc->bhw", qe.astype(BF16), ck.astype(BF16), preferred_element_type=F32)
    s_new = jnp.sum(qe.astype(BF16).astype(F32) * kn.astype(BF16).astype(F32), axis=2, keepdims=True)
    sink = sink_ref[...]
    m = jnp.maximum(jnp.maximum(jnp.max(s, axis=2, keepdims=True), s_new), sink)
    e = jnp.exp(s - m)
    e_new = jnp.exp(s_new - m)
    den = jnp.sum(e, axis=2, keepdims=True) + e_new + jnp.exp(sink - m)
    p = (e / den).astype(BF16)
    p_new = (e_new / den).astype(BF16).astype(F32)
    o = jnp.einsum("bhw,bwc->bhc", p, cv.astype(BF16), preferred_element_type=F32)
    o = o + p_new * vn.astype(BF16).astype(F32)
    o = jnp.where(own, o, 0.0)
    oh = o[:, :, 0:HEAD_DIM]
    for g in range(1, N_KV_HEADS):
        oh = oh + o[:, :, g * HEAD_DIM:(g + 1) * HEAD_DIM]
    ms = jnp.sum(jnp.sum(oh * oh, axis=2, keepdims=True), axis=1, keepdims=True) / ATTN_DIM
    o_ref[...] = (oh * lax.rsqrt(ms + RMS_EPS) * g_ref[...]).astype(BF16)
    row = lax.broadcasted_iota(jnp.int32, (1, WINDOW, 1), 1)
    nk_ref[...] = jnp.where(row == WINDOW - 1, kn, pltpu.roll(ck, WINDOW - 1, axis=1))
    nv_ref[...] = jnp.where(row == WINDOW - 1, vn, pltpu.roll(cv, WINDOW - 1, axis=1))


def _attn_decode(q, k_new, v_new, cache_k, cache_v, sinks, norm_g):
    B = q.shape[0]
    tb = DEC_TILE_B

    def b3(d1, d2):
        return pl.BlockSpec((tb, d1, d2), lambda i: (i, 0, 0))

    return pl.pallas_call(
        _attn_dec_kernel,
        grid=(B // tb,),
        in_specs=[
            b3(N_HEADS, HEAD_DIM), b3(1, KV_DIM), b3(1, KV_DIM), b3(WINDOW, KV_DIM), b3(WINDOW, KV_DIM),
            pl.BlockSpec((1, N_HEADS, 1), lambda i: (0, 0, 0)),
            pl.BlockSpec((1, N_HEADS, HEAD_DIM), lambda i: (0, 0, 0)),
        ],
        out_specs=[b3(N_HEADS, HEAD_DIM), b3(WINDOW, KV_DIM), b3(WINDOW, KV_DIM)],
        out_shape=[
            jax.ShapeDtypeStruct((B, N_HEADS, HEAD_DIM), BF16),
            jax.ShapeDtypeStruct((B, WINDOW, KV_DIM), F32),
            jax.ShapeDtypeStruct((B, WINDOW, KV_DIM), F32),
        ],
        compiler_params=_params(1),
        name="attn_dec",
    )(q.reshape(B, N_HEADS, HEAD_DIM), k_new.reshape(B, 1, KV_DIM), v_new.reshape(B, 1, KV_DIM),
      cache_k.reshape(B, WINDOW, KV_DIM), cache_v.reshape(B, WINDOW, KV_DIM),
      sinks.reshape(1, N_HEADS, 1), norm_g.reshape(1, N_HEADS, HEAD_DIM))


def _mix_tail(attn, gb, uc, um1, um2, cw_ref, gconv_ref, wout_ref, bout_ref, x_ref, gate1_ref, ln1g_ref, ln1b_ref,
              shift2_ref, scale2_ref, wr_ref, br_ref, x1_ref, u2_ref, lg_ref):
    cw = cw_ref[...]
    conv = cw[0:1, :] * um2 + cw[1:2, :] * um1 + cw[2:3, :] * uc
    conv_n = _rms(gb * conv, gconv_ref[...]).astype(BF16)
    mixed = (jnp.dot(attn, wout_ref[0:ATTN_DIM, :], preferred_element_type=F32)
             + jnp.dot(conv_n, wout_ref[ATTN_DIM:D_MODEL, :], preferred_element_type=F32) + bout_ref[...])
    x1 = _standardise(DEEPNORM_ALPHA * x_ref[...] + gate1_ref[...] * mixed) * ln1g_ref[...] + ln1b_ref[...]
    x1_ref[...] = x1
    u2 = _standardise(x1) * (1.0 + scale2_ref[...]) + shift2_ref[...]
    u2_ref[...] = u2
    lg_ref[...] = lax.dot_general(wr_ref[...], u2.astype(BF16), (((1,), (1,)), ((), ())),
                                  preferred_element_type=F32) + br_ref[...]


def _mix_seq_kernel(attn_ref, gb_ref, uc_ref, halo_ref, hist_ref, *rest):
    i = pl.program_id(0)
    uc = uc_ref[...]
    tm = uc.shape[0]
    above = jnp.where(i == 0, hist_ref[...], halo_ref[...])
    row = lax.broadcasted_iota(jnp.int32, (tm, 1), 0)
    um1 = jnp.where(row == 0, above[7:8, :], pltpu.roll(uc, 1, axis=0))
    um2 = jnp.where(row == 0, above[6:7, :], jnp.where(row == 1, above[7:8, :], pltpu.roll(uc, 2, axis=0)))
    _mix_tail(attn_ref[...], gb_ref[...], uc, um1, um2, *rest)


def _mix_tok_kernel(attn_ref, gb_ref, uc_ref, um1_ref, um2_ref, *rest):
    _mix_tail(attn_ref[...], gb_ref[...], uc_ref[...], um1_ref[...], um2_ref[...], *rest)


def _mix(attn_n, gb, uc, conv_prev, conv_w, norm_conv_g, w_out_bf, b_out, x, gate1, ln1_g, ln1_b, shift2, scale2,
         w_router_t_bf, b_router, tm, sequential):
    T = x.shape[0]
    per_row_mod = gate1.shape[0] != 1

    def row_spec(width):
        return pl.BlockSpec((tm, width), lambda i: (i, 0))

    def const_spec(rows, width):
        return pl.BlockSpec((rows, width), lambda i: (0, 0))

    mod_spec = row_spec(D_MODEL) if per_row_mod else const_spec(1, D_MODEL)
    if sequential:
        hist8 = jnp.concatenate([jnp.zeros((SUBLANES - 2, CONV_DIM), F32), conv_prev], axis=0)
        halo_blocks = tm // SUBLANES
        conv_specs = [pl.BlockSpec((SUBLANES, CONV_DIM), lambda i: (jnp.maximum(i * halo_blocks - 1, 0), 0)),
                      const_spec(SUBLANES, CONV_DIM)]
        conv_args = (uc, hist8)
        body = _mix_seq_kernel
    else:
        conv_specs = [row_spec(CONV_DIM), row_spec(CONV_DIM)]
        conv_args = conv_prev
        body = _mix_tok_kernel
    return pl.pallas_call(
        body,
        grid=(T // tm,),
        in_specs=[row_spec(ATTN_DIM), row_spec(CONV_DIM), row_spec(CONV_DIM), *conv_specs,
                  const_spec(CONV_WIDTH, CONV_DIM), const_spec(1, CONV_DIM),
                  const_spec(D_MODEL, D_MODEL), const_spec(1, D_MODEL),
                  row_spec(D_MODEL), mod_spec, const_spec(1, D_MODEL), const_spec(1, D_MODEL),
                  mod_spec, mod_spec,
                  const_spec(N_EXPERTS, D_MODEL), const_spec(N_EXPERTS, 1)],
        out_specs=[row_spec(D_MODEL), row_spec(D_MODEL), pl.BlockSpec((N_EXPERTS, tm), lambda i: (0, i))],
        out_shape=[jax.ShapeDtypeStruct((T, D_MODEL), F32), jax.ShapeDtypeStruct((T, D_MODEL), F32),
                   jax.ShapeDtypeStruct((N_EXPERTS, T), F32)],
        compiler_params=_params(1),
        name="mix_seq" if sequential else "mix_tok",
    )(attn_n, gb, uc, *conv_args, conv_w, norm_conv_g.reshape(1, CONV_DIM), w_out_bf, b_out.reshape(1, D_MODEL),
      x, gate1, ln1_g.reshape(1, D_MODEL), ln1_b.reshape(1, D_MODEL), shift2, scale2,
      w_router_t_bf, b_router.reshape(N_EXPERTS, 1))


def _route_kernel(lg_ref, eidx_ref, gate_ref, rank_ref, cnt_ref):
    n_tok = lg_ref.shape[1]
    r = lax.broadcasted_iota(jnp.int32, (TOKEN_TILE, TOKEN_TILE), 0)
    c = lax.broadcasted_iota(jnp.int32, (TOKEN_TILE, TOKEN_TILE), 1)
    earlier = (r < c).astype(BF16)
    expert = lax.broadcasted_iota(jnp.int32, (N_EXPERTS, TOKEN_TILE), 0).astype(F32)

    def body(ci, count):
        off = pl.multiple_of(ci * TOKEN_TILE, TOKEN_TILE)
        l = lg_ref[:, pl.ds(off, TOKEN_TILE)]
        vals, idxs, sels = [], [], []
        for _ in range(TOP_K):
            m = jnp.max(l, axis=0, keepdims=True)
            idx = jnp.min(jnp.where(l == m, expert, float(N_EXPERTS)), axis=0, keepdims=True)
            sel = expert == idx
            vals.append(m)
            idxs.append(idx)
            sels.append(sel)
            l = jnp.where(sel, -jnp.inf, l)
        chosen = jnp.where(sels[0] | sels[1] | sels[2] | sels[3], 1.0, 0.0)
        before = jnp.dot(chosen.astype(BF16), earlier, preferred_element_type=F32) + count
        ranks = [jnp.sum(jnp.where(s, before, 0.0), axis=0, keepdims=True) for s in sels]
        es = [jnp.exp(v - vals[0]) for v in vals]
        den = es[0] + es[1] + es[2] + es[3]
        eidx_ref[:, pl.ds(off, TOKEN_TILE)] = jnp.concatenate(idxs, axis=0).astype(jnp.int32)
        rank_ref[:, pl.ds(off, TOKEN_TILE)] = jnp.concatenate(ranks, axis=0).astype(jnp.int32)
        gate_ref[:, pl.ds(off, TOKEN_TILE)] = jnp.concatenate([e / den for e in es], axis=0)
        return count + jnp.sum(chosen, axis=1, keepdims=True)

    count = lax.fori_loop(0, n_tok // TOKEN_TILE, body, jnp.zeros((N_EXPERTS, 1), F32))
    cnt_ref[...] = jnp.broadcast_to(count, (N_EXPERTS, LANES)).astype(jnp.int32)


def _route(logits_t):
    n_tok = logits_t.shape[1]
    return pl.pallas_call(
        _route_kernel,
        out_shape=[jax.ShapeDtypeStruct((TOP_K, n_tok), jnp.int32), jax.ShapeDtypeStruct((TOP_K, n_tok), F32),
                   jax.ShapeDtypeStruct((TOP_K, n_tok), jnp.int32), jax.ShapeDtypeStruct((N_EXPERTS, LANES), jnp.int32)],
        compiler_params=pltpu.CompilerParams(vmem_limit_bytes=VMEM_LIMIT_BYTES),
        name="route",
    )(logits_t)


def _dest_kernel(start_ref, eidx_ref, rank_ref, dest_ref):
    e = eidx_ref[...]
    base = jnp.zeros(e.shape, jnp.int32)
    for x in range(N_EXPERTS):
        base = jnp.where(e == x, start_ref[x], base)
    dest_ref[...] = base + rank_ref[...]


def _dest(group_start, eidx, rank):
    return pl.pallas_call(
        _dest_kernel,
        in_specs=[pl.BlockSpec(memory_space=pltpu.SMEM), pl.BlockSpec(memory_space=pltpu.VMEM),
                  pl.BlockSpec(memory_space=pltpu.VMEM)],
        out_specs=pl.BlockSpec(memory_space=pltpu.VMEM),
        out_shape=jax.ShapeDtypeStruct(eidx.shape, jnp.int32),
        name="dest",
    )(group_start, eidx, rank)


def _row_wait(src_ref, dst_ref, sem, n_tiles):
    for _ in range(n_tiles):
        pltpu.make_async_copy(src_ref.at[pl.ds(0, TOKEN_TILE), :], dst_ref.at[pl.ds(0, TOKEN_TILE), :], sem).wait()


def _dispatch_kernel(n_prompt_tiles, gend_ref, gsize_ref, dest_ref, up_ref, us_ref, xs_ref, zero_buf, zsem, sem):
    i = pl.program_id(0)

    @pl.when(i == 0)
    def _():
        zero_buf[...] = jnp.zeros_like(zero_buf)
        for e in range(N_EXPERTS):
            @pl.when(gsize_ref[e] > 0)
            def _():
                r0 = pl.multiple_of(gend_ref[e] - ROW_BLOCK, ROW_BLOCK)
                pltpu.make_async_copy(zero_buf, xs_ref.at[pl.ds(r0, ROW_BLOCK), :], zsem).start()
        for e in range(N_EXPERTS):
            @pl.when(gsize_ref[e] > 0)
            def _():
                pltpu.make_async_copy(zero_buf, xs_ref.at[pl.ds(0, ROW_BLOCK), :], zsem).wait()

    def scatter(src_ref):
        def body(j, carry):
            t = jnp.bitwise_and(j, TOKEN_TILE - 1)
            d = dest_ref[0, 0, j]
            pltpu.make_async_copy(src_ref.at[pl.ds(t, 1), :], xs_ref.at[pl.ds(d, 1), :], sem).start()
            return carry
        lax.fori_loop(0, TOP_K * TOKEN_TILE, body, 0, unroll=8)
        _row_wait(src_ref, xs_ref, sem, TOP_K)

    @pl.when(i < n_prompt_tiles)
    def _():
        scatter(up_ref)

    @pl.when(i >= n_prompt_tiles)
    def _():
        scatter(us_ref)


def _dispatch(group_end, group_size, dest_tiles, u2_p, u2_s, n_rows):
    n_p = u2_p.shape[0] // TOKEN_TILE
    n_s = u2_s.shape[0] // TOKEN_TILE
    grid_spec = pltpu.PrefetchScalarGridSpec(
        num_scalar_prefetch=2,
        grid=(n_p + n_s,),
        in_specs=[
            pl.BlockSpec((1, 1, TOP_K * TOKEN_TILE), lambda i, *_: (i, 0, 0), memory_space=pltpu.SMEM),
            pl.BlockSpec((TOKEN_TILE, D_MODEL), lambda i, *_: (jnp.minimum(i, n_p - 1), 0)),
            pl.BlockSpec((TOKEN_TILE, D_MODEL), lambda i, *_: (jnp.maximum(i - n_p, 0), 0)),
        ],
        out_specs=pl.BlockSpec(memory_space=pl.ANY),
        scratch_shapes=[pltpu.VMEM((ROW_BLOCK, D_MODEL), F32), pltpu.SemaphoreType.DMA(()), pltpu.SemaphoreType.DMA(())],
    )
    return pl.pallas_call(
        functools.partial(_dispatch_kernel, n_p),
        grid_spec=grid_spec,
        out_shape=jax.ShapeDtypeStruct((n_rows, D_MODEL), F32),
        compiler_params=_params(1),
        name="dispatch",
    )(group_end, group_size, dest_tiles, u2_p, u2_s)


_FIRST, _MIDDLE, _LAST = 0, 1, 2


def _experts_kernel(row0_ref, nblk_ref, exp_ref, fidx_ref, xs_ref, wg_ref, wl_ref, wd_ref, bg_ref, bl_ref, bd_ref,
                    ys_ref, xbuf, acc, in_sem, out_sem):
    i = pl.program_id(0)
    f = pl.program_id(1)
    last_f = pl.num_programs(1) - 1
    nblk = nblk_ref[i]
    row0 = row0_ref[i]

    def local(j):
        return pl.ds(pl.multiple_of(j * ROW_BLOCK, ROW_BLOCK), ROW_BLOCK)

    def in_hbm(j):
        return pl.ds(pl.multiple_of(row0 + j * ROW_BLOCK, ROW_BLOCK), ROW_BLOCK)

    def x_copy(j):
        return pltpu.make_async_copy(xs_ref.at[in_hbm(j), :], xbuf.at[local(j), :], in_sem.at[j])

    def y_copy(j):
        return pltpu.make_async_copy(acc.at[local(j), :], ys_ref.at[in_hbm(j), :], out_sem.at[j])

    def blocks(js, phase):
        wg = wg_ref[...].astype(BF16)
        wl = wl_ref[...].astype(BF16)
        wd = wd_ref[...].astype(BF16)
        for j in js:
            if phase == _FIRST:
                x_copy(j).wait()
            x = xbuf[local(j), :].astype(BF16)
            glu = jnp.dot(x, wg, preferred_element_type=F32) + bg_ref[...]
            lin = jnp.dot(x, wl, preferred_element_type=F32) + bl_ref[...]
            glu = jnp.minimum(glu, SWIGLU_LIMIT)
            lin = jnp.clip(lin, -SWIGLU_LIMIT, SWIGLU_LIMIT)
            act = glu * jax.nn.sigmoid(SWIGLU_ALPHA * glu) * (lin + 1.0)
            y = jnp.dot(act.astype(BF16), wd, preferred_element_type=F32)
            if phase == _FIRST:
                acc[local(j), :] = y + bd_ref[...]
            else:
                acc[local(j), :] += y
            if phase == _LAST:
                y_copy(j).start()

    def run(phase):
        def pair(p, carry):
            blocks((2 * p, 2 * p + 1), phase)
            return carry

        lax.fori_loop(0, lax.shift_right_logical(nblk, 1), pair, 0)

        @pl.when(jnp.bitwise_and(nblk, 1) == 1)
        def _():
            blocks((nblk - 1,), phase)

    @pl.when(f == 0)
    def _():
        def start(j, carry):
            x_copy(j).start()
            return carry
        lax.fori_loop(0, nblk, start, 0)
        run(_FIRST)

    @pl.when((f > 0) & (f < last_f))
    def _():
        run(_MIDDLE)

    @pl.when(f == last_f)
    def _():
        run(_LAST)

        def drain(j, carry):
            y_copy(j).wait()
            return carry
        lax.fori_loop(0, nblk, drain, 0)


def _experts(items, xs, w_up, b_up, w_down, b_down):
    row0, nblk, exp, fidx = items
    n_items = row0.shape[0]
    n_rows = xs.shape[0]
    n_f = D_FF // FF_TILE
    assert n_f >= 2

    def fi(i, f, fidx_ref):
        return fidx_ref[i * n_f + f]

    max_rows = EXPERT_MAX_BLOCKS * ROW_BLOCK
    grid_spec = pltpu.PrefetchScalarGridSpec(
        num_scalar_prefetch=4,
        grid=(n_items, n_f),
        in_specs=[
            pl.BlockSpec(memory_space=pl.ANY),
            pl.BlockSpec((None, D_MODEL, FF_TILE), lambda i, f, r0, nb, ex, fx: (ex[i], 0, fi(i, f, fx))),
            pl.BlockSpec((None, D_MODEL, FF_TILE), lambda i, f, r0, nb, ex, fx: (ex[i], 0, n_f + fi(i, f, fx))),
            pl.BlockSpec((None, FF_TILE, D_MODEL), lambda i, f, r0, nb, ex, fx: (ex[i], fi(i, f, fx), 0)),
            pl.BlockSpec((None, 1, FF_TILE), lambda i, f, r0, nb, ex, fx: (ex[i], 0, fi(i, f, fx))),
            pl.BlockSpec((None, 1, FF_TILE), lambda i, f, r0, nb, ex, fx: (ex[i], 0, n_f + fi(i, f, fx))),
            pl.BlockSpec((None, 1, D_MODEL), lambda i, f, r0, nb, ex, fx: (ex[i], 0, 0)),
        ],
        out_specs=pl.BlockSpec(memory_space=pl.ANY),
        scratch_shapes=[pltpu.VMEM((max_rows, D_MODEL), F32), pltpu.VMEM((max_rows, D_MODEL), F32),
                        pltpu.SemaphoreType.DMA((EXPERT_MAX_BLOCKS,)), pltpu.SemaphoreType.DMA((EXPERT_MAX_BLOCKS,))],
    )
    return pl.pallas_call(
        _experts_kernel,
        grid_spec=grid_spec,
        out_shape=jax.ShapeDtypeStruct((n_rows, D_MODEL), F32),
        compiler_params=_params(2),
        name="experts",
    )(row0, nblk, exp, fidx, xs, w_up, w_up, w_down,
      b_up.reshape(N_EXPERTS, 1, 2 * D_FF), b_up.reshape(N_EXPERTS, 1, 2 * D_FF), b_down.reshape(N_EXPERTS, 1, D_MODEL))


def _combine_kernel(n_prompt_tiles, dcur_ref, dnxt_ref, gates_ref, x1p_ref, x1s_ref, g2p_ref, g2s_ref, lng_ref, lnb_ref,
                    ys_ref, yp_ref, ysm_ref, buf, sem):
    i = pl.program_id(0)
    n = pl.num_programs(0)
    slot = i % 2

    def gather(idx_ref, s):
        def body(j, carry):
            k = lax.shift_right_logical(j, TOKEN_TILE.bit_length() - 1)
            t = jnp.bitwise_and(j, TOKEN_TILE - 1)
            d = idx_ref[0, 0, j]
            pltpu.make_async_copy(ys_ref.at[pl.ds(d, 1), :], buf.at[s, k, pl.ds(t, 1), :], sem.at[s]).start()
            return carry
        lax.fori_loop(0, TOP_K * TOKEN_TILE, body, 0, unroll=8)

    @pl.when(i == 0)
    def _():
        gather(dcur_ref, 0)

    @pl.when(i + 1 < n)
    def _():
        gather(dnxt_ref, 1 - slot)

    for k in range(TOP_K):
        pltpu.make_async_copy(ys_ref.at[pl.ds(0, TOKEN_TILE), :], buf.at[slot, k], sem.at[slot]).wait()

    gates = gates_ref[...]
    ffn = buf[slot, 0] * gates[:, 0:1]
    for k in range(1, TOP_K):
        ffn = ffn + buf[slot, k] * gates[:, k:k + 1]

    def finish(x1, gate2, out_ref):
        out_ref[...] = _standardise(DEEPNORM_ALPHA * x1 + gate2 * ffn) * lng_ref[...] + lnb_ref[...]

    @pl.when(i < n_prompt_tiles)
    def _():
        finish(x1p_ref[...], g2p_ref[...], yp_ref)

    @pl.when(i >= n_prompt_tiles)
    def _():
        finish(x1s_ref[...], g2s_ref[...], ysm_ref)


def _combine(dest_tiles, gates_tok, x1_p, x1_s, gate2_p, gate2_s, ln2_g, ln2_b, ys):
    n_p = x1_p.shape[0] // TOKEN_TILE
    n_s = x1_s.shape[0] // TOKEN_TILE
    n = n_p + n_s

    def p_idx(i):
        return jnp.minimum(i, n_p - 1)

    def s_idx(i):
        return jnp.maximum(i - n_p, 0)

    smem_tile = (1, 1, TOP_K * TOKEN_TILE)
    return pl.pallas_call(
        functools.partial(_combine_kernel, n_p),
        grid=(n,),
        in_specs=[
            pl.BlockSpec(smem_tile, lambda i: (i, 0, 0), memory_space=pltpu.SMEM),
            pl.BlockSpec(smem_tile, lambda i: (jnp.minimum(i + 1, n - 1), 0, 0), memory_space=pltpu.SMEM),
            pl.BlockSpec((TOKEN_TILE, TOP_K), lambda i: (i, 0)),
            pl.BlockSpec((TOKEN_TILE, D_MODEL), lambda i: (p_idx(i), 0)),
            pl.BlockSpec((TOKEN_TILE, D_MODEL), lambda i: (s_idx(i), 0)),
            pl.BlockSpec((1, D_MODEL), lambda i: (0, 0)),
            pl.BlockSpec((TOKEN_TILE, D_MODEL), lambda i: (s_idx(i), 0)),
            pl.BlockSpec((1, D_MODEL), lambda i: (0, 0)),
            pl.BlockSpec((1, D_MODEL), lambda i: (0, 0)),
            pl.BlockSpec(memory_space=pl.ANY),
        ],
        out_specs=[pl.BlockSpec((TOKEN_TILE, D_MODEL), lambda i: (p_idx(i), 0)),
                   pl.BlockSpec((TOKEN_TILE, D_MODEL), lambda i: (s_idx(i), 0))],
        out_shape=[jax.ShapeDtypeStruct(x1_p.shape, F32), jax.ShapeDtypeStruct(x1_s.shape, F32)],
        scratch_shapes=[pltpu.VMEM((2, TOP_K, TOKEN_TILE, D_MODEL), F32), pltpu.SemaphoreType.DMA((2,))],
        compiler_params=_params(1),
        name="combine",
    )(dest_tiles, dest_tiles, gates_tok, x1_p, x1_s, gate2_p, gate2_s,
      ln2_g.reshape(1, D_MODEL), ln2_b.reshape(1, D_MODEL), ys)


def _rope_tables(pos):
    half = HEAD_DIM // 2
    inv_freq = ROPE_THETA ** (-jnp.arange(half, dtype=F32) / half)
    ang = pos.astype(F32)[:, None] * inv_freq
    cos, sin = jnp.cos(ang), jnp.sin(ang)
    reps = LANES // HEAD_DIM
    return (jnp.tile(jnp.concatenate([cos, cos], axis=1), (1, reps)),
            jnp.tile(jnp.concatenate([-sin, sin], axis=1), (1, reps)))


def _work_items(group_start, group_size, n_items):
    chunk_rows = EXPERT_MAX_BLOCKS * ROW_BLOCK
    ex = jnp.arange(N_EXPERTS, dtype=jnp.int32)
    n_chunks = (group_size + chunk_rows - 1) // chunk_rows
    chunk_end = jnp.sum(jnp.where(ex[None, :] <= ex[:, None], n_chunks[None, :], 0), axis=1)
    chunk_start = chunk_end - n_chunks
    item = jnp.arange(n_items, dtype=jnp.int32)
    live = item < chunk_end[-1]
    it = jnp.minimum(item, chunk_end[-1] - 1)
    exp = jnp.minimum(jnp.sum((chunk_end[None, :] <= it[:, None]).astype(jnp.int32), axis=1), N_EXPERTS - 1)
    own = exp[:, None] == ex[None, :]

    def pick(per_expert):
        return jnp.sum(jnp.where(own, per_expert[None, :], 0), axis=1)

    chunk = it - pick(chunk_start)
    row0 = pick(group_start) + chunk * chunk_rows
    nblk = jnp.clip((pick(group_size) - chunk * chunk_rows) // ROW_BLOCK, 0, EXPERT_MAX_BLOCKS)
    nblk = jnp.where(live, nblk, 0)
    n_f = D_FF // FF_TILE
    fidx = jnp.where(live[:, None], jnp.arange(n_f, dtype=jnp.int32)[None, :], n_f - 1).reshape(-1)
    return row0.astype(jnp.int32), nblk.astype(jnp.int32), exp.astype(jnp.int32), fidx.astype(jnp.int32)


def _moe(u2_p, u2_s, logits_t, x1_p, x1_s, gate2_p, gate2_s, ln2_g, ln2_b, w_up, b_up, w_down, b_down):
    n_tok = logits_t.shape[1]
    eidx, gates, rank, counts = _route(logits_t)
    count = counts[:, 0]
    group_size = (count + ROW_BLOCK - 1) // ROW_BLOCK * ROW_BLOCK
    ex = jnp.arange(N_EXPERTS, dtype=jnp.int32)
    group_end = jnp.sum(jnp.where(ex[None, :] <= ex[:, None], group_size[None, :], 0), axis=1)
    group_start = group_end - group_size
    dest = _dest(group_start, eidx, rank)
    n_tiles = n_tok // TOKEN_TILE
    dest_tiles = dest.reshape(TOP_K, n_tiles, TOKEN_TILE).transpose(1, 0, 2).reshape(n_tiles, 1, TOP_K * TOKEN_TILE)
    max_rows = n_tok * TOP_K + N_EXPERTS * (ROW_BLOCK - 1)
    n_rows = (max_rows + ROW_BLOCK - 1) // ROW_BLOCK * ROW_BLOCK
    n_items = N_EXPERTS + max_rows // (EXPERT_MAX_BLOCKS * ROW_BLOCK)
    xs = _dispatch(group_end, group_size, dest_tiles, u2_p, u2_s, n_rows)
    ys = _experts(_work_items(group_start, group_size, n_items), xs, w_up, b_up, w_down, b_down)
    return _combine(dest_tiles, gates.T, x1_p, x1_s, gate2_p, gate2_s, ln2_g, ln2_b, ys)


def _layer(x_p, x_s, cache_k, cache_v, state_conv, c_p, c_s,
           w_ada, b_ada, w_in, b_in, conv_w, sinks, norm_attn_g, norm_conv_g, w_out, b_out,
           ln1_g, ln1_b, w_router, b_router, w_up, b_up, w_down, b_down, ln2_g, ln2_b, past_len):
    T = x_p.shape[0]
    B = x_s.shape[0]
    n_c = 1 + B
    pad_c = (-n_c) % SUBLANES
    c_all = jnp.concatenate([c_p, c_s, jnp.zeros((pad_c, D_MODEL), F32)], axis=0)
    mod = _ada(c_all, w_ada, b_ada)
    shift1, scale1, gate1, shift2, scale2, gate2 = [mod[:, j * D_MODEL:(j + 1) * D_MODEL] for j in range(6)]

    def prompt(a):
        return a[0:1]

    def sample(a):
        return a[1:n_c]

    w_in_bf = w_in.astype(BF16)
    w_out_bf = w_out.astype(BF16)
    w_router_t_bf = w_router.T.astype(BF16)

    cos_p, sin_p = _rope_tables(jnp.arange(T, dtype=jnp.int32))
    q_p, k_p, v_p, gb_p, uc_p = _inproj(x_p, prompt(shift1), prompt(scale1), cos_p, sin_p, w_in_bf, b_in,
                                        min(INPROJ_TILE_M, T))
    attn_p = _attn_prompt(q_p, k_p, v_p, sinks, norm_attn_g)
    x1_p, u2_p, lg_p = _mix(attn_p, gb_p, uc_p, jnp.zeros((CONV_WIDTH - 1, CONV_DIM), F32), conv_w, norm_conv_g,
                            w_out_bf, b_out, x_p, prompt(gate1), ln1_g, ln1_b, prompt(shift2), prompt(scale2),
                            w_router_t_bf, b_router, min(MIX_TILE_M, T), True)

    cos_s, sin_s = _rope_tables(jnp.full((1,), past_len, jnp.int32))
    q_s, k_s, v_s, gb_s, uc_s = _inproj(x_s, sample(shift1), sample(scale1), cos_s, sin_s, w_in_bf, b_in, B)
    attn_s, new_k_s, new_v_s = _attn_decode(q_s, k_s, v_s, cache_k, cache_v, sinks, norm_attn_g)
    x1_s, u2_s, lg_s = _mix(attn_s.reshape(B, ATTN_DIM), gb_s, uc_s, (state_conv[:, 1], state_conv[:, 0]), conv_w,
                            norm_conv_g, w_out_bf, b_out, x_s, sample(gate1), ln1_g, ln1_b, sample(shift2),
                            sample(scale2), w_router_t_bf, b_router, B, False)

    y_p, y_s = _moe(u2_p, u2_s, jnp.concatenate([lg_p, lg_s], axis=1), x1_p, x1_s, prompt(gate2), sample(gate2),
                    ln2_g, ln2_b, w_up, b_up, w_down, b_down)

    new_k_p = k_p[T - WINDOW:].reshape(WINDOW, N_KV_HEADS, HEAD_DIM)
    new_v_p = v_p[T - WINDOW:].reshape(WINDOW, N_KV_HEADS, HEAD_DIM)
    new_conv_p = uc_p[T - (CONV_WIDTH - 1):]
    new_conv_s = jnp.stack([state_conv[:, 1], uc_s], axis=1)
    return (y_p, y_s, new_k_p, new_v_p, new_conv_p,
            new_k_s.reshape(B, WINDOW, N_KV_HEADS, HEAD_DIM), new_v_s.reshape(B, WINDOW, N_KV_HEADS, HEAD_DIM), new_conv_s)


def kernel(x_prompt, x_sample, cache_k, cache_v, state_conv, c_prompt, c_sample, w_ada, b_ada, w_in, b_in, conv_w, sinks, norm_attn_g, norm_conv_g, w_out, b_out, ln1_g, ln1_b, w_router, b_router, w_up, b_up, w_down, b_down, ln2_g, ln2_b):
    assert x_prompt.shape[0] == 1 and x_sample.shape[1] == 1 and w_ada.shape[0] == DEPTH == 1
    B = x_sample.shape[0]
    (y_p, y_s, nk_p, nv_p, nc_p, nk_s, nv_s, nc_s) = _layer(
        x_prompt[0], x_sample[:, 0], cache_k[0], cache_v[0], state_conv[0], c_prompt, c_sample,
        w_ada[0], b_ada[0], w_in[0], b_in[0], conv_w[0], sinks[0], norm_attn_g[0], norm_conv_g[0], w_out[0], b_out[0],
        ln1_g[0], ln1_b[0], w_router[0], b_router[0], w_up[0], b_up[0], w_down[0], b_down[0], ln2_g[0], ln2_b[0],
        PAST_LEN)
    return (y_p[None], y_s.reshape(B, 1, D_MODEL), nk_p[None, None], nv_p[None, None], nc_p[None, None],
            nk_s[None], nv_s[None], nc_s[None])
```

```python
import functools

import jax
import jax.numpy as jnp
from jax import lax
from jax.experimental import pallas as pl
from jax.experimental.pallas import tpu as pltpu

F32 = jnp.float32
BF16 = jnp.bfloat16

D_MODEL = 2048
HEAD_DIM = 64
N_HEADS = 16
N_KV_HEADS = 4
GQA_GROUP = N_HEADS // N_KV_HEADS
ATTN_DIM = N_HEADS * HEAD_DIM
KV_DIM = N_KV_HEADS * HEAD_DIM
CONV_DIM = D_MODEL - ATTN_DIM
CONV_WIDTH = 3
IN_DIM = ATTN_DIM + 2 * KV_DIM + 3 * CONV_DIM
WINDOW = 128
PAST_LEN = 16384
ROPE_THETA = 10000.0
N_EXPERTS = 32
TOP_K = 4
D_FF = D_MODEL
SWIGLU_LIMIT = 7.0
SWIGLU_ALPHA = 1.702
DEPTH = 1
DEEPNORM_ALPHA = (2.0 * DEPTH) ** 0.25
LN_EPS = 1e-5
RMS_EPS = 1e-6
COL_K = ATTN_DIM
COL_V = COL_K + KV_DIM
COL_B = COL_V + KV_DIM
COL_C = COL_B + CONV_DIM
COL_X = COL_C + CONV_DIM

LANES = 128
SUBLANES = 8
VMEM_LIMIT_BYTES = 60 * 1024 * 1024

TOKEN_TILE = 128
DISPATCH_SLOTS = 3
ADA_TILE_N = 1024
INPROJ_TILE_M = 512
INPROJ_CHUNK_N = 512
MIX_TILE_M = 256
DEC_TILE_B = 16
ROW_BLOCK = 256
EXPERT_MAX_BLOCKS = 9
FF_TILE = 256
NEG_BIG = -1e30


def _params(n_axes, vmem=VMEM_LIMIT_BYTES):
    return pltpu.CompilerParams(dimension_semantics=("arbitrary",) * n_axes, vmem_limit_bytes=vmem)


def _standardise(x):
    mu = jnp.mean(x, axis=-1, keepdims=True)
    xc = x - mu
    var = jnp.mean(xc * xc, axis=-1, keepdims=True)
    return xc * lax.rsqrt(var + LN_EPS)


def _rms(x, g):
    return x * lax.rsqrt(jnp.mean(x * x, axis=-1, keepdims=True) + RMS_EPS) * g


def _ada_kernel(c_ref, w_ref, b_ref, o_ref):
    c = c_ref[...]
    s = (c * jax.nn.sigmoid(c)).astype(BF16)
    o_ref[...] = jnp.dot(s, w_ref[...].astype(BF16), preferred_element_type=F32) + b_ref[...]


def _ada(c_all, w_ada, b_ada):
    rows = c_all.shape[0]
    n_out = w_ada.shape[1]
    return pl.pallas_call(
        _ada_kernel,
        grid=(n_out // ADA_TILE_N,),
        in_specs=[
            pl.BlockSpec((rows, D_MODEL), lambda j: (0, 0)),
            pl.BlockSpec((D_MODEL, ADA_TILE_N), lambda j: (0, j)),
            pl.BlockSpec((1, ADA_TILE_N), lambda j: (0, j)),
        ],
        out_specs=pl.BlockSpec((rows, ADA_TILE_N), lambda j: (0, j)),
        out_shape=jax.ShapeDtypeStruct((rows, n_out), F32),
        compiler_params=_params(1),
        name="ada",
    )(c_all, w_ada, b_ada.reshape(1, n_out))


def _inproj_kernel(x_ref, shift_ref, scale_ref, cos_ref, sin_ref, w_ref, b_ref,
                   q_ref, k_ref, v_ref, gb_ref, uc_ref):
    u = (_standardise(x_ref[...]) * (1.0 + scale_ref[...]) + shift_ref[...]).astype(BF16)
    cos = cos_ref[...]
    sin = sin_ref[...]
    lane = lax.broadcasted_iota(jnp.int32, (1, LANES), 1)
    first_half = (lane % HEAD_DIM) < (HEAD_DIM // 2)

    def rope(z):
        partner = jnp.where(first_half, pltpu.roll(z, LANES - HEAD_DIM // 2, axis=1),
                            pltpu.roll(z, HEAD_DIM // 2, axis=1))
        return z * cos + partner * sin

    def proj(c0):
        w = w_ref[:, c0:c0 + INPROJ_CHUNK_N]
        return jnp.dot(u, w, preferred_element_type=F32) + b_ref[:, c0:c0 + INPROJ_CHUNK_N]

    groups = INPROJ_CHUNK_N // LANES
    for j in range(ATTN_DIM // INPROJ_CHUNK_N):
        z = proj(j * INPROJ_CHUNK_N)
        for g in range(groups):
            c0 = j * INPROJ_CHUNK_N + g * LANES
            q_ref[:, c0:c0 + LANES] = (rope(z[:, g * LANES:(g + 1) * LANES]) * (HEAD_DIM ** -0.5)).astype(BF16)
    z = proj(COL_K)
    for g in range(KV_DIM // LANES):
        k_ref[:, g * LANES:(g + 1) * LANES] = rope(z[:, g * LANES:(g + 1) * LANES])
    v_ref[...] = z[:, KV_DIM:2 * KV_DIM]
    for j in range(CONV_DIM // INPROJ_CHUNK_N):
        sl = slice(j * INPROJ_CHUNK_N, (j + 1) * INPROJ_CHUNK_N)
        gb_ref[:, sl] = proj(COL_B + j * INPROJ_CHUNK_N)
        uc_ref[:, sl] = proj(COL_C + j * INPROJ_CHUNK_N) * proj(COL_X + j * INPROJ_CHUNK_N)


def _inproj(x, shift, scale, cos, sin, w_in_bf, b_in, tm):
    T = x.shape[0]
    per_row_mod = shift.shape[0] != 1
    per_row_pos = cos.shape[0] != 1
    mod_spec = pl.BlockSpec((tm, D_MODEL), lambda i: (i, 0)) if per_row_mod else pl.BlockSpec((1, D_MODEL), lambda i: (0, 0))
    pos_spec = pl.BlockSpec((tm, LANES), lambda i: (i, 0)) if per_row_pos else pl.BlockSpec((1, LANES), lambda i: (0, 0))

    def row_spec(width):
        return pl.BlockSpec((tm, width), lambda i: (i, 0))

    return pl.pallas_call(
        _inproj_kernel,
        grid=(T // tm,),
        in_specs=[
            row_spec(D_MODEL), mod_spec, mod_spec, pos_spec, pos_spec,
            pl.BlockSpec((D_MODEL, IN_DIM), lambda i: (0, 0), pipeline_mode=pl.Buffered(1)),
            pl.BlockSpec((1, IN_DIM), lambda i: (0, 0)),
        ],
        out_specs=[row_spec(ATTN_DIM), row_spec(KV_DIM), row_spec(KV_DIM), row_spec(CONV_DIM), row_spec(CONV_DIM)],
        out_shape=[
            jax.ShapeDtypeStruct((T, ATTN_DIM), BF16),
            jax.ShapeDtypeStruct((T, KV_DIM), F32),
            jax.ShapeDtypeStruct((T, KV_DIM), F32),
            jax.ShapeDtypeStruct((T, CONV_DIM), F32),
            jax.ShapeDtypeStruct((T, CONV_DIM), F32),
        ],
        compiler_params=_params(1),
        name="inproj",
    )(x, shift, scale, cos, sin, w_in_bf, b_in.reshape(1, IN_DIM))


def _attn_kernel(q_ref, kp_ref, kc_ref, vp_ref, vc_ref, bias_ref, sink_ref, g_ref, o_ref):
    q = q_ref[...]
    k = jnp.concatenate([kp_ref[...], kc_ref[...]], axis=0).astype(BF16)
    v = jnp.concatenate([vp_ref[...], vc_ref[...]], axis=0).astype(BF16)
    bias = bias_ref[...]
    outs = []
    for g in range(N_KV_HEADS):
        heads = range(g * GQA_GROUP, (g + 1) * GQA_GROUP)
        qg = jnp.concatenate([q[:, h * HEAD_DIM:(h + 1) * HEAD_DIM] for h in heads], axis=0)
        kg = k[:, g * HEAD_DIM:(g + 1) * HEAD_DIM]
        vg = v[:, g * HEAD_DIM:(g + 1) * HEAD_DIM]
        sink = sink_ref[g]
        s = lax.dot_general(kg, qg, (((1,), (1,)), ((), ())), preferred_element_type=F32) + bias
        m = jnp.maximum(jnp.max(s, axis=0, keepdims=True), sink)
        e = jnp.exp(s - m)
        den = jnp.sum(e, axis=0, keepdims=True) + jnp.exp(sink - m)
        o = lax.dot_general(vg, e.astype(BF16), (((0,), (0,)), ((), ())), preferred_element_type=F32)
        o = o / den
        outs.extend(o[:, j * WINDOW:(j + 1) * WINDOW] for j in range(GQA_GROUP))
    o = jnp.concatenate(outs, axis=0).T
    o_ref[...] = _rms(o, g_ref[...]).astype(BF16)


def _attn_mask_bias():
    qi = jnp.arange(GQA_GROUP * WINDOW, dtype=jnp.int32)[None, :] % WINDOW
    ks = jnp.arange(2 * WINDOW, dtype=jnp.int32)[:, None]
    later = (ks >= qi) & (ks <= qi + WINDOW)
    first = later & (ks >= WINDOW)
    return jnp.where(jnp.stack([first, later]), 0.0, NEG_BIG).astype(F32)


def _attn_prompt(q, k, v, sinks, norm_g):
    T = q.shape[0]

    def cur(width):
        return pl.BlockSpec((WINDOW, width), lambda n: (n, 0))

    def prev(width):
        return pl.BlockSpec((WINDOW, width), lambda n: (jnp.maximum(n - 1, 0), 0))

    return pl.pallas_call(
        _attn_kernel,
        grid=(T // WINDOW,),
        in_specs=[
            cur(ATTN_DIM), prev(KV_DIM), cur(KV_DIM), prev(KV_DIM), cur(KV_DIM),
            pl.BlockSpec((None, 2 * WINDOW, GQA_GROUP * WINDOW), lambda n: (jnp.minimum(n, 1), 0, 0)),
            pl.BlockSpec((N_KV_HEADS, 1, GQA_GROUP * WINDOW), lambda n: (0, 0, 0)),
            pl.BlockSpec((1, ATTN_DIM), lambda n: (0, 0)),
        ],
        out_specs=cur(ATTN_DIM),
        out_shape=jax.ShapeDtypeStruct((T, ATTN_DIM), BF16),
        compiler_params=_params(1),
        name="attn",
    )(q, k, k, v, v, _attn_mask_bias(),
      jnp.repeat(sinks, WINDOW).reshape(N_KV_HEADS, 1, GQA_GROUP * WINDOW),
      norm_g.reshape(1, ATTN_DIM))


def _attn_dec_kernel(q_ref, kn_ref, vn_ref, ck_ref, cv_ref, sink_ref, g_ref, o_ref, nk_ref, nv_ref):
    tb = q_ref.shape[0]
    q = q_ref[...].astype(F32)
    lane_group = lax.broadcasted_iota(jnp.int32, (1, N_HEADS, KV_DIM), 2) // HEAD_DIM
    head_group = lax.broadcasted_iota(jnp.int32, (1, N_HEADS, KV_DIM), 1) // GQA_GROUP
    own = lane_group == head_group
    qe = jnp.where(own, jnp.concatenate([q] * N_KV_HEADS, axis=2), 0.0)
    ck = ck_ref[...]
    cv = cv_ref[...]
    kn = kn_ref[...]
    vn = vn_ref[...]
    s = jnp.einsum("bhc,bwc->bhw", qe.astype(BF16), ck.astype(BF16), preferred_element_type=F32)
    s_new = jnp.sum(qe.astype(BF16).astype(F32) * kn.astype(BF16).astype(F32), axis=2, keepdims=True)
    sink = sink_ref[...]
    m = jnp.maximum(jnp.maximum(jnp.max(s, axis=2, keepdims=True), s_new), sink)
    e = jnp.exp(s - m)
    e_new = jnp.exp(s_new - m)
    den = jnp.sum(e, axis=2, keepdims=True) + e_new + jnp.exp(sink - m)
    p = (e / den).astype(BF16)
    p_new = (e_new / den).astype(BF16).astype(F32)
    o = jnp.einsum("bhw,bwc->bhc", p, cv.astype(BF16), preferred_element_type=F32)
    o = o + p_new * vn.astype(BF16).astype(F32)
    o = jnp.where(own, o, 0.0)
    oh = o[:, :, 0:HEAD_DIM]
    for g in range(1, N_KV_HEADS):
        oh = oh + o[:, :, g * HEAD_DIM:(g + 1) * HEAD_DIM]
    ms = jnp.sum(jnp.sum(oh * oh, axis=2, keepdims=True), axis=1, keepdims=True) / ATTN_DIM
    o_ref[...] = (oh * lax.rsqrt(ms + RMS_EPS) * g_ref[...]).astype(BF16)
    row = lax.broadcasted_iota(jnp.int32, (1, WINDOW, 1), 1)
    nk_ref[...] = jnp.where(row == WINDOW - 1, kn, pltpu.roll(ck, WINDOW - 1, axis=1))
    nv_ref[...] = jnp.where(row == WINDOW - 1, vn, pltpu.roll(cv, WINDOW - 1, axis=1))


def _attn_decode(q, k_new, v_new, cache_k, cache_v, sinks, norm_g):
    B = q.shape[0]
    tb = DEC_TILE_B

    def b3(d1, d2):
        return pl.BlockSpec((tb, d1, d2), lambda i: (i, 0, 0))

    return pl.pallas_call(
        _attn_dec_kernel,
        grid=(B // tb,),
        in_specs=[
            b3(N_HEADS, HEAD_DIM), b3(1, KV_DIM), b3(1, KV_DIM), b3(WINDOW, KV_DIM), b3(WINDOW, KV_DIM),
            pl.BlockSpec((1, N_HEADS, 1), lambda i: (0, 0, 0)),
            pl.BlockSpec((1, N_HEADS, HEAD_DIM), lambda i: (0, 0, 0)),
        ],
        out_specs=[b3(N_HEADS, HEAD_DIM), b3(WINDOW, KV_DIM), b3(WINDOW, KV_DIM)],
        out_shape=[
            jax.ShapeDtypeStruct((B, N_HEADS, HEAD_DIM), BF16),
            jax.ShapeDtypeStruct((B, WINDOW, KV_DIM), F32),
            jax.ShapeDtypeStruct((B, WINDOW, KV_DIM), F32),
        ],
        compiler_params=_params(1),
        name="attn_dec",
    )(q.reshape(B, N_HEADS, HEAD_DIM), k_new.reshape(B, 1, KV_DIM), v_new.reshape(B, 1, KV_DIM),
      cache_k.reshape(B, WINDOW, KV_DIM), cache_v.reshape(B, WINDOW, KV_DIM),
      sinks.reshape(1, N_HEADS, 1), norm_g.reshape(1, N_HEADS, HEAD_DIM))


def _mix_tail(attn, gb, uc, um1, um2, cw_ref, gconv_ref, wout_ref, bout_ref, x_ref, gate1_ref, ln1g_ref, ln1b_ref,
              shift2_ref, scale2_ref, wr_ref, br_ref, x1_ref, u2_ref, lg_ref):
    cw = cw_ref[...]
    conv = cw[0:1, :] * um2 + cw[1:2, :] * um1 + cw[2:3, :] * uc
    conv_n = _rms(gb * conv, gconv_ref[...]).astype(BF16)
    mixed = (jnp.dot(attn, wout_ref[0:ATTN_DIM, :], preferred_element_type=F32)
             + jnp.dot(conv_n, wout_ref[ATTN_DIM:D_MODEL, :], preferred_element_type=F32) + bout_ref[...])
    x1 = _standardise(DEEPNORM_ALPHA * x_ref[...] + gate1_ref[...] * mixed) * ln1g_ref[...] + ln1b_ref[...]
    x1_ref[...] = x1
    u2 = _standardise(x1) * (1.0 + scale2_ref[...]) + shift2_ref[...]
    u2_ref[...] = u2
    lg_ref[...] = lax.dot_general(wr_ref[...], u2.astype(BF16), (((1,), (1,)), ((), ())),
                                  preferred_element_type=F32) + br_ref[...]


def _mix_seq_kernel(attn_ref, gb_ref, uc_ref, halo_ref, hist_ref, *rest):
    i = pl.program_id(0)
    uc = uc_ref[...]
    tm = uc.shape[0]
    above = jnp.where(i == 0, hist_ref[...], halo_ref[...])
    row = lax.broadcasted_iota(jnp.int32, (tm, 1), 0)
    um1 = jnp.where(row == 0, above[7:8, :], pltpu.roll(uc, 1, axis=0))
    um2 = jnp.where(row == 0, above[6:7, :], jnp.where(row == 1, above[7:8, :], pltpu.roll(uc, 2, axis=0)))
    _mix_tail(attn_ref[...], gb_ref[...], uc, um1, um2, *rest)


def _mix_tok_kernel(attn_ref, gb_ref, uc_ref, um1_ref, um2_ref, *rest):
    _mix_tail(attn_ref[...], gb_ref[...], uc_ref[...], um1_ref[...], um2_ref[...], *rest)


def _mix(attn_n, gb, uc, conv_prev, conv_w, norm_conv_g, w_out_bf, b_out, x, gate1, ln1_g, ln1_b, shift2, scale2,
         w_router_t_bf, b_router, tm, sequential):
    T = x.shape[0]
    per_row_mod = gate1.shape[0] != 1

    def row_spec(width):
        return pl.BlockSpec((tm, width), lambda i: (i, 0))

    def const_spec(rows, width):
        return pl.BlockSpec((rows, width), lambda i: (0, 0))

    mod_spec = row_spec(D_MODEL) if per_row_mod else const_spec(1, D_MODEL)
    if sequential:
        hist8 = jnp.concatenate([jnp.zeros((SUBLANES - 2, CONV_DIM), F32), conv_prev], axis=0)
        halo_blocks = tm // SUBLANES
        conv_specs = [pl.BlockSpec((SUBLANES, CONV_DIM), lambda i: (jnp.maximum(i * halo_blocks - 1, 0), 0)),
                      const_spec(SUBLANES, CONV_DIM)]
        conv_args = (uc, hist8)
        body = _mix_seq_kernel
    else:
        conv_specs = [row_spec(CONV_DIM), row_spec(CONV_DIM)]
        conv_args = conv_prev
        body = _mix_tok_kernel
    return pl.pallas_call(
        body,
        grid=(T // tm,),
        in_specs=[row_spec(ATTN_DIM), row_spec(CONV_DIM), row_spec(CONV_DIM), *conv_specs,
                  const_spec(CONV_WIDTH, CONV_DIM), const_spec(1, CONV_DIM),
                  const_spec(D_MODEL, D_MODEL), const_spec(1, D_MODEL),
                  row_spec(D_MODEL), mod_spec, const_spec(1, D_MODEL), const_spec(1, D_MODEL),
                  mod_spec, mod_spec,
                  const_spec(N_EXPERTS, D_MODEL), const_spec(N_EXPERTS, 1)],
        out_specs=[row_spec(D_MODEL), row_spec(D_MODEL), pl.BlockSpec((N_EXPERTS, tm), lambda i: (0, i))],
        out_shape=[jax.ShapeDtypeStruct((T, D_MODEL), F32), jax.ShapeDtypeStruct((T, D_MODEL), F32),
                   jax.ShapeDtypeStruct((N_EXPERTS, T), F32)],
        compiler_params=_params(1),
        name="mix_seq" if sequential else "mix_tok",
    )(attn_n, gb, uc, *conv_args, conv_w, norm_conv_g.reshape(1, CONV_DIM), w_out_bf, b_out.reshape(1, D_MODEL),
      x, gate1, ln1_g.reshape(1, D_MODEL), ln1_b.reshape(1, D_MODEL), shift2, scale2,
      w_router_t_bf, b_router.reshape(N_EXPERTS, 1))


def _route_kernel(lg_ref, eidx_ref, gate_ref, rank_ref, cnt_ref):
    n_tok = lg_ref.shape[1]
    r = lax.broadcasted_iota(jnp.int32, (TOKEN_TILE, TOKEN_TILE), 0)
    c = lax.broadcasted_iota(jnp.int32, (TOKEN_TILE, TOKEN_TILE), 1)
    earlier = (r < c).astype(BF16)
    expert = lax.broadcasted_iota(jnp.int32, (N_EXPERTS, TOKEN_TILE), 0).astype(F32)

    def body(ci, count):
        off = pl.multiple_of(ci * TOKEN_TILE, TOKEN_TILE)
        l = lg_ref[:, pl.ds(off, TOKEN_TILE)]
        vals, idxs, sels = [], [], []
        for _ in range(TOP_K):
            m = jnp.max(l, axis=0, keepdims=True)
            idx = jnp.min(jnp.where(l == m, expert, float(N_EXPERTS)), axis=0, keepdims=True)
            sel = expert == idx
            vals.append(m)
            idxs.append(idx)
            sels.append(sel)
            l = jnp.where(sel, -jnp.inf, l)
        chosen = jnp.where(sels[0] | sels[1] | sels[2] | sels[3], 1.0, 0.0)
        before = jnp.dot(chosen.astype(BF16), earlier, preferred_element_type=F32) + count
        ranks = [jnp.sum(jnp.where(s, before, 0.0), axis=0, keepdims=True) for s in sels]
        es = [jnp.exp(v - vals[0]) for v in vals]
        den = es[0] + es[1] + es[2] + es[3]
        eidx_ref[:, pl.ds(off, TOKEN_TILE)] = jnp.concatenate(idxs, axis=0).astype(jnp.int32)
        rank_ref[:, pl.ds(off, TOKEN_TILE)] = jnp.concatenate(ranks, axis=0).astype(jnp.int32)
        gate_ref[:, pl.ds(off, TOKEN_TILE)] = jnp.concatenate([e / den for e in es], axis=0)
        return count + jnp.sum(chosen, axis=1, keepdims=True)

    count = lax.fori_loop(0, n_tok // TOKEN_TILE, body, jnp.zeros((N_EXPERTS, 1), F32))
    cnt_ref[...] = jnp.broadcast_to(count, (N_EXPERTS, LANES)).astype(jnp.int32)


def _route(logits_t):
    n_tok = logits_t.shape[1]
    return pl.pallas_call(
        _route_kernel,
        out_shape=[jax.ShapeDtypeStruct((TOP_K, n_tok), jnp.int32), jax.ShapeDtypeStruct((TOP_K, n_tok), F32),
                   jax.ShapeDtypeStruct((TOP_K, n_tok), jnp.int32), jax.ShapeDtypeStruct((N_EXPERTS, LANES), jnp.int32)],
        compiler_params=pltpu.CompilerParams(vmem_limit_bytes=VMEM_LIMIT_BYTES),
        name="route",
    )(logits_t)


def _dest_kernel(start_ref, eidx_ref, rank_ref, dest_ref):
    e = eidx_ref[...]
    base = jnp.zeros(e.shape, jnp.int32)
    for x in range(N_EXPERTS):
        base = jnp.where(e == x, start_ref[x], base)
    dest_ref[...] = base + rank_ref[...]


def _dest(group_start, eidx, rank):
    return pl.pallas_call(
        _dest_kernel,
        in_specs=[pl.BlockSpec(memory_space=pltpu.SMEM), pl.BlockSpec(memory_space=pltpu.VMEM),
                  pl.BlockSpec(memory_space=pltpu.VMEM)],
        out_specs=pl.BlockSpec(memory_space=pltpu.VMEM),
        out_shape=jax.ShapeDtypeStruct(eidx.shape, jnp.int32),
        name="dest",
    )(group_start, eidx, rank)


def _dispatch_kernel(n_prompt_tiles, n_tiles, gend_ref, gsize_ref, dest_ref, up_ref, us_ref, xs_ref,
                     zero_buf, tiles, zsem, load_sem, scat_sem):
    i = pl.program_id(0)
    slot = lax.rem(i, DISPATCH_SLOTS)

    def start_load(tile, s):
        @pl.when(tile < n_prompt_tiles)
        def _():
            r = pl.multiple_of(tile * TOKEN_TILE, TOKEN_TILE)
            pltpu.make_async_copy(up_ref.at[pl.ds(r, TOKEN_TILE), :], tiles.at[s], load_sem.at[s]).start()

        @pl.when(tile >= n_prompt_tiles)
        def _():
            r = pl.multiple_of((tile - n_prompt_tiles) * TOKEN_TILE, TOKEN_TILE)
            pltpu.make_async_copy(us_ref.at[pl.ds(r, TOKEN_TILE), :], tiles.at[s], load_sem.at[s]).start()

    def wait_load(s):
        pltpu.make_async_copy(up_ref.at[pl.ds(0, TOKEN_TILE), :], tiles.at[s], load_sem.at[s]).wait()

    def wait_scatter(s):
        for _ in range(TOP_K):
            pltpu.make_async_copy(tiles.at[s], xs_ref.at[pl.ds(0, TOKEN_TILE), :], scat_sem.at[s]).wait()

    @pl.when(i == 0)
    def _():
        start_load(0, 0)
        zero_buf[...] = jnp.zeros_like(zero_buf)
        for e in range(N_EXPERTS):
            @pl.when(gsize_ref[e] > 0)
            def _():
                r0 = pl.multiple_of(gend_ref[e] - ROW_BLOCK, ROW_BLOCK)
                pltpu.make_async_copy(zero_buf, xs_ref.at[pl.ds(r0, ROW_BLOCK), :], zsem).start()
        for e in range(N_EXPERTS):
            @pl.when(gsize_ref[e] > 0)
            def _():
                pltpu.make_async_copy(zero_buf, xs_ref.at[pl.ds(0, ROW_BLOCK), :], zsem).wait()

    @pl.when(i + 1 < n_tiles)
    def _():
        nxt = lax.rem(i + 1, DISPATCH_SLOTS)

        @pl.when(i + 1 >= DISPATCH_SLOTS)
        def _():
            wait_scatter(nxt)
        start_load(i + 1, nxt)

    wait_load(slot)
    for s in range(DISPATCH_SLOTS):
        @pl.when(slot == s)
        def _():
            for j in range(TOP_K * TOKEN_TILE):
                d = dest_ref[0, 0, j]
                pltpu.make_async_copy(tiles.at[s, pl.ds(j % TOKEN_TILE, 1), :], xs_ref.at[pl.ds(d, 1), :],
                                      scat_sem.at[s]).start()

    @pl.when(i == n_tiles - 1)
    def _():
        for tile in range(max(n_tiles - DISPATCH_SLOTS, 0), n_tiles):
            wait_scatter(tile % DISPATCH_SLOTS)


def _dispatch(group_end, group_size, dest_tiles, u2_p, u2_s, n_rows):
    n_p = u2_p.shape[0] // TOKEN_TILE
    n_s = u2_s.shape[0] // TOKEN_TILE
    grid_spec = pltpu.PrefetchScalarGridSpec(
        num_scalar_prefetch=2,
        grid=(n_p + n_s,),
        in_specs=[
            pl.BlockSpec((1, 1, TOP_K * TOKEN_TILE), lambda i, *_: (i, 0, 0), memory_space=pltpu.SMEM),
            pl.BlockSpec(memory_space=pl.ANY),
            pl.BlockSpec(memory_space=pl.ANY),
        ],
        out_specs=pl.BlockSpec(memory_space=pl.ANY),
        scratch_shapes=[pltpu.VMEM((ROW_BLOCK, D_MODEL), F32), pltpu.VMEM((DISPATCH_SLOTS, TOKEN_TILE, D_MODEL), F32),
                        pltpu.SemaphoreType.DMA(()), pltpu.SemaphoreType.DMA((DISPATCH_SLOTS,)),
                        pltpu.SemaphoreType.DMA((DISPATCH_SLOTS,))],
    )
    return pl.pallas_call(
        functools.partial(_dispatch_kernel, n_p, n_p + n_s),
        grid_spec=grid_spec,
        out_shape=jax.ShapeDtypeStruct((n_rows, D_MODEL), F32),
        compiler_params=_params(1),
        name="dispatch",
    )(group_end, group_size, dest_tiles, u2_p, u2_s)


_FIRST, _MIDDLE, _LAST = 0, 1, 2


def _experts_kernel(row0_ref, nblk_ref, exp_ref, fidx_ref, xs_ref, wg_ref, wl_ref, wd_ref, bg_ref, bl_ref, bd_ref,
                    ys_ref, xbuf, acc, in_sem, out_sem):
    i = pl.program_id(0)
    f = pl.program_id(1)
    last_f = pl.num_programs(1) - 1
    nblk = nblk_ref[i]
    row0 = row0_ref[i]

    def local(j):
        return pl.ds(pl.multiple_of(j * ROW_BLOCK, ROW_BLOCK), ROW_BLOCK)

    def in_hbm(j):
        return pl.ds(pl.multiple_of(row0 + j * ROW_BLOCK, ROW_BLOCK), ROW_BLOCK)

    def x_copy(j):
        return pltpu.make_async_copy(xs_ref.at[in_hbm(j), :], xbuf.at[local(j), :], in_sem.at[j])

    def y_copy(j):
        return pltpu.make_async_copy(acc.at[local(j), :], ys_ref.at[in_hbm(j), :], out_sem.at[j])

    def body(chunks, phase):
        wg = wg_ref[...].astype(BF16)
        wl = wl_ref[...].astype(BF16)
        wd = wd_ref[...].astype(BF16)
        for j0, m in chunks:
            rows = pl.ds(pl.multiple_of(j0 * ROW_BLOCK, ROW_BLOCK), m * ROW_BLOCK)
            if phase == _FIRST:
                for j in range(m):
                    x_copy(j0 + j).wait()
            x = xbuf[rows, :].astype(BF16)
            glu = jnp.dot(x, wg, preferred_element_type=F32) + bg_ref[...]
            lin = jnp.dot(x, wl, preferred_element_type=F32) + bl_ref[...]
            glu = jnp.minimum(glu, SWIGLU_LIMIT)
            lin = jnp.clip(lin, -SWIGLU_LIMIT, SWIGLU_LIMIT)
            act = glu * jax.nn.sigmoid(SWIGLU_ALPHA * glu) * (lin + 1.0)
            y = jnp.dot(act.astype(BF16), wd, preferred_element_type=F32)
            if phase == _FIRST:
                acc[rows, :] = y + bd_ref[...]
            else:
                acc[rows, :] += y
            if phase == _LAST:
                for j in range(m):
                    y_copy(j0 + j).start()

    def run(phase):
        done = 0
        if phase == _MIDDLE:
            def quad(q, carry):
                body(((4 * q, 2), (4 * q + 2, 2)), phase)
                return carry

            lax.fori_loop(0, lax.shift_right_logical(nblk, 2), quad, 0)
            done = lax.shift_left(lax.shift_right_logical(nblk, 2), 2)
            left = nblk - done

            @pl.when(jnp.bitwise_and(left, 2) == 2)
            def _():
                body(((done, 1), (done + 1, 1)), phase)
        else:
            def pair(p, carry):
                body(((2 * p, 1), (2 * p + 1, 1)), phase)
                return carry

            lax.fori_loop(0, lax.shift_right_logical(nblk, 1), pair, 0)

        @pl.when(jnp.bitwise_and(nblk, 1) == 1)
        def _():
            body(((nblk - 1, 1),), phase)

    @pl.when(f == 0)
    def _():
        def start(j, carry):
            x_copy(j).start()
            return carry
        lax.fori_loop(0, nblk, start, 0)
        run(_FIRST)

    @pl.when((f > 0) & (f < last_f))
    def _():
        run(_MIDDLE)

    @pl.when(f == last_f)
    def _():
        run(_LAST)

        def drain(j, carry):
            y_copy(j).wait()
            return carry
        lax.fori_loop(0, nblk, drain, 0)


def _experts(items, xs, w_up, b_up, w_down, b_down):
    row0, nblk, exp, fidx = items
    n_items = row0.shape[0]
    n_rows = xs.shape[0]
    n_f = D_FF // FF_TILE
    assert n_f >= 2

    def fi(i, f, fidx_ref):
        return fidx_ref[i * n_f + f]

    max_rows = EXPERT_MAX_BLOCKS * ROW_BLOCK
    grid_spec = pltpu.PrefetchScalarGridSpec(
        num_scalar_prefetch=4,
        grid=(n_items, n_f),
        in_specs=[
            pl.BlockSpec(memory_space=pl.ANY),
            pl.BlockSpec((None, D_MODEL, FF_TILE), lambda i, f, r0, nb, ex, fx: (ex[i], 0, fi(i, f, fx))),
            pl.BlockSpec((None, D_MODEL, FF_TILE), lambda i, f, r0, nb, ex, fx: (ex[i], 0, n_f + fi(i, f, fx))),
            pl.BlockSpec((None, FF_TILE, D_MODEL), lambda i, f, r0, nb, ex, fx: (ex[i], fi(i, f, fx), 0)),
            pl.BlockSpec((None, 1, FF_TILE), lambda i, f, r0, nb, ex, fx: (ex[i], 0, fi(i, f, fx))),
            pl.BlockSpec((None, 1, FF_TILE), lambda i, f, r0, nb, ex, fx: (ex[i], 0, n_f + fi(i, f, fx))),
            pl.BlockSpec((None, 1, D_MODEL), lambda i, f, r0, nb, ex, fx: (ex[i], 0, 0)),
        ],
        out_specs=pl.BlockSpec(memory_space=pl.ANY),
        scratch_shapes=[pltpu.VMEM((max_rows, D_MODEL), F32), pltpu.VMEM((max_rows, D_MODEL), F32),
                        pltpu.SemaphoreType.DMA((EXPERT_MAX_BLOCKS,)), pltpu.SemaphoreType.DMA((EXPERT_MAX_BLOCKS,))],
    )
    return pl.pallas_call(
        _experts_kernel,
        grid_spec=grid_spec,
        out_shape=jax.ShapeDtypeStruct((n_rows, D_MODEL), F32),
        compiler_params=_params(2),
        name="experts",
    )(row0, nblk, exp, fidx, xs, w_up, w_up, w_down,
      b_up.reshape(N_EXPERTS, 1, 2 * D_FF), b_up.reshape(N_EXPERTS, 1, 2 * D_FF), b_down.reshape(N_EXPERTS, 1, D_MODEL))


def _combine_kernel(n_prompt_tiles, dcur_ref, dnxt_ref, gates_ref, x1p_ref, x1s_ref, g2p_ref, g2s_ref, lng_ref, lnb_ref,
                    ys_ref, yp_ref, ysm_ref, buf, sem):
    i = pl.program_id(0)
    n = pl.num_programs(0)
    slot = i % 2

    def gather(idx_ref, s):
        for j in range(TOP_K * TOKEN_TILE):
            d = idx_ref[0, 0, j]
            pltpu.make_async_copy(ys_ref.at[pl.ds(d, 1), :],
                                  buf.at[s, j // TOKEN_TILE, pl.ds(j % TOKEN_TILE, 1), :], sem.at[s]).start()

    @pl.when(i == 0)
    def _():
        gather(dcur_ref, 0)

    for s in range(2):
        @pl.when((i + 1 < n) & (slot == 1 - s))
        def _():
            gather(dnxt_ref, s)

    for k in range(TOP_K):
        pltpu.make_async_copy(ys_ref.at[pl.ds(0, TOKEN_TILE), :], buf.at[slot, k], sem.at[slot]).wait()

    gates = gates_ref[...]
    ffn = buf[slot, 0] * gates[:, 0:1]
    for k in range(1, TOP_K):
        ffn = ffn + buf[slot, k] * gates[:, k:k + 1]

    def finish(x1, gate2, out_ref):
        out_ref[...] = _standardise(DEEPNORM_ALPHA * x1 + gate2 * ffn) * lng_ref[...] + lnb_ref[...]

    @pl.when(i < n_prompt_tiles)
    def _():
        finish(x1p_ref[...], g2p_ref[...], yp_ref)

    @pl.when(i >= n_prompt_tiles)
    def _():
        finish(x1s_ref[...], g2s_ref[...], ysm_ref)


def _combine(dest_tiles, gates_tok, x1_p, x1_s, gate2_p, gate2_s, ln2_g, ln2_b, ys):
    n_p = x1_p.shape[0] // TOKEN_TILE
    n_s = x1_s.shape[0] // TOKEN_TILE
    n = n_p + n_s

    def p_idx(i):
        return jnp.minimum(i, n_p - 1)

    def s_idx(i):
        return jnp.maximum(i - n_p, 0)

    smem_tile = (1, 1, TOP_K * TOKEN_TILE)
    return pl.pallas_call(
        functools.partial(_combine_kernel, n_p),
        grid=(n,),
        in_specs=[
            pl.BlockSpec(smem_tile, lambda i: (i, 0, 0), memory_space=pltpu.SMEM),
            pl.BlockSpec(smem_tile, lambda i: (jnp.minimum(i + 1, n - 1), 0, 0), memory_space=pltpu.SMEM),
            pl.BlockSpec((TOKEN_TILE, TOP_K), lambda i: (i, 0)),
            pl.BlockSpec((TOKEN_TILE, D_MODEL), lambda i: (p_idx(i), 0)),
            pl.BlockSpec((TOKEN_TILE, D_MODEL), lambda i: (s_idx(i), 0)),
            pl.BlockSpec((1, D_MODEL), lambda i: (0, 0)),
            pl.BlockSpec((TOKEN_TILE, D_MODEL), lambda i: (s_idx(i), 0)),
            pl.BlockSpec((1, D_MODEL), lambda i: (0, 0)),
            pl.BlockSpec((1, D_MODEL), lambda i: (0, 0)),
            pl.BlockSpec(memory_space=pl.ANY),
        ],
        out_specs=[pl.BlockSpec((TOKEN_TILE, D_MODEL), lambda i: (p_idx(i), 0)),
                   pl.BlockSpec((TOKEN_TILE, D_MODEL), lambda i: (s_idx(i), 0))],
        out_shape=[jax.ShapeDtypeStruct(x1_p.shape, F32), jax.ShapeDtypeStruct(x1_s.shape, F32)],
        scratch_shapes=[pltpu.VMEM((2, TOP_K, TOKEN_TILE, D_MODEL), F32), pltpu.SemaphoreType.DMA((2,))],
        compiler_params=_params(1),
        name="combine",
    )(dest_tiles, dest_tiles, gates_tok, x1_p, x1_s, gate2_p, gate2_s,
      ln2_g.reshape(1, D_MODEL), ln2_b.reshape(1, D_MODEL), ys)


def _rope_tables(pos):
    half = HEAD_DIM // 2
    inv_freq = ROPE_THETA ** (-jnp.arange(half, dtype=F32) / half)
    ang = pos.astype(F32)[:, None] * inv_freq
    cos, sin = jnp.cos(ang), jnp.sin(ang)
    reps = LANES // HEAD_DIM
    return (jnp.tile(jnp.concatenate([cos, cos], axis=1), (1, reps)),
            jnp.tile(jnp.concatenate([-sin, sin], axis=1), (1, reps)))


def _work_items(group_start, group_size, n_items):
    chunk_rows = EXPERT_MAX_BLOCKS * ROW_BLOCK
    ex = jnp.arange(N_EXPERTS, dtype=jnp.int32)
    n_chunks = (group_size + chunk_rows - 1) // chunk_rows
    chunk_end = jnp.sum(jnp.where(ex[None, :] <= ex[:, None], n_chunks[None, :], 0), axis=1)
    chunk_start = chunk_end - n_chunks
    item = jnp.arange(n_items, dtype=jnp.int32)
    live = item < chunk_end[-1]
    it = jnp.minimum(item, chunk_end[-1] - 1)
    exp = jnp.minimum(jnp.sum((chunk_end[None, :] <= it[:, None]).astype(jnp.int32), axis=1), N_EXPERTS - 1)
    own = exp[:, None] == ex[None, :]

    def pick(per_expert):
        return jnp.sum(jnp.where(own, per_expert[None, :], 0), axis=1)

    chunk = it - pick(chunk_start)
    row0 = pick(group_start) + chunk * chunk_rows
    nblk = jnp.clip((pick(group_size) - chunk * chunk_rows) // ROW_BLOCK, 0, EXPERT_MAX_BLOCKS)
    nblk = jnp.where(live, nblk, 0)
    n_f = D_FF // FF_TILE
    fidx = jnp.where(live[:, None], jnp.arange(n_f, dtype=jnp.int32)[None, :], n_f - 1).reshape(-1)
    return row0.astype(jnp.int32), nblk.astype(jnp.int32), exp.astype(jnp.int32), fidx.astype(jnp.int32)


def _moe(u2_p, u2_s, logits_t, x1_p, x1_s, gate2_p, gate2_s, ln2_g, ln2_b, w_up, b_up, w_down, b_down):
    n_tok = logits_t.shape[1]
    eidx, gates, rank, counts = _route(logits_t)
    count = counts[:, 0]
    group_size = (count + ROW_BLOCK - 1) // ROW_BLOCK * ROW_BLOCK
    ex = jnp.arange(N_EXPERTS, dtype=jnp.int32)
    group_end = jnp.sum(jnp.where(ex[None, :] <= ex[:, None], group_size[None, :], 0), axis=1)
    group_start = group_end - group_size
    dest = _dest(group_start, eidx, rank)
    n_tiles = n_tok // TOKEN_TILE
    dest_tiles = dest.reshape(TOP_K, n_tiles, TOKEN_TILE).transpose(1, 0, 2).reshape(n_tiles, 1, TOP_K * TOKEN_TILE)
    max_rows = n_tok * TOP_K + N_EXPERTS * (ROW_BLOCK - 1)
    n_rows = (max_rows + ROW_BLOCK - 1) // ROW_BLOCK * ROW_BLOCK
    n_items = N_EXPERTS + max_rows // (EXPERT_MAX_BLOCKS * ROW_BLOCK)
    xs = _dispatch(group_end, group_size, dest_tiles, u2_p, u2_s, n_rows)
    ys = _experts(_work_items(group_start, group_size, n_items), xs, w_up, b_up, w_down, b_down)
    return _combine(dest_tiles, gates.T, x1_p, x1_s, gate2_p, gate2_s, ln2_g, ln2_b, ys)


def _layer(x_p, x_s, cache_k, cache_v, state_conv, c_p, c_s,
           w_ada, b_ada, w_in, b_in, conv_w, sinks, norm_attn_g, norm_conv_g, w_out, b_out,
           ln1_g, ln1_b, w_router, b_router, w_up, b_up, w_down, b_down, ln2_g, ln2_b, past_len):
    T = x_p.shape[0]
    B = x_s.shape[0]
    n_c = 1 + B
    pad_c = (-n_c) % SUBLANES
    c_all = jnp.concatenate([c_p, c_s, jnp.zeros((pad_c, D_MODEL), F32)], axis=0)
    mod = _ada(c_all, w_ada, b_ada)
    shift1, scale1, gate1, shift2, scale2, gate2 = [mod[:, j * D_MODEL:(j + 1) * D_MODEL] for j in range(6)]

    def prompt(a):
        return a[0:1]

    def sample(a):
        return a[1:n_c]

    w_in_bf = w_in.astype(BF16)
    w_out_bf = w_out.astype(BF16)
    w_router_t_bf = w_router.T.astype(BF16)

    cos_p, sin_p = _rope_tables(jnp.arange(T, dtype=jnp.int32))
    q_p, k_p, v_p, gb_p, uc_p = _inproj(x_p, prompt(shift1), prompt(scale1), cos_p, sin_p, w_in_bf, b_in,
                                        min(INPROJ_TILE_M, T))
    attn_p = _attn_prompt(q_p, k_p, v_p, sinks, norm_attn_g)
    x1_p, u2_p, lg_p = _mix(attn_p, gb_p, uc_p, jnp.zeros((CONV_WIDTH - 1, CONV_DIM), F32), conv_w, norm_conv_g,
                            w_out_bf, b_out, x_p, prompt(gate1), ln1_g, ln1_b, prompt(shift2), prompt(scale2),
                            w_router_t_bf, b_router, min(MIX_TILE_M, T), True)

    cos_s, sin_s = _rope_tables(jnp.full((1,), past_len, jnp.int32))
    q_s, k_s, v_s, gb_s, uc_s = _inproj(x_s, sample(shift1), sample(scale1), cos_s, sin_s, w_in_bf, b_in, B)
    attn_s, new_k_s, new_v_s = _attn_decode(q_s, k_s, v_s, cache_k, cache_v, sinks, norm_attn_g)
    x1_s, u2_s, lg_s = _mix(attn_s.reshape(B, ATTN_DIM), gb_s, uc_s, (state_conv[:, 1], state_conv[:, 0]), conv_w,
                            norm_conv_g, w_out_bf, b_out, x_s, sample(gate1), ln1_g, ln1_b, sample(shift2),
                            sample(scale2), w_router_t_bf, b_router, B, False)

    y_p, y_s = _moe(u2_p, u2_s, jnp.concatenate([lg_p, lg_s], axis=1), x1_p, x1_s, prompt(gate2), sample(gate2),
                    ln2_g, ln2_b, w_up, b_up, w_down, b_down)

    new_k_p = k_p[T - WINDOW:].reshape(WINDOW, N_KV_HEADS, HEAD_DIM)
    new_v_p = v_p[T - WINDOW:].reshape(WINDOW, N_KV_HEADS, HEAD_DIM)
    new_conv_p = uc_p[T - (CONV_WIDTH - 1):]
    new_conv_s = jnp.stack([state_conv[:, 1], uc_s], axis=1)
    return (y_p, y_s, new_k_p, new_v_p, new_conv_p,
            new_k_s.reshape(B, WINDOW, N_KV_HEADS, HEAD_DIM), new_v_s.reshape(B, WINDOW, N_KV_HEADS, HEAD_DIM), new_conv_s)


def kernel(x_prompt, x_sample, cache_k, cache_v, state_conv, c_prompt, c_sample, w_ada, b_ada, w_in, b_in, conv_w, sinks, norm_attn_g, norm_conv_g, w_out, b_out, ln1_g, ln1_b, w_router, b_router, w_up, b_up, w_down, b_down, ln2_g, ln2_b):
    assert x_prompt.shape[0] == 1 and x_sample.shape[1] == 1 and w_ada.shape[0] == DEPTH == 1
    B = x_sample.shape[0]
    (y_p, y_s, nk_p, nv_p, nc_p, nk_s, nv_s, nc_s) = _layer(
        x_prompt[0], x_sample[:, 0], cache_k[0], cache_v[0], state_conv[0], c_prompt, c_sample,
        w_ada[0], b_ada[0], w_in[0], b_in[0], conv_w[0], sinks[0], norm_attn_g[0], norm_conv_g[0], w_out[0], b_out[0],
        ln1_g[0], ln1_b[0], w_router[0], b_router[0], w_up[0], b_up[0], w_down[0], b_down[0], ln2_g[0], ln2_b[0],
        PAST_LEN)
    return (y_p[None], y_s.reshape(B, 1, D_MODEL), nk_p[None, None], nv_p[None, None], nc_p[None, None],
            nk_s[None], nv_s[None], nc_s[None])
```

```python
import functools

import jax
import jax.numpy as jnp
from jax import lax
from jax.experimental import pallas as pl
from jax.experimental.pallas import tpu as pltpu

F32 = jnp.float32
BF16 = jnp.bfloat16

D_MODEL = 2048
HEAD_DIM = 64
N_HEADS = 16
N_KV_HEADS = 4
GQA_GROUP = N_HEADS // N_KV_HEADS
ATTN_DIM = N_HEADS * HEAD_DIM
KV_DIM = N_KV_HEADS * HEAD_DIM
CONV_DIM = D_MODEL - ATTN_DIM
CONV_WIDTH = 3
IN_DIM = ATTN_DIM + 2 * KV_DIM + 3 * CONV_DIM
WINDOW = 128
PAST_LEN = 16384
ROPE_THETA = 10000.0
N_EXPERTS = 32
TOP_K = 4
D_FF = D_MODEL
SWIGLU_LIMIT = 7.0
SWIGLU_ALPHA = 1.702
DEPTH = 1
DEEPNORM_ALPHA = (2.0 * DEPTH) ** 0.25
LN_EPS = 1e-5
RMS_EPS = 1e-6
COL_K = ATTN_DIM
COL_V = COL_K + KV_DIM
COL_B = COL_V + KV_DIM
COL_C = COL_B + CONV_DIM
COL_X = COL_C + CONV_DIM

LANES = 128
SUBLANES = 8
VMEM_LIMIT_BYTES = 60 * 1024 * 1024

TOKEN_TILE = 128
DISPATCH_SLOTS = 3
ADA_TILE_N = 1024
INPROJ_TILE_M = 512
INPROJ_CHUNK_N = 512
MIX_TILE_M = 256
DEC_TILE_B = 16
ROW_BLOCK = 256
EXPERT_MAX_BLOCKS = 9
FF_TILE = 256
NEG_BIG = -1e30


def _params(n_axes, vmem=VMEM_LIMIT_BYTES):
    return pltpu.CompilerParams(dimension_semantics=("arbitrary",) * n_axes, vmem_limit_bytes=vmem)


def _standardise(x):
    mu = jnp.mean(x, axis=-1, keepdims=True)
    xc = x - mu
    var = jnp.mean(xc * xc, axis=-1, keepdims=True)
    return xc * lax.rsqrt(var + LN_EPS)


def _rms(x, g):
    return x * lax.rsqrt(jnp.mean(x * x, axis=-1, keepdims=True) + RMS_EPS) * g


def _ada_kernel(c_ref, w_ref, b_ref, o_ref):
    c = c_ref[...]
    s = (c * jax.nn.sigmoid(c)).astype(BF16)
    o_ref[...] = jnp.dot(s, w_ref[...].astype(BF16), preferred_element_type=F32) + b_ref[...]


def _ada(c_all, w_ada, b_ada):
    rows = c_all.shape[0]
    n_out = w_ada.shape[1]
    return pl.pallas_call(
        _ada_kernel,
        grid=(n_out // ADA_TILE_N,),
        in_specs=[
            pl.BlockSpec((rows, D_MODEL), lambda j: (0, 0)),
            pl.BlockSpec((D_MODEL, ADA_TILE_N), lambda j: (0, j)),
            pl.BlockSpec((1, ADA_TILE_N), lambda j: (0, j)),
        ],
        out_specs=pl.BlockSpec((rows, ADA_TILE_N), lambda j: (0, j)),
        out_shape=jax.ShapeDtypeStruct((rows, n_out), F32),
        compiler_params=_params(1),
        name="ada",
    )(c_all, w_ada, b_ada.reshape(1, n_out))


def _inproj_kernel(x_ref, shift_ref, scale_ref, cos_ref, sin_ref, w_ref, b_ref,
                   q_ref, k_ref, v_ref, gb_ref, uc_ref):
    u = (_standardise(x_ref[...]) * (1.0 + scale_ref[...]) + shift_ref[...]).astype(BF16)
    cos = cos_ref[...]
    sin = sin_ref[...]
    lane = lax.broadcasted_iota(jnp.int32, (1, LANES), 1)
    first_half = (lane % HEAD_DIM) < (HEAD_DIM // 2)

    def rope(z):
        partner = jnp.where(first_half, pltpu.roll(z, LANES - HEAD_DIM // 2, axis=1),
                            pltpu.roll(z, HEAD_DIM // 2, axis=1))
        return z * cos + partner * sin

    def proj(c0):
        w = w_ref[:, c0:c0 + INPROJ_CHUNK_N]
        return jnp.dot(u, w, preferred_element_type=F32) + b_ref[:, c0:c0 + INPROJ_CHUNK_N]

    groups = INPROJ_CHUNK_N // LANES
    for j in range(ATTN_DIM // INPROJ_CHUNK_N):
        z = proj(j * INPROJ_CHUNK_N)
        for g in range(groups):
            c0 = j * INPROJ_CHUNK_N + g * LANES
            q_ref[:, c0:c0 + LANES] = (rope(z[:, g * LANES:(g + 1) * LANES]) * (HEAD_DIM ** -0.5)).astype(BF16)
    z = proj(COL_K)
    for g in range(KV_DIM // LANES):
        k_ref[:, g * LANES:(g + 1) * LANES] = rope(z[:, g * LANES:(g + 1) * LANES])
    v_ref[...] = z[:, KV_DIM:2 * KV_DIM]
    for j in range(CONV_DIM // INPROJ_CHUNK_N):
        sl = slice(j * INPROJ_CHUNK_N, (j + 1) * INPROJ_CHUNK_N)
        gb_ref[:, sl] = proj(COL_B + j * INPROJ_CHUNK_N)
        uc_ref[:, sl] = proj(COL_C + j * INPROJ_CHUNK_N) * proj(COL_X + j * INPROJ_CHUNK_N)


def _inproj(x, shift, scale, cos, sin, w_in_bf, b_in, tm):
    T = x.shape[0]
    per_row_mod = shift.shape[0] != 1
    per_row_pos = cos.shape[0] != 1
    mod_spec = pl.BlockSpec((tm, D_MODEL), lambda i: (i, 0)) if per_row_mod else pl.BlockSpec((1, D_MODEL), lambda i: (0, 0))
    pos_spec = pl.BlockSpec((tm, LANES), lambda i: (i, 0)) if per_row_pos else pl.BlockSpec((1, LANES), lambda i: (0, 0))

    def row_spec(width):
        return pl.BlockSpec((tm, width), lambda i: (i, 0))

    return pl.pallas_call(
        _inproj_kernel,
        grid=(T // tm,),
        in_specs=[
            row_spec(D_MODEL), mod_spec, mod_spec, pos_spec, pos_spec,
            pl.BlockSpec((D_MODEL, IN_DIM), lambda i: (0, 0), pipeline_mode=pl.Buffered(1)),
            pl.BlockSpec((1, IN_DIM), lambda i: (0, 0)),
        ],
        out_specs=[row_spec(ATTN_DIM), row_spec(KV_DIM), row_spec(KV_DIM), row_spec(CONV_DIM), row_spec(CONV_DIM)],
        out_shape=[
            jax.ShapeDtypeStruct((T, ATTN_DIM), BF16),
            jax.ShapeDtypeStruct((T, KV_DIM), F32),
            jax.ShapeDtypeStruct((T, KV_DIM), F32),
            jax.ShapeDtypeStruct((T, CONV_DIM), F32),
            jax.ShapeDtypeStruct((T, CONV_DIM), F32),
        ],
        compiler_params=_params(1),
        name="inproj",
    )(x, shift, scale, cos, sin, w_in_bf, b_in.reshape(1, IN_DIM))


def _attn_kernel(q_ref, kp_ref, kc_ref, vp_ref, vc_ref, bias_ref, sink_ref, g_ref, o_ref):
    q = q_ref[...]
    k = jnp.concatenate([kp_ref[...], kc_ref[...]], axis=0).astype(BF16)
    v = jnp.concatenate([vp_ref[...], vc_ref[...]], axis=0).astype(BF16)
    bias = bias_ref[...]
    outs = []
    for g in range(N_KV_HEADS):
        heads = range(g * GQA_GROUP, (g + 1) * GQA_GROUP)
        qg = jnp.concatenate([q[:, h * HEAD_DIM:(h + 1) * HEAD_DIM] for h in heads], axis=0)
        kg = k[:, g * HEAD_DIM:(g + 1) * HEAD_DIM]
        vg = v[:, g * HEAD_DIM:(g + 1) * HEAD_DIM]
        sink = sink_ref[g]
        s = lax.dot_general(kg, qg, (((1,), (1,)), ((), ())), preferred_element_type=F32) + bias
        m = jnp.maximum(jnp.max(s, axis=0, keepdims=True), sink)
        e = jnp.exp(s - m)
        den = jnp.sum(e, axis=0, keepdims=True) + jnp.exp(sink - m)
        o = lax.dot_general(vg, e.astype(BF16), (((0,), (0,)), ((), ())), preferred_element_type=F32)
        o = o / den
        outs.extend(o[:, j * WINDOW:(j + 1) * WINDOW] for j in range(GQA_GROUP))
    o = jnp.concatenate(outs, axis=0).T
    o_ref[...] = _rms(o, g_ref[...]).astype(BF16)


def _attn_mask_bias():
    qi = jnp.arange(GQA_GROUP * WINDOW, dtype=jnp.int32)[None, :] % WINDOW
    ks = jnp.arange(2 * WINDOW, dtype=jnp.int32)[:, None]
    later = (ks >= qi) & (ks <= qi + WINDOW)
    first = later & (ks >= WINDOW)
    return jnp.where(jnp.stack([first, later]), 0.0, NEG_BIG).astype(F32)


def _attn_prompt(q, k, v, sinks, norm_g):
    T = q.shape[0]

    def cur(width):
        return pl.BlockSpec((WINDOW, width), lambda n: (n, 0))

    def prev(width):
        return pl.BlockSpec((WINDOW, width), lambda n: (jnp.maximum(n - 1, 0), 0))

    return pl.pallas_call(
        _attn_kernel,
        grid=(T // WINDOW,),
        in_specs=[
            cur(ATTN_DIM), prev(KV_DIM), cur(KV_DIM), prev(KV_DIM), cur(KV_DIM),
            pl.BlockSpec((None, 2 * WINDOW, GQA_GROUP * WINDOW), lambda n: (jnp.minimum(n, 1), 0, 0)),
            pl.BlockSpec((N_KV_HEADS, 1, GQA_GROUP * WINDOW), lambda n: (0, 0, 0)),
            pl.BlockSpec((1, ATTN_DIM), lambda n: (0, 0)),
        ],
        out_specs=cur(ATTN_DIM),
        out_shape=jax.ShapeDtypeStruct((T, ATTN_DIM), BF16),
        compiler_params=_params(1),
        name="attn",
    )(q, k, k, v, v, _attn_mask_bias(),
      jnp.repeat(sinks, WINDOW).reshape(N_KV_HEADS, 1, GQA_GROUP * WINDOW),
      norm_g.reshape(1, ATTN_DIM))


def _attn_dec_kernel(q_ref, kn_ref, vn_ref, ck_ref, cv_ref, sink_ref, g_ref, o_ref, nk_ref, nv_ref):
    tb = q_ref.shape[0]
    q = q_ref[...].astype(F32)
    lane_group = lax.broadcasted_iota(jnp.int32, (1, N_HEADS, KV_DIM), 2) // HEAD_DIM
    head_group = lax.broadcasted_iota(jnp.int32, (1, N_HEADS, KV_DIM), 1) // GQA_GROUP
    own = lane_group == head_group
    qe = jnp.where(own, jnp.concatenate([q] * N_KV_HEADS, axis=2), 0.0)
    ck = ck_ref[...]
    cv = cv_ref[...]
    kn = kn_ref[...]
    vn = vn_ref[...]
    s = jnp.einsum("bhc,bwc->bhw", qe.astype(BF16), ck.astype(BF16), preferred_element_type=F32)
    s_new = jnp.sum(qe.astype(BF16).astype(F32) * kn.astype(BF16).astype(F32), axis=2, keepdims=True)
    sink = sink_ref[...]
    m = jnp.maximum(jnp.maximum(jnp.max(s, axis=2, keepdims=True), s_new), sink)
    e = jnp.exp(s - m)
    e_new = jnp.exp(s_new - m)
    den = jnp.sum(e, axis=2, keepdims=True) + e_new + jnp.exp(sink - m)
    p = (e / den).astype(BF16)
    p_new = (e_new / den).astype(BF16).astype(F32)
    o = jnp.einsum("bhw,bwc->bhc", p, cv.astype(BF16), preferred_element_type=F32)
    o = o + p_new * vn.astype(BF16).astype(F32)
    o = jnp.where(own, o, 0.0)
    oh = o[:, :, 0:HEAD_DIM]
    for g in range(1, N_KV_HEADS):
        oh = oh + o[:, :, g * HEAD_DIM:(g + 1) * HEAD_DIM]
    ms = jnp.sum(jnp.sum(oh * oh, axis=2, keepdims=True), axis=1, keepdims=True) / ATTN_DIM
    o_ref[...] = (oh * lax.rsqrt(ms + RMS_EPS) * g_ref[...]).astype(BF16)
    row = lax.broadcasted_iota(jnp.int32, (1, WINDOW, 1), 1)
    nk_ref[...] = jnp.where(row == WINDOW - 1, kn, pltpu.roll(ck, WINDOW - 1, axis=1))
    nv_ref[...] = jnp.where(row == WINDOW - 1, vn, pltpu.roll(cv, WINDOW - 1, axis=1))


def _attn_decode(q, k_new, v_new, cache_k, cache_v, sinks, norm_g):
    B = q.shape[0]
    tb = DEC_TILE_B

    def b3(d1, d2):
        return pl.BlockSpec((tb, d1, d2), lambda i: (i, 0, 0))

    return pl.pallas_call(
        _attn_dec_kernel,
        grid=(B // tb,),
        in_specs=[
            b3(N_HEADS, HEAD_DIM), b3(1, KV_DIM), b3(1, KV_DIM), b3(WINDOW, KV_DIM), b3(WINDOW, KV_DIM),
            pl.BlockSpec((1, N_HEADS, 1), lambda i: (0, 0, 0)),
            pl.BlockSpec((1, N_HEADS, HEAD_DIM), lambda i: (0, 0, 0)),
        ],
        out_specs=[b3(N_HEADS, HEAD_DIM), b3(WINDOW, KV_DIM), b3(WINDOW, KV_DIM)],
        out_shape=[
            jax.ShapeDtypeStruct((B, N_HEADS, HEAD_DIM), BF16),
            jax.ShapeDtypeStruct((B, WINDOW, KV_DIM), F32),
            jax.ShapeDtypeStruct((B, WINDOW, KV_DIM), F32),
        ],
        compiler_params=_params(1),
        name="attn_dec",
    )(q.reshape(B, N_HEADS, HEAD_DIM), k_new.reshape(B, 1, KV_DIM), v_new.reshape(B, 1, KV_DIM),
      cache_k.reshape(B, WINDOW, KV_DIM), cache_v.reshape(B, WINDOW, KV_DIM),
      sinks.reshape(1, N_HEADS, 1), norm_g.reshape(1, N_HEADS, HEAD_DIM))


def _mix_tail(attn, gb, uc, um1, um2, cw_ref, gconv_ref, wout_ref, bout_ref, x_ref, gate1_ref, ln1g_ref, ln1b_ref,
              shift2_ref, scale2_ref, wr_ref, br_ref, x1_ref, u2_ref, lg_ref):
    cw = cw_ref[...]
    conv = cw[0:1, :] * um2 + cw[1:2, :] * um1 + cw[2:3, :] * uc
    conv_n = _rms(gb * conv, gconv_ref[...]).astype(BF16)
    mixed = (jnp.dot(attn, wout_ref[0:ATTN_DIM, :], preferred_element_type=F32)
             + jnp.dot(conv_n, wout_ref[ATTN_DIM:D_MODEL, :], preferred_element_type=F32) + bout_ref[...])
    x1 = _standardise(DEEPNORM_ALPHA * x_ref[...] + gate1_ref[...] * mixed) * ln1g_ref[...] + ln1b_ref[...]
    x1_ref[...] = x1
    u2 = _standardise(x1) * (1.0 + scale2_ref[...]) + shift2_ref[...]
    u2_ref[...] = u2
    lg_ref[...] = lax.dot_general(wr_ref[...], u2.astype(BF16), (((1,), (1,)), ((), ())),
                                  preferred_element_type=F32) + br_ref[...]


def _mix_seq_kernel(attn_ref, gb_ref, uc_ref, halo_ref, hist_ref, *rest):
    i = pl.program_id(0)
    uc = uc_ref[...]
    tm = uc.shape[0]
    above = jnp.where(i == 0, hist_ref[...], halo_ref[...])
    row = lax.broadcasted_iota(jnp.int32, (tm, 1), 0)
    um1 = jnp.where(row == 0, above[7:8, :], pltpu.roll(uc, 1, axis=0))
    um2 = jnp.where(row == 0, above[6:7, :], jnp.where(row == 1, above[7:8, :], pltpu.roll(uc, 2, axis=0)))
    _mix_tail(attn_ref[...], gb_ref[...], uc, um1, um2, *rest)


def _mix_tok_kernel(attn_ref, gb_ref, uc_ref, um1_ref, um2_ref, *rest):
    _mix_tail(attn_ref[...], gb_ref[...], uc_ref[...], um1_ref[...], um2_ref[...], *rest)


def _mix(attn_n, gb, uc, conv_prev, conv_w, norm_conv_g, w_out_bf, b_out, x, gate1, ln1_g, ln1_b, shift2, scale2,
         w_router_t_bf, b_router, tm, sequential):
    T = x.shape[0]
    per_row_mod = gate1.shape[0] != 1

    def row_spec(width):
        return pl.BlockSpec((tm, width), lambda i: (i, 0))

    def const_spec(rows, width):
        return pl.BlockSpec((rows, width), lambda i: (0, 0))

    mod_spec = row_spec(D_MODEL) if per_row_mod else const_spec(1, D_MODEL)
    if sequential:
        hist8 = jnp.concatenate([jnp.zeros((SUBLANES - 2, CONV_DIM), F32), conv_prev], axis=0)
        halo_blocks = tm // SUBLANES
        conv_specs = [pl.BlockSpec((SUBLANES, CONV_DIM), lambda i: (jnp.maximum(i * halo_blocks - 1, 0), 0)),
                      const_spec(SUBLANES, CONV_DIM)]
        conv_args = (uc, hist8)
        body = _mix_seq_kernel
    else:
        conv_specs = [row_spec(CONV_DIM), row_spec(CONV_DIM)]
        conv_args = conv_prev
        body = _mix_tok_kernel
    return pl.pallas_call(
        body,
        grid=(T // tm,),
        in_specs=[row_spec(ATTN_DIM), row_spec(CONV_DIM), row_spec(CONV_DIM), *conv_specs,
                  const_spec(CONV_WIDTH, CONV_DIM), const_spec(1, CONV_DIM),
                  const_spec(D_MODEL, D_MODEL), const_spec(1, D_MODEL),
                  row_spec(D_MODEL), mod_spec, const_spec(1, D_MODEL), const_spec(1, D_MODEL),
                  mod_spec, mod_spec,
                  const_spec(N_EXPERTS, D_MODEL), const_spec(N_EXPERTS, 1)],
        out_specs=[row_spec(D_MODEL), row_spec(D_MODEL), pl.BlockSpec((N_EXPERTS, tm), lambda i: (0, i))],
        out_shape=[jax.ShapeDtypeStruct((T, D_MODEL), F32), jax.ShapeDtypeStruct((T, D_MODEL), F32),
                   jax.ShapeDtypeStruct((N_EXPERTS, T), F32)],
        compiler_params=_params(1),
        name="mix_seq" if sequential else "mix_tok",
    )(attn_n, gb, uc, *conv_args, conv_w, norm_conv_g.reshape(1, CONV_DIM), w_out_bf, b_out.reshape(1, D_MODEL),
      x, gate1, ln1_g.reshape(1, D_MODEL), ln1_b.reshape(1, D_MODEL), shift2, scale2,
      w_router_t_bf, b_router.reshape(N_EXPERTS, 1))


def _route_kernel(lg_ref, eidx_ref, gate_ref, rank_ref, cnt_ref):
    n_tok = lg_ref.shape[1]
    r = lax.broadcasted_iota(jnp.int32, (TOKEN_TILE, TOKEN_TILE), 0)
    c = lax.broadcasted_iota(jnp.int32, (TOKEN_TILE, TOKEN_TILE), 1)
    earlier = (r < c).astype(BF16)
    expert = lax.broadcasted_iota(jnp.int32, (N_EXPERTS, TOKEN_TILE), 0).astype(F32)

    def body(ci, count):
        off = pl.multiple_of(ci * TOKEN_TILE, TOKEN_TILE)
        l = lg_ref[:, pl.ds(off, TOKEN_TILE)]
        vals, idxs, sels = [], [], []
        for _ in range(TOP_K):
            m = jnp.max(l, axis=0, keepdims=True)
            idx = jnp.min(jnp.where(l == m, expert, float(N_EXPERTS)), axis=0, keepdims=True)
            sel = expert == idx
            vals.append(m)
            idxs.append(idx)
            sels.append(sel)
            l = jnp.where(sel, -jnp.inf, l)
        chosen = jnp.where(sels[0] | sels[1] | sels[2] | sels[3], 1.0, 0.0)
        before = jnp.dot(chosen.astype(BF16), earlier, preferred_element_type=F32) + count
        ranks = [jnp.sum(jnp.where(s, before, 0.0), axis=0, keepdims=True) for s in sels]
        es = [jnp.exp(v - vals[0]) for v in vals]
        den = es[0] + es[1] + es[2] + es[3]
        eidx_ref[:, pl.ds(off, TOKEN_TILE)] = jnp.concatenate(idxs, axis=0).astype(jnp.int32)
        rank_ref[:, pl.ds(off, TOKEN_TILE)] = jnp.concatenate(ranks, axis=0).astype(jnp.int32)
        gate_ref[:, pl.ds(off, TOKEN_TILE)] = jnp.concatenate([e / den for e in es], axis=0)
        return count + jnp.sum(chosen, axis=1, keepdims=True)

    count = lax.fori_loop(0, n_tok // TOKEN_TILE, body, jnp.zeros((N_EXPERTS, 1), F32))
    cnt_ref[...] = jnp.broadcast_to(count, (N_EXPERTS, LANES)).astype(jnp.int32)


def _route(logits_t):
    n_tok = logits_t.shape[1]
    return pl.pallas_call(
        _route_kernel,
        out_shape=[jax.ShapeDtypeStruct((TOP_K, n_tok), jnp.int32), jax.ShapeDtypeStruct((TOP_K, n_tok), F32),
                   jax.ShapeDtypeStruct((TOP_K, n_tok), jnp.int32), jax.ShapeDtypeStruct((N_EXPERTS, LANES), jnp.int32)],
        compiler_params=pltpu.CompilerParams(vmem_limit_bytes=VMEM_LIMIT_BYTES),
        name="route",
    )(logits_t)


def _dest_kernel(start_ref, eidx_ref, rank_ref, dest_ref):
    e = eidx_ref[...]
    base = jnp.zeros(e.shape, jnp.int32)
    for x in range(N_EXPERTS):
        base = jnp.where(e == x, start_ref[x], base)
    dest_ref[...] = base + rank_ref[...]


def _dest(group_start, eidx, rank):
    return pl.pallas_call(
        _dest_kernel,
        in_specs=[pl.BlockSpec(memory_space=pltpu.SMEM), pl.BlockSpec(memory_space=pltpu.VMEM),
                  pl.BlockSpec(memory_space=pltpu.VMEM)],
        out_specs=pl.BlockSpec(memory_space=pltpu.VMEM),
        out_shape=jax.ShapeDtypeStruct(eidx.shape, jnp.int32),
        name="dest",
    )(group_start, eidx, rank)


def _dispatch_kernel(n_prompt_tiles, n_tiles, gend_ref, gsize_ref, dest_ref, up_ref, us_ref, xs_ref,
                     zero_buf, tiles, zsem, load_sem, scat_sem):
    i = pl.program_id(0)
    slot = lax.rem(i, DISPATCH_SLOTS)

    def start_load(tile, s):
        @pl.when(tile < n_prompt_tiles)
        def _():
            r = pl.multiple_of(tile * TOKEN_TILE, TOKEN_TILE)
            pltpu.make_async_copy(up_ref.at[pl.ds(r, TOKEN_TILE), :], tiles.at[s], load_sem.at[s]).start()

        @pl.when(tile >= n_prompt_tiles)
        def _():
            r = pl.multiple_of((tile - n_prompt_tiles) * TOKEN_TILE, TOKEN_TILE)
            pltpu.make_async_copy(us_ref.at[pl.ds(r, TOKEN_TILE), :], tiles.at[s], load_sem.at[s]).start()

    def wait_load(s):
        pltpu.make_async_copy(up_ref.at[pl.ds(0, TOKEN_TILE), :], tiles.at[s], load_sem.at[s]).wait()

    def wait_scatter(s):
        for _ in range(TOP_K):
            pltpu.make_async_copy(tiles.at[s], xs_ref.at[pl.ds(0, TOKEN_TILE), :], scat_sem.at[s]).wait()

    @pl.when(i == 0)
    def _():
        start_load(0, 0)
        zero_buf[...] = jnp.zeros_like(zero_buf)
        for e in range(N_EXPERTS):
            @pl.when(gsize_ref[e] > 0)
            def _():
                r0 = pl.multiple_of(gend_ref[e] - ROW_BLOCK, ROW_BLOCK)
                pltpu.make_async_copy(zero_buf, xs_ref.at[pl.ds(r0, ROW_BLOCK), :], zsem).start()
        for e in range(N_EXPERTS):
            @pl.when(gsize_ref[e] > 0)
            def _():
                pltpu.make_async_copy(zero_buf, xs_ref.at[pl.ds(0, ROW_BLOCK), :], zsem).wait()

    @pl.when(i + 1 < n_tiles)
    def _():
        nxt = lax.rem(i + 1, DISPATCH_SLOTS)

        @pl.when(i + 1 >= DISPATCH_SLOTS)
        def _():
            wait_scatter(nxt)
        start_load(i + 1, nxt)

    wait_load(slot)
    for s in range(DISPATCH_SLOTS):
        @pl.when(slot == s)
        def _():
            for j in range(TOP_K * TOKEN_TILE):
                d = dest_ref[0, 0, j]
                pltpu.make_async_copy(tiles.at[s, pl.ds(j % TOKEN_TILE, 1), :], xs_ref.at[pl.ds(d, 1), :],
                                      scat_sem.at[s]).start()

    @pl.when(i == n_tiles - 1)
    def _():
        for tile in range(max(n_tiles - DISPATCH_SLOTS, 0), n_tiles):
            wait_scatter(tile % DISPATCH_SLOTS)


def _dispatch(group_end, group_size, dest_tiles, u2_p, u2_s, n_rows):
    n_p = u2_p.shape[0] // TOKEN_TILE
    n_s = u2_s.shape[0] // TOKEN_TILE
    grid_spec = pltpu.PrefetchScalarGridSpec(
        num_scalar_prefetch=2,
        grid=(n_p + n_s,),
        in_specs=[
            pl.BlockSpec((1, 1, TOP_K * TOKEN_TILE), lambda i, *_: (i, 0, 0), memory_space=pltpu.SMEM),
            pl.BlockSpec(memory_space=pl.ANY),
            pl.BlockSpec(memory_space=pl.ANY),
        ],
        out_specs=pl.BlockSpec(memory_space=pl.ANY),
        scratch_shapes=[pltpu.VMEM((ROW_BLOCK, D_MODEL), F32), pltpu.VMEM((DISPATCH_SLOTS, TOKEN_TILE, D_MODEL), F32),
                        pltpu.SemaphoreType.DMA(()), pltpu.SemaphoreType.DMA((DISPATCH_SLOTS,)),
                        pltpu.SemaphoreType.DMA((DISPATCH_SLOTS,))],
    )
    return pl.pallas_call(
        functools.partial(_dispatch_kernel, n_p, n_p + n_s),
        grid_spec=grid_spec,
        out_shape=jax.ShapeDtypeStruct((n_rows, D_MODEL), F32),
        compiler_params=_params(1),
        name="dispatch",
    )(group_end, group_size, dest_tiles, u2_p, u2_s)


_FIRST, _MIDDLE, _LAST = 0, 1, 2


def _experts_kernel(row0_ref, nblk_ref, exp_ref, xs_ref, wg_ref, wl_ref, wd_ref, bg_ref, bl_ref, bd_ref,
                    ys_ref, xbuf, acc, in_sem, out_sem):
    i = pl.program_id(0)
    f = pl.program_id(1)
    last_f = pl.num_programs(1) - 1
    nblk = nblk_ref[i]
    row0 = row0_ref[i]

    def local(j):
        return pl.ds(pl.multiple_of(j * ROW_BLOCK, ROW_BLOCK), ROW_BLOCK)

    def in_hbm(j):
        return pl.ds(pl.multiple_of(row0 + j * ROW_BLOCK, ROW_BLOCK), ROW_BLOCK)

    def x_copy(j):
        return pltpu.make_async_copy(xs_ref.at[in_hbm(j), :], xbuf.at[local(j), :], in_sem.at[j])

    def y_copy(j):
        return pltpu.make_async_copy(acc.at[local(j), :], ys_ref.at[in_hbm(j), :], out_sem.at[j])

    def body(chunks, phase):
        wg = wg_ref[...].astype(BF16)
        wl = wl_ref[...].astype(BF16)
        wd = wd_ref[...].astype(BF16)
        for j0, m in chunks:
            rows = pl.ds(pl.multiple_of(j0 * ROW_BLOCK, ROW_BLOCK), m * ROW_BLOCK)
            if phase == _FIRST:
                for j in range(m):
                    x_copy(j0 + j).wait()
            x = xbuf[rows, :].astype(BF16)
            glu = jnp.dot(x, wg, preferred_element_type=F32) + bg_ref[...]
            lin = jnp.dot(x, wl, preferred_element_type=F32) + bl_ref[...]
            glu = jnp.minimum(glu, SWIGLU_LIMIT)
            lin = jnp.clip(lin, -SWIGLU_LIMIT, SWIGLU_LIMIT)
            act = glu * jax.nn.sigmoid(SWIGLU_ALPHA * glu) * (lin + 1.0)
            y = jnp.dot(act.astype(BF16), wd, preferred_element_type=F32)
            if phase == _FIRST:
                acc[rows, :] = y + bd_ref[...]
            else:
                acc[rows, :] += y
            if phase == _LAST:
                for j in range(m):
                    y_copy(j0 + j).start()

    def run(phase):
        def quad(q, carry):
            body(((4 * q, 2), (4 * q + 2, 2)), phase)
            return carry

        n_quads = lax.shift_right_logical(nblk, 2)
        lax.fori_loop(0, n_quads, quad, 0)
        done = lax.shift_left(n_quads, 2)

        @pl.when(jnp.bitwise_and(nblk, 2) == 2)
        def _():
            body(((done, 1), (done + 1, 1)), phase)

        @pl.when(jnp.bitwise_and(nblk, 1) == 1)
        def _():
            body(((nblk - 1, 1),), phase)

    @pl.when(f == 0)
    def _():
        def start(j, carry):
            x_copy(j).start()
            return carry
        lax.fori_loop(0, nblk, start, 0)
        run(_FIRST)

    @pl.when((f > 0) & (f < last_f))
    def _():
        run(_MIDDLE)

    @pl.when(f == last_f)
    def _():
        run(_LAST)

        def drain(j, carry):
            y_copy(j).wait()
            return carry
        lax.fori_loop(0, nblk, drain, 0)


def _experts(items, n_live, xs, w_up, b_up, w_down, b_down):
    row0, nblk, exp = items
    n_rows = xs.shape[0]
    n_f = D_FF // FF_TILE
    assert n_f >= 2

    max_rows = EXPERT_MAX_BLOCKS * ROW_BLOCK
    grid_spec = pltpu.PrefetchScalarGridSpec(
        num_scalar_prefetch=3,
        grid=(n_live, n_f),
        in_specs=[
            pl.BlockSpec(memory_space=pl.ANY),
            pl.BlockSpec((None, D_MODEL, FF_TILE), lambda i, f, r0, nb, ex: (ex[i], 0, f)),
            pl.BlockSpec((None, D_MODEL, FF_TILE), lambda i, f, r0, nb, ex: (ex[i], 0, n_f + f)),
            pl.BlockSpec((None, FF_TILE, D_MODEL), lambda i, f, r0, nb, ex: (ex[i], f, 0)),
            pl.BlockSpec((None, 1, FF_TILE), lambda i, f, r0, nb, ex: (ex[i], 0, f)),
            pl.BlockSpec((None, 1, FF_TILE), lambda i, f, r0, nb, ex: (ex[i], 0, n_f + f)),
            pl.BlockSpec((None, 1, D_MODEL), lambda i, f, r0, nb, ex: (ex[i], 0, 0)),
        ],
        out_specs=pl.BlockSpec(memory_space=pl.ANY),
        scratch_shapes=[pltpu.VMEM((max_rows, D_MODEL), F32), pltpu.VMEM((max_rows, D_MODEL), F32),
                        pltpu.SemaphoreType.DMA((EXPERT_MAX_BLOCKS,)), pltpu.SemaphoreType.DMA((EXPERT_MAX_BLOCKS,))],
    )
    return pl.pallas_call(
        _experts_kernel,
        grid_spec=grid_spec,
        out_shape=jax.ShapeDtypeStruct((n_rows, D_MODEL), F32),
        compiler_params=_params(2),
        name="experts",
    )(row0, nblk, exp, xs, w_up, w_up, w_down,
      b_up.reshape(N_EXPERTS, 1, 2 * D_FF), b_up.reshape(N_EXPERTS, 1, 2 * D_FF), b_down.reshape(N_EXPERTS, 1, D_MODEL))


def _combine_kernel(n_prompt_tiles, dcur_ref, dnxt_ref, gates_ref, x1p_ref, x1s_ref, g2p_ref, g2s_ref, lng_ref, lnb_ref,
                    ys_ref, yp_ref, ysm_ref, buf, sem):
    i = pl.program_id(0)
    n = pl.num_programs(0)
    slot = i % 2

    def gather(idx_ref, s):
        for j in range(TOP_K * TOKEN_TILE):
            d = idx_ref[0, 0, j]
            pltpu.make_async_copy(ys_ref.at[pl.ds(d, 1), :],
                                  buf.at[s, j // TOKEN_TILE, pl.ds(j % TOKEN_TILE, 1), :], sem.at[s]).start()

    @pl.when(i == 0)
    def _():
        gather(dcur_ref, 0)

    for s in range(2):
        @pl.when((i + 1 < n) & (slot == 1 - s))
        def _():
            gather(dnxt_ref, s)

    for k in range(TOP_K):
        pltpu.make_async_copy(ys_ref.at[pl.ds(0, TOKEN_TILE), :], buf.at[slot, k], sem.at[slot]).wait()

    gates = gates_ref[...]
    ffn = buf[slot, 0] * gates[:, 0:1]
    for k in range(1, TOP_K):
        ffn = ffn + buf[slot, k] * gates[:, k:k + 1]

    def finish(x1, gate2, out_ref):
        out_ref[...] = _standardise(DEEPNORM_ALPHA * x1 + gate2 * ffn) * lng_ref[...] + lnb_ref[...]

    @pl.when(i < n_prompt_tiles)
    def _():
        finish(x1p_ref[...], g2p_ref[...], yp_ref)

    @pl.when(i >= n_prompt_tiles)
    def _():
        finish(x1s_ref[...], g2s_ref[...], ysm_ref)


def _combine(dest_tiles, gates_tok, x1_p, x1_s, gate2_p, gate2_s, ln2_g, ln2_b, ys):
    n_p = x1_p.shape[0] // TOKEN_TILE
    n_s = x1_s.shape[0] // TOKEN_TILE
    n = n_p + n_s

    def p_idx(i):
        return jnp.minimum(i, n_p - 1)

    def s_idx(i):
        return jnp.maximum(i - n_p, 0)

    smem_tile = (1, 1, TOP_K * TOKEN_TILE)
    return pl.pallas_call(
        functools.partial(_combine_kernel, n_p),
        grid=(n,),
        in_specs=[
            pl.BlockSpec(smem_tile, lambda i: (i, 0, 0), memory_space=pltpu.SMEM),
            pl.BlockSpec(smem_tile, lambda i: (jnp.minimum(i + 1, n - 1), 0, 0), memory_space=pltpu.SMEM),
            pl.BlockSpec((TOKEN_TILE, TOP_K), lambda i: (i, 0)),
            pl.BlockSpec((TOKEN_TILE, D_MODEL), lambda i: (p_idx(i), 0)),
            pl.BlockSpec((TOKEN_TILE, D_MODEL), lambda i: (s_idx(i), 0)),
            pl.BlockSpec((1, D_MODEL), lambda i: (0, 0)),
            pl.BlockSpec((TOKEN_TILE, D_MODEL), lambda i: (s_idx(i), 0)),
            pl.BlockSpec((1, D_MODEL), lambda i: (0, 0)),
            pl.BlockSpec((1, D_MODEL), lambda i: (0, 0)),
            pl.BlockSpec(memory_space=pl.ANY),
        ],
        out_specs=[pl.BlockSpec((TOKEN_TILE, D_MODEL), lambda i: (p_idx(i), 0)),
                   pl.BlockSpec((TOKEN_TILE, D_MODEL), lambda i: (s_idx(i), 0))],
        out_shape=[jax.ShapeDtypeStruct(x1_p.shape, F32), jax.ShapeDtypeStruct(x1_s.shape, F32)],
        scratch_shapes=[pltpu.VMEM((2, TOP_K, TOKEN_TILE, D_MODEL), F32), pltpu.SemaphoreType.DMA((2,))],
        compiler_params=_params(1),
        name="combine",
    )(dest_tiles, dest_tiles, gates_tok, x1_p, x1_s, gate2_p, gate2_s,
      ln2_g.reshape(1, D_MODEL), ln2_b.reshape(1, D_MODEL), ys)


def _rope_tables(pos):
    half = HEAD_DIM // 2
    inv_freq = ROPE_THETA ** (-jnp.arange(half, dtype=F32) / half)
    ang = pos.astype(F32)[:, None] * inv_freq
    cos, sin = jnp.cos(ang), jnp.sin(ang)
    reps = LANES // HEAD_DIM
    return (jnp.tile(jnp.concatenate([cos, cos], axis=1), (1, reps)),
            jnp.tile(jnp.concatenate([-sin, sin], axis=1), (1, reps)))


def _work_items(group_start, group_size, n_items):
    chunk_rows = EXPERT_MAX_BLOCKS * ROW_BLOCK
    ex = jnp.arange(N_EXPERTS, dtype=jnp.int32)
    n_chunks = (group_size + chunk_rows - 1) // chunk_rows
    chunk_end = jnp.sum(jnp.where(ex[None, :] <= ex[:, None], n_chunks[None, :], 0), axis=1)
    chunk_start = chunk_end - n_chunks
    item = jnp.arange(n_items, dtype=jnp.int32)
    live = item < chunk_end[-1]
    it = jnp.minimum(item, chunk_end[-1] - 1)
    exp = jnp.minimum(jnp.sum((chunk_end[None, :] <= it[:, None]).astype(jnp.int32), axis=1), N_EXPERTS - 1)
    own = exp[:, None] == ex[None, :]

    def pick(per_expert):
        return jnp.sum(jnp.where(own, per_expert[None, :], 0), axis=1)

    chunk = it - pick(chunk_start)
    row0 = pick(group_start) + chunk * chunk_rows
    nblk = jnp.clip((pick(group_size) - chunk * chunk_rows) // ROW_BLOCK, 0, EXPERT_MAX_BLOCKS)
    nblk = jnp.where(live, nblk, 0)
    return (row0.astype(jnp.int32), nblk.astype(jnp.int32), exp.astype(jnp.int32)), chunk_end[-1].astype(jnp.int32)


def _moe(u2_p, u2_s, logits_t, x1_p, x1_s, gate2_p, gate2_s, ln2_g, ln2_b, w_up, b_up, w_down, b_down):
    n_tok = logits_t.shape[1]
    eidx, gates, rank, counts = _route(logits_t)
    count = counts[:, 0]
    group_size = (count + ROW_BLOCK - 1) // ROW_BLOCK * ROW_BLOCK
    ex = jnp.arange(N_EXPERTS, dtype=jnp.int32)
    group_end = jnp.sum(jnp.where(ex[None, :] <= ex[:, None], group_size[None, :], 0), axis=1)
    group_start = group_end - group_size
    dest = _dest(group_start, eidx, rank)
    n_tiles = n_tok // TOKEN_TILE
    dest_tiles = dest.reshape(TOP_K, n_tiles, TOKEN_TILE).transpose(1, 0, 2).reshape(n_tiles, 1, TOP_K * TOKEN_TILE)
    max_rows = n_tok * TOP_K + N_EXPERTS * (ROW_BLOCK - 1)
    n_rows = (max_rows + ROW_BLOCK - 1) // ROW_BLOCK * ROW_BLOCK
    n_items = N_EXPERTS + max_rows // (EXPERT_MAX_BLOCKS * ROW_BLOCK)
    xs = _dispatch(group_end, group_size, dest_tiles, u2_p, u2_s, n_rows)
    items, n_live = _work_items(group_start, group_size, n_items)
    ys = _experts(items, n_live, xs, w_up, b_up, w_down, b_down)
    return _combine(dest_tiles, gates.T, x1_p, x1_s, gate2_p, gate2_s, ln2_g, ln2_b, ys)


def _layer(x_p, x_s, cache_k, cache_v, state_conv, c_p, c_s,
           w_ada, b_ada, w_in, b_in, conv_w, sinks, norm_attn_g, norm_conv_g, w_out, b_out,
           ln1_g, ln1_b, w_router, b_router, w_up, b_up, w_down, b_down, ln2_g, ln2_b, past_len):
    T = x_p.shape[0]
    B = x_s.shape[0]
    n_c = 1 + B
    pad_c = (-n_c) % SUBLANES
    c_all = jnp.concatenate([c_p, c_s, jnp.zeros((pad_c, D_MODEL), F32)], axis=0)
    mod = _ada(c_all, w_ada, b_ada)
    shift1, scale1, gate1, shift2, scale2, gate2 = [mod[:, j * D_MODEL:(j + 1) * D_MODEL] for j in range(6)]

    def prompt(a):
        return a[0:1]

    def sample(a):
        return a[1:n_c]

    w_in_bf = w_in.astype(BF16)
    w_out_bf = w_out.astype(BF16)
    w_router_t_bf = w_router.T.astype(BF16)

    cos_p, sin_p = _rope_tables(jnp.arange(T, dtype=jnp.int32))
    q_p, k_p, v_p, gb_p, uc_p = _inproj(x_p, prompt(shift1), prompt(scale1), cos_p, sin_p, w_in_bf, b_in,
                                        min(INPROJ_TILE_M, T))
    attn_p = _attn_prompt(q_p, k_p, v_p, sinks, norm_attn_g)
    x1_p, u2_p, lg_p = _mix(attn_p, gb_p, uc_p, jnp.zeros((CONV_WIDTH - 1, CONV_DIM), F32), conv_w, norm_conv_g,
                            w_out_bf, b_out, x_p, prompt(gate1), ln1_g, ln1_b, prompt(shift2), prompt(scale2),
                            w_router_t_bf, b_router, min(MIX_TILE_M, T), True)

    cos_s, sin_s = _rope_tables(jnp.full((1,), past_len, jnp.int32))
    q_s, k_s, v_s, gb_s, uc_s = _inproj(x_s, sample(shift1), sample(scale1), cos_s, sin_s, w_in_bf, b_in, B)
    attn_s, new_k_s, new_v_s = _attn_decode(q_s, k_s, v_s, cache_k, cache_v, sinks, norm_attn_g)
    x1_s, u2_s, lg_s = _mix(attn_s.reshape(B, ATTN_DIM), gb_s, uc_s, (state_conv[:, 1], state_conv[:, 0]), conv_w,
                            norm_conv_g, w_out_bf, b_out, x_s, sample(gate1), ln1_g, ln1_b, sample(shift2),
                            sample(scale2), w_router_t_bf, b_router, B, False)

    y_p, y_s = _moe(u2_p, u2_s, jnp.concatenate([lg_p, lg_s], axis=1), x1_p, x1_s, prompt(gate2), sample(gate2),
                    ln2_g, ln2_b, w_up, b_up, w_down, b_down)

    new_k_p = k_p[T - WINDOW:].reshape(WINDOW, N_KV_HEADS, HEAD_DIM)
    new_v_p = v_p[T - WINDOW:].reshape(WINDOW, N_KV_HEADS, HEAD_DIM)
    new_conv_p = uc_p[T - (CONV_WIDTH - 1):]
    new_conv_s = jnp.stack([state_conv[:, 1], uc_s], axis=1)
    return (y_p, y_s, new_k_p, new_v_p, new_conv_p,
            new_k_s.reshape(B, WINDOW, N_KV_HEADS, HEAD_DIM), new_v_s.reshape(B, WINDOW, N_KV_HEADS, HEAD_DIM), new_conv_s)


def kernel(x_prompt, x_sample, cache_k, cache_v, state_conv, c_prompt, c_sample, w_ada, b_ada, w_in, b_in, conv_w, sinks, norm_attn_g, norm_conv_g, w_out, b_out, ln1_g, ln1_b, w_router, b_router, w_up, b_up, w_down, b_down, ln2_g, ln2_b):
    assert x_prompt.shape[0] == 1 and x_sample.shape[1] == 1 and w_ada.shape[0] == DEPTH == 1
    B = x_sample.shape[0]
    (y_p, y_s, nk_p, nv_p, nc_p, nk_s, nv_s, nc_s) = _layer(
        x_prompt[0], x_sample[:, 0], cache_k[0], cache_v[0], state_conv[0], c_prompt, c_sample,
        w_ada[0], b_ada[0], w_in[0], b_in[0], conv_w[0], sinks[0], norm_attn_g[0], norm_conv_g[0], w_out[0], b_out[0],
        ln1_g[0], ln1_b[0], w_router[0], b_router[0], w_up[0], b_up[0], w_down[0], b_down[0], ln2_g[0], ln2_b[0],
        PAST_LEN)
    return (y_p[None], y_s.reshape(B, 1, D_MODEL), nk_p[None, None], nv_p[None, None], nc_p[None, None],
            nk_s[None], nv_s[None], nc_s[None])
```

```python
import functools

import jax
import jax.numpy as jnp
from jax import lax
from jax.experimental import pallas as pl
from jax.experimental.pallas import tpu as pltpu

F32 = jnp.float32
BF16 = jnp.bfloat16

D_MODEL = 2048
HEAD_DIM = 64
N_HEADS = 16
N_KV_HEADS = 4
GQA_GROUP = N_HEADS // N_KV_HEADS
ATTN_DIM = N_HEADS * HEAD_DIM
KV_DIM = N_KV_HEADS * HEAD_DIM
CONV_DIM = D_MODEL - ATTN_DIM
CONV_WIDTH = 3
IN_DIM = ATTN_DIM + 2 * KV_DIM + 3 * CONV_DIM
WINDOW = 128
PAST_LEN = 16384
ROPE_THETA = 10000.0
N_EXPERTS = 32
TOP_K = 4
D_FF = D_MODEL
SWIGLU_LIMIT = 7.0
SWIGLU_ALPHA = 1.702
DEPTH = 1
DEEPNORM_ALPHA = (2.0 * DEPTH) ** 0.25
LN_EPS = 1e-5
RMS_EPS = 1e-6
COL_K = ATTN_DIM
COL_V = COL_K + KV_DIM
COL_B = COL_V + KV_DIM
COL_C = COL_B + CONV_DIM
COL_X = COL_C + CONV_DIM

LANES = 128
SUBLANES = 8
VMEM_LIMIT_BYTES = 60 * 1024 * 1024

TOKEN_TILE = 128
DISPATCH_SLOTS = 3
ADA_TILE_N = 1024
INPROJ_TILE_M = 512
INPROJ_CHUNK_N = 512
ATTN_BLOCKS = 4
MIX_TILE_M = 512
MIX_CHAIN_M = 256
DEC_TILE_B = 16
ROW_BLOCK = 256
EXPERT_MAX_BLOCKS = 9
FF_TILE = 256
NEG_BIG = -1e30


def _params(n_axes, vmem=VMEM_LIMIT_BYTES):
    return pltpu.CompilerParams(dimension_semantics=("arbitrary",) * n_axes, vmem_limit_bytes=vmem)


def _standardise(x):
    mu = jnp.mean(x, axis=-1, keepdims=True)
    xc = x - mu
    var = jnp.mean(xc * xc, axis=-1, keepdims=True)
    return xc * lax.rsqrt(var + LN_EPS)


def _rms(x, g):
    return x * lax.rsqrt(jnp.mean(x * x, axis=-1, keepdims=True) + RMS_EPS) * g


def _ada_kernel(c_ref, w_ref, b_ref, o_ref):
    c = c_ref[...]
    s = (c * jax.nn.sigmoid(c)).astype(BF16)
    o_ref[...] = jnp.dot(s, w_ref[...].astype(BF16), preferred_element_type=F32) + b_ref[...]


def _ada(c_all, w_ada, b_ada):
    rows = c_all.shape[0]
    n_out = w_ada.shape[1]
    return pl.pallas_call(
        _ada_kernel,
        grid=(n_out // ADA_TILE_N,),
        in_specs=[
            pl.BlockSpec((rows, D_MODEL), lambda j: (0, 0)),
            pl.BlockSpec((D_MODEL, ADA_TILE_N), lambda j: (0, j)),
            pl.BlockSpec((1, ADA_TILE_N), lambda j: (0, j)),
        ],
        out_specs=pl.BlockSpec((rows, ADA_TILE_N), lambda j: (0, j)),
        out_shape=jax.ShapeDtypeStruct((rows, n_out), F32),
        compiler_params=_params(1),
        name="ada",
    )(c_all, w_ada, b_ada.reshape(1, n_out))


def _inproj_kernel(x_ref, shift_ref, scale_ref, cos_ref, sin_ref, w_ref, b_ref,
                   q_ref, k_ref, v_ref, gb_ref, uc_ref):
    u = (_standardise(x_ref[...]) * (1.0 + scale_ref[...]) + shift_ref[...]).astype(BF16)
    cos = cos_ref[...]
    sin = sin_ref[...]
    lane = lax.broadcasted_iota(jnp.int32, (1, LANES), 1)
    first_half = (lane % HEAD_DIM) < (HEAD_DIM // 2)

    def rope(z):
        partner = jnp.where(first_half, pltpu.roll(z, LANES - HEAD_DIM // 2, axis=1),
                            pltpu.roll(z, HEAD_DIM // 2, axis=1))
        return z * cos + partner * sin

    def proj(c0):
        w = w_ref[:, c0:c0 + INPROJ_CHUNK_N]
        return jnp.dot(u, w, preferred_element_type=F32) + b_ref[:, c0:c0 + INPROJ_CHUNK_N]

    groups = INPROJ_CHUNK_N // LANES
    for j in range(ATTN_DIM // INPROJ_CHUNK_N):
        z = proj(j * INPROJ_CHUNK_N)
        for g in range(groups):
            c0 = j * INPROJ_CHUNK_N + g * LANES
            q_ref[:, c0:c0 + LANES] = (rope(z[:, g * LANES:(g + 1) * LANES]) * (HEAD_DIM ** -0.5)).astype(BF16)
    z = proj(COL_K)
    for g in range(KV_DIM // LANES):
        k_ref[:, g * LANES:(g + 1) * LANES] = rope(z[:, g * LANES:(g + 1) * LANES])
    v_ref[...] = z[:, KV_DIM:2 * KV_DIM]
    for j in range(CONV_DIM // INPROJ_CHUNK_N):
        sl = slice(j * INPROJ_CHUNK_N, (j + 1) * INPROJ_CHUNK_N)
        gb_ref[:, sl] = proj(COL_B + j * INPROJ_CHUNK_N)
        uc_ref[:, sl] = proj(COL_C + j * INPROJ_CHUNK_N) * proj(COL_X + j * INPROJ_CHUNK_N)


def _inproj(x, shift, scale, cos, sin, w_in_bf, b_in, tm):
    T = x.shape[0]
    per_row_mod = shift.shape[0] != 1
    per_row_pos = cos.shape[0] != 1
    mod_spec = pl.BlockSpec((tm, D_MODEL), lambda i: (i, 0)) if per_row_mod else pl.BlockSpec((1, D_MODEL), lambda i: (0, 0))
    pos_spec = pl.BlockSpec((tm, LANES), lambda i: (i, 0)) if per_row_pos else pl.BlockSpec((1, LANES), lambda i: (0, 0))

    def row_spec(width):
        return pl.BlockSpec((tm, width), lambda i: (i, 0))

    return pl.pallas_call(
        _inproj_kernel,
        grid=(T // tm,),
        in_specs=[
            row_spec(D_MODEL), mod_spec, mod_spec, pos_spec, pos_spec,
            pl.BlockSpec((D_MODEL, IN_DIM), lambda i: (0, 0), pipeline_mode=pl.Buffered(1)),
            pl.BlockSpec((1, IN_DIM), lambda i: (0, 0)),
        ],
        out_specs=[row_spec(ATTN_DIM), row_spec(KV_DIM), row_spec(KV_DIM), row_spec(CONV_DIM), row_spec(CONV_DIM)],
        out_shape=[
            jax.ShapeDtypeStruct((T, ATTN_DIM), BF16),
            jax.ShapeDtypeStruct((T, KV_DIM), F32),
            jax.ShapeDtypeStruct((T, KV_DIM), F32),
            jax.ShapeDtypeStruct((T, CONV_DIM), F32),
            jax.ShapeDtypeStruct((T, CONV_DIM), F32),
        ],
        compiler_params=_params(1),
        name="inproj",
    )(x, shift, scale, cos, sin, w_in_bf, b_in.reshape(1, IN_DIM))


def _attn_kernel(q_ref, kp_ref, kc_ref, vp_ref, vc_ref, bias_ref, sink_ref, g_ref, o_ref):
    n = pl.program_id(0)
    n_blocks = q_ref.shape[0] // WINDOW
    k = jnp.concatenate([kp_ref[...], kc_ref[...]], axis=0).astype(BF16)
    v = jnp.concatenate([vp_ref[...], vc_ref[...]], axis=0).astype(BF16)
    for b in range(n_blocks):
        q = q_ref[b * WINDOW:(b + 1) * WINDOW, :]
        kb = k[b * WINDOW:(b + 2) * WINDOW, :]
        vb = v[b * WINDOW:(b + 2) * WINDOW, :]
        bias = bias_ref[jnp.minimum(n, 1)] if b == 0 else bias_ref[1]
        outs = []
        for g in range(N_KV_HEADS):
            heads = range(g * GQA_GROUP, (g + 1) * GQA_GROUP)
            qg = jnp.concatenate([q[:, h * HEAD_DIM:(h + 1) * HEAD_DIM] for h in heads], axis=0)
            kg = kb[:, g * HEAD_DIM:(g + 1) * HEAD_DIM]
            vg = vb[:, g * HEAD_DIM:(g + 1) * HEAD_DIM]
            sink = sink_ref[g]
            s = lax.dot_general(kg, qg, (((1,), (1,)), ((), ())), preferred_element_type=F32) + bias
            m = jnp.maximum(jnp.max(s, axis=0, keepdims=True), sink)
            e = jnp.exp(s - m)
            den = jnp.sum(e, axis=0, keepdims=True) + jnp.exp(sink - m)
            o = lax.dot_general(vg, e.astype(BF16), (((0,), (0,)), ((), ())), preferred_element_type=F32)
            o = o / den
            outs.extend(o[:, j * WINDOW:(j + 1) * WINDOW] for j in range(GQA_GROUP))
        o = jnp.concatenate(outs, axis=0).T
        o_ref[b * WINDOW:(b + 1) * WINDOW, :] = _rms(o, g_ref[...]).astype(BF16)


def _attn_mask_bias():
    qi = jnp.arange(GQA_GROUP * WINDOW, dtype=jnp.int32)[None, :] % WINDOW
    ks = jnp.arange(2 * WINDOW, dtype=jnp.int32)[:, None]
    later = (ks >= qi) & (ks <= qi + WINDOW)
    first = later & (ks >= WINDOW)
    return jnp.where(jnp.stack([first, later]), 0.0, NEG_BIG).astype(F32)


def _attn_prompt(q, k, v, sinks, norm_g):
    T = q.shape[0]
    nb = ATTN_BLOCKS if T % (ATTN_BLOCKS * WINDOW) == 0 else 1

    def cur(width):
        return pl.BlockSpec((nb * WINDOW, width), lambda n: (n, 0))

    def prev(width):
        return pl.BlockSpec((WINDOW, width), lambda n: (jnp.maximum(n * nb - 1, 0), 0))

    return pl.pallas_call(
        _attn_kernel,
        grid=(T // (nb * WINDOW),),
        in_specs=[
            cur(ATTN_DIM), prev(KV_DIM), cur(KV_DIM), prev(KV_DIM), cur(KV_DIM),
            pl.BlockSpec((2, 2 * WINDOW, GQA_GROUP * WINDOW), lambda n: (0, 0, 0)),
            pl.BlockSpec((N_KV_HEADS, 1, GQA_GROUP * WINDOW), lambda n: (0, 0, 0)),
            pl.BlockSpec((1, ATTN_DIM), lambda n: (0, 0)),
        ],
        out_specs=cur(ATTN_DIM),
        out_shape=jax.ShapeDtypeStruct((T, ATTN_DIM), BF16),
        compiler_params=_params(1),
        name="attn",
    )(q, k, k, v, v, _attn_mask_bias(),
      jnp.repeat(sinks, WINDOW).reshape(N_KV_HEADS, 1, GQA_GROUP * WINDOW),
      norm_g.reshape(1, ATTN_DIM))


def _attn_dec_kernel(q_ref, kn_ref, vn_ref, ck_ref, cv_ref, sink_ref, g_ref, o_ref, nk_ref, nv_ref):
    tb = q_ref.shape[0]
    q = q_ref[...].astype(F32)
    lane_group = lax.broadcasted_iota(jnp.int32, (1, N_HEADS, KV_DIM), 2) // HEAD_DIM
    head_group = lax.broadcasted_iota(jnp.int32, (1, N_HEADS, KV_DIM), 1) // GQA_GROUP
    own = lane_group == head_group
    qe = jnp.where(own, jnp.concatenate([q] * N_KV_HEADS, axis=2), 0.0)
    ck = ck_ref[...]
    cv = cv_ref[...]
    kn = kn_ref[...]
    vn = vn_ref[...]
    s = jnp.einsum("bhc,bwc->bhw", qe.astype(BF16), ck.astype(BF16), preferred_element_type=F32)
    s_new = jnp.sum(qe.astype(BF16).astype(F32) * kn.astype(BF16).astype(F32), axis=2, keepdims=True)
    sink = sink_ref[...]
    m = jnp.maximum(jnp.maximum(jnp.max(s, axis=2, keepdims=True), s_new), sink)
    e = jnp.exp(s - m)
    e_new = jnp.exp(s_new - m)
    den = jnp.sum(e, axis=2, keepdims=True) + e_new + jnp.exp(sink - m)
    p = (e / den).astype(BF16)
    p_new = (e_new / den).astype(BF16).astype(F32)
    o = jnp.einsum("bhw,bwc->bhc", p, cv.astype(BF16), preferred_element_type=F32)
    o = o + p_new * vn.astype(BF16).astype(F32)
    o = jnp.where(own, o, 0.0)
    oh = o[:, :, 0:HEAD_DIM]
    for g in range(1, N_KV_HEADS):
        oh = oh + o[:, :, g * HEAD_DIM:(g + 1) * HEAD_DIM]
    ms = jnp.sum(jnp.sum(oh * oh, axis=2, keepdims=True), axis=1, keepdims=True) / ATTN_DIM
    o_ref[...] = (oh * lax.rsqrt(ms + RMS_EPS) * g_ref[...]).astype(BF16)
    row = lax.broadcasted_iota(jnp.int32, (1, WINDOW, 1), 1)
    nk_ref[...] = jnp.where(row == WINDOW - 1, kn, pltpu.roll(ck, WINDOW - 1, axis=1))
    nv_ref[...] = jnp.where(row == WINDOW - 1, vn, pltpu.roll(cv, WINDOW - 1, axis=1))


def _attn_decode(q, k_new, v_new, cache_k, cache_v, sinks, norm_g):
    B = q.shape[0]
    tb = DEC_TILE_B

    def b3(d1, d2):
        return pl.BlockSpec((tb, d1, d2), lambda i: (i, 0, 0))

    return pl.pallas_call(
        _attn_dec_kernel,
        grid=(B // tb,),
        in_specs=[
            b3(N_HEADS, HEAD_DIM), b3(1, KV_DIM), b3(1, KV_DIM), b3(WINDOW, KV_DIM), b3(WINDOW, KV_DIM),
            pl.BlockSpec((1, N_HEADS, 1), lambda i: (0, 0, 0)),
            pl.BlockSpec((1, N_HEADS, HEAD_DIM), lambda i: (0, 0, 0)),
        ],
        out_specs=[b3(N_HEADS, HEAD_DIM), b3(WINDOW, KV_DIM), b3(WINDOW, KV_DIM)],
        out_shape=[
            jax.ShapeDtypeStruct((B, N_HEADS, HEAD_DIM), BF16),
            jax.ShapeDtypeStruct((B, WINDOW, KV_DIM), F32),
            jax.ShapeDtypeStruct((B, WINDOW, KV_DIM), F32),
        ],
        compiler_params=_params(1),
        name="attn_dec",
    )(q.reshape(B, N_HEADS, HEAD_DIM), k_new.reshape(B, 1, KV_DIM), v_new.reshape(B, 1, KV_DIM),
      cache_k.reshape(B, WINDOW, KV_DIM), cache_v.reshape(B, WINDOW, KV_DIM),
      sinks.reshape(1, N_HEADS, 1), norm_g.reshape(1, N_HEADS, HEAD_DIM))


def _mix_tail(attn_ref, gb_ref, uc, um1, um2, cw_ref, gconv_ref, wout_ref, bout_ref, x_ref, gate1_ref, ln1g_ref, ln1b_ref,
              shift2_ref, scale2_ref, wr_ref, br_ref, x1_ref, u2_ref, lg_ref):
    tm = uc.shape[0]
    n_split = max(tm // MIX_CHAIN_M, 1)
    h = tm // n_split
    cw = cw_ref[...]
    for s in range(n_split):
        r = slice(s * h, (s + 1) * h)

        def rows(ref):
            return ref[r, :] if ref.shape[0] == tm else ref[...]

        conv = cw[0:1, :] * um2[r] + cw[1:2, :] * um1[r] + cw[2:3, :] * uc[r]
        conv_n = _rms(gb_ref[r, :] * conv, gconv_ref[...]).astype(BF16)
        mixed = (jnp.dot(attn_ref[r, :], wout_ref[0:ATTN_DIM, :], preferred_element_type=F32)
                 + jnp.dot(conv_n, wout_ref[ATTN_DIM:D_MODEL, :], preferred_element_type=F32) + bout_ref[...])
        x1 = _standardise(DEEPNORM_ALPHA * x_ref[r, :] + rows(gate1_ref) * mixed) * ln1g_ref[...] + ln1b_ref[...]
        x1_ref[r, :] = x1
        u2 = _standardise(x1) * (1.0 + rows(scale2_ref)) + rows(shift2_ref)
        u2_ref[r, :] = u2
        lg_ref[:, r] = lax.dot_general(wr_ref[...], u2.astype(BF16), (((1,), (1,)), ((), ())),
                                       preferred_element_type=F32) + br_ref[...]


def _mix_seq_kernel(attn_ref, gb_ref, uc_ref, halo_ref, hist_ref, *rest):
    i = pl.program_id(0)
    uc = uc_ref[...]
    tm = uc.shape[0]
    above = jnp.where(i == 0, hist_ref[...], halo_ref[...])
    row = lax.broadcasted_iota(jnp.int32, (tm, 1), 0)
    um1 = jnp.where(row == 0, above[7:8, :], pltpu.roll(uc, 1, axis=0))
    um2 = jnp.where(row == 0, above[6:7, :], jnp.where(row == 1, above[7:8, :], pltpu.roll(uc, 2, axis=0)))
    _mix_tail(attn_ref, gb_ref, uc, um1, um2, *rest)


def _mix_tok_kernel(attn_ref, gb_ref, uc_ref, um1_ref, um2_ref, *rest):
    _mix_tail(attn_ref, gb_ref, uc_ref[...], um1_ref[...], um2_ref[...], *rest)


def _mix(attn_n, gb, uc, conv_prev, conv_w, norm_conv_g, w_out_bf, b_out, x, gate1, ln1_g, ln1_b, shift2, scale2,
         w_router_t_bf, b_router, tm, sequential):
    T = x.shape[0]
    per_row_mod = gate1.shape[0] != 1

    def row_spec(width):
        return pl.BlockSpec((tm, width), lambda i: (i, 0))

    def const_spec(rows, width):
        return pl.BlockSpec((rows, width), lambda i: (0, 0), pipeline_mode=pl.Buffered(1))

    mod_spec = row_spec(D_MODEL) if per_row_mod else const_spec(1, D_MODEL)
    if sequential:
        hist8 = jnp.concatenate([jnp.zeros((SUBLANES - 2, CONV_DIM), F32), conv_prev], axis=0)
        halo_blocks = tm // SUBLANES
        conv_specs = [pl.BlockSpec((SUBLANES, CONV_DIM), lambda i: (jnp.maximum(i * halo_blocks - 1, 0), 0)),
                      const_spec(SUBLANES, CONV_DIM)]
        conv_args = (uc, hist8)
        body = _mix_seq_kernel
    else:
        conv_specs = [row_spec(CONV_DIM), row_spec(CONV_DIM)]
        conv_args = conv_prev
        body = _mix_tok_kernel
    return pl.pallas_call(
        body,
        grid=(T // tm,),
        in_specs=[row_spec(ATTN_DIM), row_spec(CONV_DIM), row_spec(CONV_DIM), *conv_specs,
                  const_spec(CONV_WIDTH, CONV_DIM), const_spec(1, CONV_DIM),
                  const_spec(D_MODEL, D_MODEL), const_spec(1, D_MODEL),
                  row_spec(D_MODEL), mod_spec, const_spec(1, D_MODEL), const_spec(1, D_MODEL),
                  mod_spec, mod_spec,
                  const_spec(N_EXPERTS, D_MODEL), const_spec(N_EXPERTS, 1)],
        out_specs=[row_spec(D_MODEL), row_spec(D_MODEL), pl.BlockSpec((N_EXPERTS, tm), lambda i: (0, i))],
        out_shape=[jax.ShapeDtypeStruct((T, D_MODEL), F32), jax.ShapeDtypeStruct((T, D_MODEL), F32),
                   jax.ShapeDtypeStruct((N_EXPERTS, T), F32)],
        compiler_params=_params(1),
        name="mix_seq" if sequential else "mix_tok",
    )(attn_n, gb, uc, *conv_args, conv_w, norm_conv_g.reshape(1, CONV_DIM), w_out_bf, b_out.reshape(1, D_MODEL),
      x, gate1, ln1_g.reshape(1, D_MODEL), ln1_b.reshape(1, D_MODEL), shift2, scale2,
      w_router_t_bf, b_router.reshape(N_EXPERTS, 1))


def _route_kernel(lg_ref, eidx_ref, gate_ref, rank_ref, cnt_ref):
    n_tok = lg_ref.shape[1]
    r = lax.broadcasted_iota(jnp.int32, (TOKEN_TILE, TOKEN_TILE), 0)
    c = lax.broadcasted_iota(jnp.int32, (TOKEN_TILE, TOKEN_TILE), 1)
    earlier = (r < c).astype(BF16)
    expert = lax.broadcasted_iota(jnp.int32, (N_EXPERTS, TOKEN_TILE), 0).astype(F32)

    def body(ci, count):
        off = pl.multiple_of(ci * TOKEN_TILE, TOKEN_TILE)
        l = lg_ref[:, pl.ds(off, TOKEN_TILE)]
        vals, idxs, sels = [], [], []
        for _ in range(TOP_K):
            m = jnp.max(l, axis=0, keepdims=True)
            idx = jnp.min(jnp.where(l == m, expert, float(N_EXPERTS)), axis=0, keepdims=True)
            sel = expert == idx
            vals.append(m)
            idxs.append(idx)
            sels.append(sel)
            l = jnp.where(sel, -jnp.inf, l)
        chosen = jnp.where(sels[0] | sels[1] | sels[2] | sels[3], 1.0, 0.0)
        before = jnp.dot(chosen.astype(BF16), earlier, preferred_element_type=F32) + count
        ranks = [jnp.sum(jnp.where(s, before, 0.0), axis=0, keepdims=True) for s in sels]
        es = [jnp.exp(v - vals[0]) for v in vals]
        den = es[0] + es[1] + es[2] + es[3]
        eidx_ref[:, pl.ds(off, TOKEN_TILE)] = jnp.concatenate(idxs, axis=0).astype(jnp.int32)
        rank_ref[:, pl.ds(off, TOKEN_TILE)] = jnp.concatenate(ranks, axis=0).astype(jnp.int32)
        gate_ref[:, pl.ds(off, TOKEN_TILE)] = jnp.concatenate([e / den for e in es], axis=0)
        return count + jnp.sum(chosen, axis=1, keepdims=True)

    count = lax.fori_loop(0, n_tok // TOKEN_TILE, body, jnp.zeros((N_EXPERTS, 1), F32))
    cnt_ref[...] = jnp.broadcast_to(count, (N_EXPERTS, LANES)).astype(jnp.int32)


def _route(logits_t):
    n_tok = logits_t.shape[1]
    return pl.pallas_call(
        _route_kernel,
        out_shape=[jax.ShapeDtypeStruct((TOP_K, n_tok), jnp.int32), jax.ShapeDtypeStruct((TOP_K, n_tok), F32),
                   jax.ShapeDtypeStruct((TOP_K, n_tok), jnp.int32), jax.ShapeDtypeStruct((N_EXPERTS, LANES), jnp.int32)],
        compiler_params=pltpu.CompilerParams(vmem_limit_bytes=VMEM_LIMIT_BYTES),
        name="route",
    )(logits_t)


def _dest_kernel(start_ref, eidx_ref, rank_ref, dest_ref):
    e = eidx_ref[...]
    base = jnp.zeros(e.shape, jnp.int32)
    for x in range(N_EXPERTS):
        base = jnp.where(e == x, start_ref[x], base)
    dest_ref[...] = base + rank_ref[...]


def _dest(group_start, eidx, rank):
    return pl.pallas_call(
        _dest_kernel,
        in_specs=[pl.BlockSpec(memory_space=pltpu.SMEM), pl.BlockSpec(memory_space=pltpu.VMEM),
                  pl.BlockSpec(memory_space=pltpu.VMEM)],
        out_specs=pl.BlockSpec(memory_space=pltpu.VMEM),
        out_shape=jax.ShapeDtypeStruct(eidx.shape, jnp.int32),
        name="dest",
    )(group_start, eidx, rank)


def _dispatch_kernel(n_prompt_tiles, n_tiles, gend_ref, gsize_ref, dest_ref, up_ref, us_ref, xs_ref,
                     zero_buf, tiles, zsem, load_sem, scat_sem):
    i = pl.program_id(0)
    slot = lax.rem(i, DISPATCH_SLOTS)

    def start_load(tile, s):
        @pl.when(tile < n_prompt_tiles)
        def _():
            r = pl.multiple_of(tile * TOKEN_TILE, TOKEN_TILE)
            pltpu.make_async_copy(up_ref.at[pl.ds(r, TOKEN_TILE), :], tiles.at[s], load_sem.at[s]).start()

        @pl.when(tile >= n_prompt_tiles)
        def _():
            r = pl.multiple_of((tile - n_prompt_tiles) * TOKEN_TILE, TOKEN_TILE)
            pltpu.make_async_copy(us_ref.at[pl.ds(r, TOKEN_TILE), :], tiles.at[s], load_sem.at[s]).start()

    def wait_load(s):
        pltpu.make_async_copy(up_ref.at[pl.ds(0, TOKEN_TILE), :], tiles.at[s], load_sem.at[s]).wait()

    def wait_scatter(s):
        for _ in range(TOP_K):
            pltpu.make_async_copy(tiles.at[s], xs_ref.at[pl.ds(0, TOKEN_TILE), :], scat_sem.at[s]).wait()

    @pl.when(i == 0)
    def _():
        start_load(0, 0)
        zero_buf[...] = jnp.zeros_like(zero_buf)
        for e in range(N_EXPERTS):
            @pl.when(gsize_ref[e] > 0)
            def _():
                r0 = pl.multiple_of(gend_ref[e] - ROW_BLOCK, ROW_BLOCK)
                pltpu.make_async_copy(zero_buf, xs_ref.at[pl.ds(r0, ROW_BLOCK), :], zsem).start()
        for e in range(N_EXPERTS):
            @pl.when(gsize_ref[e] > 0)
            def _():
                pltpu.make_async_copy(zero_buf, xs_ref.at[pl.ds(0, ROW_BLOCK), :], zsem).wait()

    @pl.when(i + 1 < n_tiles)
    def _():
        nxt = lax.rem(i + 1, DISPATCH_SLOTS)

        @pl.when(i + 1 >= DISPATCH_SLOTS)
        def _():
            wait_scatter(nxt)
        start_load(i + 1, nxt)

    wait_load(slot)
    for s in range(DISPATCH_SLOTS):
        @pl.when(slot == s)
        def _():
            for j in range(TOP_K * TOKEN_TILE):
                d = dest_ref[0, 0, j]
                pltpu.make_async_copy(tiles.at[s, pl.ds(j % TOKEN_TILE, 1), :], xs_ref.at[pl.ds(d, 1), :],
                                      scat_sem.at[s]).start()

    @pl.when(i == n_tiles - 1)
    def _():
        for tile in range(max(n_tiles - DISPATCH_SLOTS, 0), n_tiles):
            wait_scatter(tile % DISPATCH_SLOTS)


def _dispatch(group_end, group_size, dest_tiles, u2_p, u2_s, n_rows):
    n_p = u2_p.shape[0] // TOKEN_TILE
    n_s = u2_s.shape[0] // TOKEN_TILE
    grid_spec = pltpu.PrefetchScalarGridSpec(
        num_scalar_prefetch=2,
        grid=(n_p + n_s,),
        in_specs=[
            pl.BlockSpec((1, 1, TOP_K * TOKEN_TILE), lambda i, *_: (i, 0, 0), memory_space=pltpu.SMEM),
            pl.BlockSpec(memory_space=pl.ANY),
            pl.BlockSpec(memory_space=pl.ANY),
        ],
        out_specs=pl.BlockSpec(memory_space=pl.ANY),
        scratch_shapes=[pltpu.VMEM((ROW_BLOCK, D_MODEL), F32), pltpu.VMEM((DISPATCH_SLOTS, TOKEN_TILE, D_MODEL), F32),
                        pltpu.SemaphoreType.DMA(()), pltpu.SemaphoreType.DMA((DISPATCH_SLOTS,)),
                        pltpu.SemaphoreType.DMA((DISPATCH_SLOTS,))],
    )
    return pl.pallas_call(
        functools.partial(_dispatch_kernel, n_p, n_p + n_s),
        grid_spec=grid_spec,
        out_shape=jax.ShapeDtypeStruct((n_rows, D_MODEL), F32),
        compiler_params=_params(1),
        name="dispatch",
    )(group_end, group_size, dest_tiles, u2_p, u2_s)


_FIRST, _MIDDLE, _LAST = 0, 1, 2


def _experts_kernel(row0_ref, nblk_ref, exp_ref, xs_ref, wg_ref, wl_ref, wd_ref, bg_ref, bl_ref, bd_ref,
                    ys_ref, xbuf, acc, in_sem, out_sem):
    i = pl.program_id(0)
    f = pl.program_id(1)
    last_f = pl.num_programs(1) - 1
    nblk = nblk_ref[i]
    row0 = row0_ref[i]

    def local(j):
        return pl.ds(pl.multiple_of(j * ROW_BLOCK, ROW_BLOCK), ROW_BLOCK)

    def in_hbm(j):
        return pl.ds(pl.multiple_of(row0 + j * ROW_BLOCK, ROW_BLOCK), ROW_BLOCK)

    def x_copy(j):
        return pltpu.make_async_copy(xs_ref.at[in_hbm(j), :], xbuf.at[local(j), :], in_sem.at[j])

    def y_copy(j):
        return pltpu.make_async_copy(acc.at[local(j), :], ys_ref.at[in_hbm(j), :], out_sem.at[j])

    def body(chunks, phase):
        wg = wg_ref[...].astype(BF16)
        wl = wl_ref[...].astype(BF16)
        wd = wd_ref[...].astype(BF16)
        for j0, m in chunks:
            rows = pl.ds(pl.multiple_of(j0 * ROW_BLOCK, ROW_BLOCK), m * ROW_BLOCK)
            if phase == _FIRST:
                for j in range(m):
                    x_copy(j0 + j).wait()
            x = xbuf[rows, :].astype(BF16)
            glu = jnp.dot(x, wg, preferred_element_type=F32) + bg_ref[...]
            lin = jnp.dot(x, wl, preferred_element_type=F32) + bl_ref[...]
            glu = jnp.minimum(glu, SWIGLU_LIMIT)
            lin = jnp.clip(lin, -SWIGLU_LIMIT, SWIGLU_LIMIT)
            act = glu * jax.nn.sigmoid(SWIGLU_ALPHA * glu) * (lin + 1.0)
            y = jnp.dot(act.astype(BF16), wd, preferred_element_type=F32)
            if phase == _FIRST:
                acc[rows, :] = y + bd_ref[...]
            else:
                acc[rows, :] += y
            if phase == _LAST:
                for j in range(m):
                    y_copy(j0 + j).start()

    def run(phase):
        def quad(q, carry):
            body(((4 * q, 2), (4 * q + 2, 2)), phase)
            return carry

        n_quads = lax.shift_right_logical(nblk, 2)
        lax.fori_loop(0, n_quads, quad, 0)
        done = lax.shift_left(n_quads, 2)

        @pl.when(jnp.bitwise_and(nblk, 2) == 2)
        def _():
            body(((done, 1), (done + 1, 1)), phase)

        @pl.when(jnp.bitwise_and(nblk, 1) == 1)
        def _():
            body(((nblk - 1, 1),), phase)

    @pl.when(f == 0)
    def _():
        def start(j, carry):
            x_copy(j).start()
            return carry
        lax.fori_loop(0, nblk, start, 0)
        run(_FIRST)

    @pl.when((f > 0) & (f < last_f))
    def _():
        run(_MIDDLE)

    @pl.when(f == last_f)
    def _():
        run(_LAST)

        def drain(j, carry):
            y_copy(j).wait()
            return carry
        lax.fori_loop(0, nblk, drain, 0)


def _experts(items, n_live, xs, w_up, b_up, w_down, b_down):
    row0, nblk, exp = items
    n_rows = xs.shape[0]
    n_f = D_FF // FF_TILE
    assert n_f >= 2

    max_rows = EXPERT_MAX_BLOCKS * ROW_BLOCK
    grid_spec = pltpu.PrefetchScalarGridSpec(
        num_scalar_prefetch=3,
        grid=(n_live, n_f),
        in_specs=[
            pl.BlockSpec(memory_space=pl.ANY),
            pl.BlockSpec((None, D_MODEL, FF_TILE), lambda i, f, r0, nb, ex: (ex[i], 0, f)),
            pl.BlockSpec((None, D_MODEL, FF_TILE), lambda i, f, r0, nb, ex: (ex[i], 0, n_f + f)),
            pl.BlockSpec((None, FF_TILE, D_MODEL), lambda i, f, r0, nb, ex: (ex[i], f, 0)),
            pl.BlockSpec((None, 1, FF_TILE), lambda i, f, r0, nb, ex: (ex[i], 0, f)),
            pl.BlockSpec((None, 1, FF_TILE), lambda i, f, r0, nb, ex: (ex[i], 0, n_f + f)),
            pl.BlockSpec((None, 1, D_MODEL), lambda i, f, r0, nb, ex: (ex[i], 0, 0)),
        ],
        out_specs=pl.BlockSpec(memory_space=pl.ANY),
        scratch_shapes=[pltpu.VMEM((max_rows, D_MODEL), F32), pltpu.VMEM((max_rows, D_MODEL), F32),
                        pltpu.SemaphoreType.DMA((EXPERT_MAX_BLOCKS,)), pltpu.SemaphoreType.DMA((EXPERT_MAX_BLOCKS,))],
    )
    return pl.pallas_call(
        _experts_kernel,
        grid_spec=grid_spec,
        out_shape=jax.ShapeDtypeStruct((n_rows, D_MODEL), F32),
        compiler_params=_params(2),
        name="experts",
    )(row0, nblk, exp, xs, w_up, w_up, w_down,
      b_up.reshape(N_EXPERTS, 1, 2 * D_FF), b_up.reshape(N_EXPERTS, 1, 2 * D_FF), b_down.reshape(N_EXPERTS, 1, D_MODEL))


def _combine_kernel(n_prompt_tiles, dcur_ref, dnxt_ref, gates_ref, x1p_ref, x1s_ref, g2p_ref, g2s_ref, lng_ref, lnb_ref,
                    ys_ref, yp_ref, ysm_ref, buf, sem):
    i = pl.program_id(0)
    n = pl.num_programs(0)
    slot = i % 2

    def gather(idx_ref, s):
        for j in range(TOP_K * TOKEN_TILE):
            d = idx_ref[0, 0, j]
            pltpu.make_async_copy(ys_ref.at[pl.ds(d, 1), :],
                                  buf.at[s, j // TOKEN_TILE, pl.ds(j % TOKEN_TILE, 1), :], sem.at[s]).start()

    @pl.when(i == 0)
    def _():
        gather(dcur_ref, 0)

    for s in range(2):
        @pl.when((i + 1 < n) & (slot == 1 - s))
        def _():
            gather(dnxt_ref, s)

    for k in range(TOP_K):
        pltpu.make_async_copy(ys_ref.at[pl.ds(0, TOKEN_TILE), :], buf.at[slot, k], sem.at[slot]).wait()

    gates = gates_ref[...]
    ffn = buf[slot, 0] * gates[:, 0:1]
    for k in range(1, TOP_K):
        ffn = ffn + buf[slot, k] * gates[:, k:k + 1]

    def finish(x1, gate2, out_ref):
        out_ref[...] = _standardise(DEEPNORM_ALPHA * x1 + gate2 * ffn) * lng_ref[...] + lnb_ref[...]

    @pl.when(i < n_prompt_tiles)
    def _():
        finish(x1p_ref[...], g2p_ref[...], yp_ref)

    @pl.when(i >= n_prompt_tiles)
    def _():
        finish(x1s_ref[...], g2s_ref[...], ysm_ref)


def _combine(dest_tiles, gates_tok, x1_p, x1_s, gate2_p, gate2_s, ln2_g, ln2_b, ys):
    n_p = x1_p.shape[0] // TOKEN_TILE
    n_s = x1_s.shape[0] // TOKEN_TILE
    n = n_p + n_s

    def p_idx(i):
        return jnp.minimum(i, n_p - 1)

    def s_idx(i):
        return jnp.maximum(i - n_p, 0)

    smem_tile = (1, 1, TOP_K * TOKEN_TILE)
    return pl.pallas_call(
        functools.partial(_combine_kernel, n_p),
        grid=(n,),
        in_specs=[
            pl.BlockSpec(smem_tile, lambda i: (i, 0, 0), memory_space=pltpu.SMEM),
            pl.BlockSpec(smem_tile, lambda i: (jnp.minimum(i + 1, n - 1), 0, 0), memory_space=pltpu.SMEM),
            pl.BlockSpec((TOKEN_TILE, TOP_K), lambda i: (i, 0)),
            pl.BlockSpec((TOKEN_TILE, D_MODEL), lambda i: (p_idx(i), 0)),
            pl.BlockSpec((TOKEN_TILE, D_MODEL), lambda i: (s_idx(i), 0)),
            pl.BlockSpec((1, D_MODEL), lambda i: (0, 0)),
            pl.BlockSpec((TOKEN_TILE, D_MODEL), lambda i: (s_idx(i), 0)),
            pl.BlockSpec((1, D_MODEL), lambda i: (0, 0)),
            pl.BlockSpec((1, D_MODEL), lambda i: (0, 0)),
            pl.BlockSpec(memory_space=pl.ANY),
        ],
        out_specs=[pl.BlockSpec((TOKEN_TILE, D_MODEL), lambda i: (p_idx(i), 0)),
                   pl.BlockSpec((TOKEN_TILE, D_MODEL), lambda i: (s_idx(i), 0))],
        out_shape=[jax.ShapeDtypeStruct(x1_p.shape, F32), jax.ShapeDtypeStruct(x1_s.shape, F32)],
        scratch_shapes=[pltpu.VMEM((2, TOP_K, TOKEN_TILE, D_MODEL), F32), pltpu.SemaphoreType.DMA((2,))],
        compiler_params=_params(1),
        name="combine",
    )(dest_tiles, dest_tiles, gates_tok, x1_p, x1_s, gate2_p, gate2_s,
      ln2_g.reshape(1, D_MODEL), ln2_b.reshape(1, D_MODEL), ys)


def _rope_tables(pos):
    half = HEAD_DIM // 2
    inv_freq = ROPE_THETA ** (-jnp.arange(half, dtype=F32) / half)
    ang = pos.astype(F32)[:, None] * inv_freq
    cos, sin = jnp.cos(ang), jnp.sin(ang)
    reps = LANES // HEAD_DIM
    return (jnp.tile(jnp.concatenate([cos, cos], axis=1), (1, reps)),
            jnp.tile(jnp.concatenate([-sin, sin], axis=1), (1, reps)))


def _work_items(group_start, group_size, n_items):
    chunk_rows = EXPERT_MAX_BLOCKS * ROW_BLOCK
    ex = jnp.arange(N_EXPERTS, dtype=jnp.int32)
    n_chunks = (group_size + chunk_rows - 1) // chunk_rows
    chunk_end = jnp.sum(jnp.where(ex[None, :] <= ex[:, None], n_chunks[None, :], 0), axis=1)
    chunk_start = chunk_end - n_chunks
    item = jnp.arange(n_items, dtype=jnp.int32)
    live = item < chunk_end[-1]
    it = jnp.minimum(item, chunk_end[-1] - 1)
    exp = jnp.minimum(jnp.sum((chunk_end[None, :] <= it[:, None]).astype(jnp.int32), axis=1), N_EXPERTS - 1)
    own = exp[:, None] == ex[None, :]

    def pick(per_expert):
        return jnp.sum(jnp.where(own, per_expert[None, :], 0), axis=1)

    chunk = it - pick(chunk_start)
    blocks = pick(group_size) // ROW_BLOCK
    per_chunk = (blocks + jnp.maximum(pick(n_chunks), 1) - 1) // jnp.maximum(pick(n_chunks), 1)
    row0 = pick(group_start) + chunk * per_chunk * ROW_BLOCK
    nblk = jnp.clip(blocks - chunk * per_chunk, 0, per_chunk)
    nblk = jnp.where(live, nblk, 0)
    return (row0.astype(jnp.int32), nblk.astype(jnp.int32), exp.astype(jnp.int32)), chunk_end[-1].astype(jnp.int32)


def _moe(u2_p, u2_s, logits_t, x1_p, x1_s, gate2_p, gate2_s, ln2_g, ln2_b, w_up, b_up, w_down, b_down):
    n_tok = logits_t.shape[1]
    eidx, gates, rank, counts = _route(logits_t)
    count = counts[:, 0]
    group_size = (count + ROW_BLOCK - 1) // ROW_BLOCK * ROW_BLOCK
    ex = jnp.arange(N_EXPERTS, dtype=jnp.int32)
    group_end = jnp.sum(jnp.where(ex[None, :] <= ex[:, None], group_size[None, :], 0), axis=1)
    group_start = group_end - group_size
    dest = _dest(group_start, eidx, rank)
    n_tiles = n_tok // TOKEN_TILE
    dest_tiles = dest.reshape(TOP_K, n_tiles, TOKEN_TILE).transpose(1, 0, 2).reshape(n_tiles, 1, TOP_K * TOKEN_TILE)
    max_rows = n_tok * TOP_K + N_EXPERTS * (ROW_BLOCK - 1)
    n_rows = (max_rows + ROW_BLOCK - 1) // ROW_BLOCK * ROW_BLOCK
    n_items = N_EXPERTS + max_rows // (EXPERT_MAX_BLOCKS * ROW_BLOCK)
    xs = _dispatch(group_end, group_size, dest_tiles, u2_p, u2_s, n_rows)
    items, n_live = _work_items(group_start, group_size, n_items)
    ys = _experts(items, n_live, xs, w_up, b_up, w_down, b_down)
    return _combine(dest_tiles, gates.T, x1_p, x1_s, gate2_p, gate2_s, ln2_g, ln2_b, ys)


def _layer(x_p, x_s, cache_k, cache_v, state_conv, c_p, c_s,
           w_ada, b_ada, w_in, b_in, conv_w, sinks, norm_attn_g, norm_conv_g, w_out, b_out,
           ln1_g, ln1_b, w_router, b_router, w_up, b_up, w_down, b_down, ln2_g, ln2_b, past_len):
    T = x_p.shape[0]
    B = x_s.shape[0]
    n_c = 1 + B
    pad_c = (-n_c) % SUBLANES
    c_all = jnp.concatenate([c_p, c_s, jnp.zeros((pad_c, D_MODEL), F32)], axis=0)
    mod = _ada(c_all, w_ada, b_ada)
    shift1, scale1, gate1, shift2, scale2, gate2 = [mod[:, j * D_MODEL:(j + 1) * D_MODEL] for j in range(6)]

    def prompt(a):
        return a[0:1]

    def sample(a):
        return a[1:n_c]

    w_in_bf = w_in.astype(BF16)
    w_out_bf = w_out.astype(BF16)
    w_router_t_bf = w_router.T.astype(BF16)

    cos_p, sin_p = _rope_tables(jnp.arange(T, dtype=jnp.int32))
    q_p, k_p, v_p, gb_p, uc_p = _inproj(x_p, prompt(shift1), prompt(scale1), cos_p, sin_p, w_in_bf, b_in,
                                        min(INPROJ_TILE_M, T))
    attn_p = _attn_prompt(q_p, k_p, v_p, sinks, norm_attn_g)
    x1_p, u2_p, lg_p = _mix(attn_p, gb_p, uc_p, jnp.zeros((CONV_WIDTH - 1, CONV_DIM), F32), conv_w, norm_conv_g,
                            w_out_bf, b_out, x_p, prompt(gate1), ln1_g, ln1_b, prompt(shift2), prompt(scale2),
                            w_router_t_bf, b_router, min(MIX_TILE_M, T), True)

    cos_s, sin_s = _rope_tables(jnp.full((1,), past_len, jnp.int32))
    q_s, k_s, v_s, gb_s, uc_s = _inproj(x_s, sample(shift1), sample(scale1), cos_s, sin_s, w_in_bf, b_in, B)
    attn_s, new_k_s, new_v_s = _attn_decode(q_s, k_s, v_s, cache_k, cache_v, sinks, norm_attn_g)
    x1_s, u2_s, lg_s = _mix(attn_s.reshape(B, ATTN_DIM), gb_s, uc_s, (state_conv[:, 1], state_conv[:, 0]), conv_w,
                            norm_conv_g, w_out_bf, b_out, x_s, sample(gate1), ln1_g, ln1_b, sample(shift2),
                            sample(scale2), w_router_t_bf, b_router, B, False)

    y_p, y_s = _moe(u2_p, u2_s, jnp.concatenate([lg_p, lg_s], axis=1), x1_p, x1_s, prompt(gate2), sample(gate2),
                    ln2_g, ln2_b, w_up, b_up, w_down, b_down)

    new_k_p = k_p[T - WINDOW:].reshape(WINDOW, N_KV_HEADS, HEAD_DIM)
    new_v_p = v_p[T - WINDOW:].reshape(WINDOW, N_KV_HEADS, HEAD_DIM)
    new_conv_p = uc_p[T - (CONV_WIDTH - 1):]
    new_conv_s = jnp.stack([state_conv[:, 1], uc_s], axis=1)
    return (y_p, y_s, new_k_p, new_v_p, new_conv_p,
            new_k_s.reshape(B, WINDOW, N_KV_HEADS, HEAD_DIM), new_v_s.reshape(B, WINDOW, N_KV_HEADS, HEAD_DIM), new_conv_s)


def kernel(x_prompt, x_sample, cache_k, cache_v, state_conv, c_prompt, c_sample, w_ada, b_ada, w_in, b_in, conv_w, sinks, norm_attn_g, norm_conv_g, w_out, b_out, ln1_g, ln1_b, w_router, b_router, w_up, b_up, w_down, b_down, ln2_g, ln2_b):
    assert x_prompt.shape[0] == 1 and x_sample.shape[1] == 1 and w_ada.shape[0] == DEPTH == 1
    B = x_sample.shape[0]
    (y_p, y_s, nk_p, nv_p, nc_p, nk_s, nv_s, nc_s) = _layer(
        x_prompt[0], x_sample[:, 0], cache_k[0], cache_v[0], state_conv[0], c_prompt, c_sample,
        w_ada[0], b_ada[0], w_in[0], b_in[0], conv_w[0], sinks[0], norm_attn_g[0], norm_conv_g[0], w_out[0], b_out[0],
        ln1_g[0], ln1_b[0], w_router[0], b_router[0], w_up[0], b_up[0], w_down[0], b_down[0], ln2_g[0], ln2_b[0],
        PAST_LEN)
    return (y_p[None], y_s.reshape(B, 1, D_MODEL), nk_p[None, None], nv_p[None, None], nc_p[None, None],
            nk_s[None], nv_s[None], nc_s[None])
```

```python
import functools

import jax
import jax.numpy as jnp
from jax import lax
from jax.experimental import pallas as pl
from jax.experimental.pallas import tpu as pltpu

F32 = jnp.float32
BF16 = jnp.bfloat16

D_MODEL = 2048
HEAD_DIM = 64
N_HEADS = 16
N_KV_HEADS = 4
GQA_GROUP = N_HEADS // N_KV_HEADS
ATTN_DIM = N_HEADS * HEAD_DIM
KV_DIM = N_KV_HEADS * HEAD_DIM
CONV_DIM = D_MODEL - ATTN_DIM
CONV_WIDTH = 3
IN_DIM = ATTN_DIM + 2 * KV_DIM + 3 * CONV_DIM
WINDOW = 128
PAST_LEN = 16384
ROPE_THETA = 10000.0
N_EXPERTS = 32
TOP_K = 4
D_FF = D_MODEL
SWIGLU_LIMIT = 7.0
SWIGLU_ALPHA = 1.702
DEPTH = 1
DEEPNORM_ALPHA = (2.0 * DEPTH) ** 0.25
LN_EPS = 1e-5
RMS_EPS = 1e-6
COL_K = ATTN_DIM
COL_V = COL_K + KV_DIM
COL_B = COL_V + KV_DIM
COL_C = COL_B + CONV_DIM
COL_X = COL_C + CONV_DIM

LANES = 128
SUBLANES = 8
VMEM_LIMIT_BYTES = 60 * 1024 * 1024

TOKEN_TILE = 128
DISPATCH_SLOTS = 3
ADA_TILE_N = 1024
INPROJ_TILE_M = 512
INPROJ_CHUNK_N = 512
ATTN_BLOCKS = 4
MIX_TILE_M = 512
MIX_CHAIN_M = 256
DEC_TILE_B = 16
ROW_BLOCK = 256
EXPERT_MAX_BLOCKS = 9
FF_TILE = 256
NEG_BIG = -1e30


def _params(n_axes, vmem=VMEM_LIMIT_BYTES):
    return pltpu.CompilerParams(dimension_semantics=("arbitrary",) * n_axes, vmem_limit_bytes=vmem)


def _standardise(x):
    mu = jnp.mean(x, axis=-1, keepdims=True)
    xc = x - mu
    var = jnp.mean(xc * xc, axis=-1, keepdims=True)
    return xc * lax.rsqrt(var + LN_EPS)


def _rms(x, g):
    return x * lax.rsqrt(jnp.mean(x * x, axis=-1, keepdims=True) + RMS_EPS) * g


def _ada_kernel(c_ref, w_ref, b_ref, o_ref):
    c = c_ref[...]
    s = (c * jax.nn.sigmoid(c)).astype(BF16)
    o_ref[...] = jnp.dot(s, w_ref[...].astype(BF16), preferred_element_type=F32) + b_ref[...]


def _ada(c_all, w_ada, b_ada):
    rows = c_all.shape[0]
    n_out = w_ada.shape[1]
    return pl.pallas_call(
        _ada_kernel,
        grid=(n_out // ADA_TILE_N,),
        in_specs=[
            pl.BlockSpec((rows, D_MODEL), lambda j: (0, 0)),
            pl.BlockSpec((D_MODEL, ADA_TILE_N), lambda j: (0, j)),
            pl.BlockSpec((1, ADA_TILE_N), lambda j: (0, j)),
        ],
        out_specs=pl.BlockSpec((rows, ADA_TILE_N), lambda j: (0, j)),
        out_shape=jax.ShapeDtypeStruct((rows, n_out), F32),
        compiler_params=_params(1),
        name="ada",
    )(c_all, w_ada, b_ada.reshape(1, n_out))


def _inproj_kernel(x_ref, shift_ref, scale_ref, cos_ref, sin_ref, w_ref, b_ref,
                   q_ref, k_ref, v_ref, gb_ref, uc_ref):
    u = (_standardise(x_ref[...]) * (1.0 + scale_ref[...]) + shift_ref[...]).astype(BF16)
    cos = cos_ref[...]
    sin = sin_ref[...]
    lane = lax.broadcasted_iota(jnp.int32, (1, LANES), 1)
    first_half = (lane % HEAD_DIM) < (HEAD_DIM // 2)

    def rope(z):
        partner = jnp.where(first_half, pltpu.roll(z, LANES - HEAD_DIM // 2, axis=1),
                            pltpu.roll(z, HEAD_DIM // 2, axis=1))
        return z * cos + partner * sin

    def proj(c0):
        w = w_ref[:, c0:c0 + INPROJ_CHUNK_N]
        return jnp.dot(u, w, preferred_element_type=F32) + b_ref[:, c0:c0 + INPROJ_CHUNK_N]

    groups = INPROJ_CHUNK_N // LANES
    for j in range(ATTN_DIM // INPROJ_CHUNK_N):
        z = proj(j * INPROJ_CHUNK_N)
        for g in range(groups):
            c0 = j * INPROJ_CHUNK_N + g * LANES
            q_ref[:, c0:c0 + LANES] = (rope(z[:, g * LANES:(g + 1) * LANES]) * (HEAD_DIM ** -0.5)).astype(BF16)
    z = proj(COL_K)
    for g in range(KV_DIM // LANES):
        k_ref[:, g * LANES:(g + 1) * LANES] = rope(z[:, g * LANES:(g + 1) * LANES])
    v_ref[...] = z[:, KV_DIM:2 * KV_DIM]
    for j in range(CONV_DIM // INPROJ_CHUNK_N):
        sl = slice(j * INPROJ_CHUNK_N, (j + 1) * INPROJ_CHUNK_N)
        gb_ref[:, sl] = proj(COL_B + j * INPROJ_CHUNK_N)
        uc_ref[:, sl] = proj(COL_C + j * INPROJ_CHUNK_N) * proj(COL_X + j * INPROJ_CHUNK_N)


def _inproj(x, shift, scale, cos, sin, w_in_bf, b_in, tm):
    T = x.shape[0]
    per_row_mod = shift.shape[0] != 1
    per_row_pos = cos.shape[0] != 1
    mod_spec = pl.BlockSpec((tm, D_MODEL), lambda i: (i, 0)) if per_row_mod else pl.BlockSpec((1, D_MODEL), lambda i: (0, 0))
    pos_spec = pl.BlockSpec((tm, LANES), lambda i: (i, 0)) if per_row_pos else pl.BlockSpec((1, LANES), lambda i: (0, 0))

    def row_spec(width):
        return pl.BlockSpec((tm, width), lambda i: (i, 0))

    return pl.pallas_call(
        _inproj_kernel,
        grid=(T // tm,),
        in_specs=[
            row_spec(D_MODEL), mod_spec, mod_spec, pos_spec, pos_spec,
            pl.BlockSpec((D_MODEL, IN_DIM), lambda i: (0, 0), pipeline_mode=pl.Buffered(1)),
            pl.BlockSpec((1, IN_DIM), lambda i: (0, 0)),
        ],
        out_specs=[row_spec(ATTN_DIM), row_spec(KV_DIM), row_spec(KV_DIM), row_spec(CONV_DIM), row_spec(CONV_DIM)],
        out_shape=[
            jax.ShapeDtypeStruct((T, ATTN_DIM), BF16),
            jax.ShapeDtypeStruct((T, KV_DIM), F32),
            jax.ShapeDtypeStruct((T, KV_DIM), F32),
            jax.ShapeDtypeStruct((T, CONV_DIM), F32),
            jax.ShapeDtypeStruct((T, CONV_DIM), F32),
        ],
        compiler_params=_params(1),
        name="inproj",
    )(x, shift, scale, cos, sin, w_in_bf, b_in.reshape(1, IN_DIM))


def _attn_kernel(q_ref, kp_ref, kc_ref, vp_ref, vc_ref, bias_ref, sink_ref, g_ref, o_ref):
    n = pl.program_id(0)
    n_blocks = q_ref.shape[0] // WINDOW
    k = jnp.concatenate([kp_ref[...], kc_ref[...]], axis=0).astype(BF16)
    v = jnp.concatenate([vp_ref[...], vc_ref[...]], axis=0).astype(BF16)
    for b in range(n_blocks):
        q = q_ref[b * WINDOW:(b + 1) * WINDOW, :]
        kb = k[b * WINDOW:(b + 2) * WINDOW, :]
        vb = v[b * WINDOW:(b + 2) * WINDOW, :]
        bias = bias_ref[jnp.minimum(n, 1)] if b == 0 else bias_ref[1]
        outs = []
        for g in range(N_KV_HEADS):
            heads = range(g * GQA_GROUP, (g + 1) * GQA_GROUP)
            qg = jnp.concatenate([q[:, h * HEAD_DIM:(h + 1) * HEAD_DIM] for h in heads], axis=0)
            kg = kb[:, g * HEAD_DIM:(g + 1) * HEAD_DIM]
            vg = vb[:, g * HEAD_DIM:(g + 1) * HEAD_DIM]
            sink = sink_ref[g]
            s = lax.dot_general(kg, qg, (((1,), (1,)), ((), ())), preferred_element_type=F32) + bias
            m = jnp.maximum(jnp.max(s, axis=0, keepdims=True), sink)
            e = jnp.exp(s - m)
            den = jnp.sum(e, axis=0, keepdims=True) + jnp.exp(sink - m)
            o = lax.dot_general(vg, e.astype(BF16), (((0,), (0,)), ((), ())), preferred_element_type=F32)
            o = o / den
            outs.extend(o[:, j * WINDOW:(j + 1) * WINDOW] for j in range(GQA_GROUP))
        o = jnp.concatenate(outs, axis=0).T
        o_ref[b * WINDOW:(b + 1) * WINDOW, :] = _rms(o, g_ref[...]).astype(BF16)


def _attn_mask_bias():
    qi = jnp.arange(GQA_GROUP * WINDOW, dtype=jnp.int32)[None, :] % WINDOW
    ks = jnp.arange(2 * WINDOW, dtype=jnp.int32)[:, None]
    later = (ks >= qi) & (ks <= qi + WINDOW)
    first = later & (ks >= WINDOW)
    return jnp.where(jnp.stack([first, later]), 0.0, NEG_BIG).astype(F32)


def _attn_prompt(q, k, v, sinks, norm_g):
    T = q.shape[0]
    nb = ATTN_BLOCKS if T % (ATTN_BLOCKS * WINDOW) == 0 else 1

    def cur(width):
        return pl.BlockSpec((nb * WINDOW, width), lambda n: (n, 0))

    def prev(width):
        return pl.BlockSpec((WINDOW, width), lambda n: (jnp.maximum(n * nb - 1, 0), 0))

    return pl.pallas_call(
        _attn_kernel,
        grid=(T // (nb * WINDOW),),
        in_specs=[
            cur(ATTN_DIM), prev(KV_DIM), cur(KV_DIM), prev(KV_DIM), cur(KV_DIM),
            pl.BlockSpec((2, 2 * WINDOW, GQA_GROUP * WINDOW), lambda n: (0, 0, 0)),
            pl.BlockSpec((N_KV_HEADS, 1, GQA_GROUP * WINDOW), lambda n: (0, 0, 0)),
            pl.BlockSpec((1, ATTN_DIM), lambda n: (0, 0)),
        ],
        out_specs=cur(ATTN_DIM),
        out_shape=jax.ShapeDtypeStruct((T, ATTN_DIM), BF16),
        compiler_params=_params(1),
        name="attn",
    )(q, k, k, v, v, _attn_mask_bias(),
      jnp.repeat(sinks, WINDOW).reshape(N_KV_HEADS, 1, GQA_GROUP * WINDOW),
      norm_g.reshape(1, ATTN_DIM))


def _attn_dec_kernel(q_ref, kn_ref, vn_ref, ck_ref, cv_ref, sink_ref, g_ref, o_ref, nk_ref, nv_ref):
    tb = q_ref.shape[0]
    q = q_ref[...].astype(F32)
    lane_group = lax.broadcasted_iota(jnp.int32, (1, N_HEADS, KV_DIM), 2) // HEAD_DIM
    head_group = lax.broadcasted_iota(jnp.int32, (1, N_HEADS, KV_DIM), 1) // GQA_GROUP
    own = lane_group == head_group
    qe = jnp.where(own, jnp.concatenate([q] * N_KV_HEADS, axis=2), 0.0)
    ck = ck_ref[...]
    cv = cv_ref[...]
    kn = kn_ref[...]
    vn = vn_ref[...]
    s = jnp.einsum("bhc,bwc->bhw", qe.astype(BF16), ck.astype(BF16), preferred_element_type=F32)
    s_new = jnp.sum(qe.astype(BF16).astype(F32) * kn.astype(BF16).astype(F32), axis=2, keepdims=True)
    sink = sink_ref[...]
    m = jnp.maximum(jnp.maximum(jnp.max(s, axis=2, keepdims=True), s_new), sink)
    e = jnp.exp(s - m)
    e_new = jnp.exp(s_new - m)
    den = jnp.sum(e, axis=2, keepdims=True) + e_new + jnp.exp(sink - m)
    p = (e / den).astype(BF16)
    p_new = (e_new / den).astype(BF16).astype(F32)
    o = jnp.einsum("bhw,bwc->bhc", p, cv.astype(BF16), preferred_element_type=F32)
    o = o + p_new * vn.astype(BF16).astype(F32)
    o = jnp.where(own, o, 0.0)
    oh = o[:, :, 0:HEAD_DIM]
    for g in range(1, N_KV_HEADS):
        oh = oh + o[:, :, g * HEAD_DIM:(g + 1) * HEAD_DIM]
    ms = jnp.sum(jnp.sum(oh * oh, axis=2, keepdims=True), axis=1, keepdims=True) / ATTN_DIM
    o_ref[...] = (oh * lax.rsqrt(ms + RMS_EPS) * g_ref[...]).astype(BF16)
    row = lax.broadcasted_iota(jnp.int32, (1, WINDOW, 1), 1)
    nk_ref[...] = jnp.where(row == WINDOW - 1, kn, pltpu.roll(ck, WINDOW - 1, axis=1))
    nv_ref[...] = jnp.where(row == WINDOW - 1, vn, pltpu.roll(cv, WINDOW - 1, axis=1))


def _attn_decode(q, k_new, v_new, cache_k, cache_v, sinks, norm_g):
    B = q.shape[0]
    tb = DEC_TILE_B

    def b3(d1, d2):
        return pl.BlockSpec((tb, d1, d2), lambda i: (i, 0, 0))

    return pl.pallas_call(
        _attn_dec_kernel,
        grid=(B // tb,),
        in_specs=[
            b3(N_HEADS, HEAD_DIM), b3(1, KV_DIM), b3(1, KV_DIM), b3(WINDOW, KV_DIM), b3(WINDOW, KV_DIM),
            pl.BlockSpec((1, N_HEADS, 1), lambda i: (0, 0, 0)),
            pl.BlockSpec((1, N_HEADS, HEAD_DIM), lambda i: (0, 0, 0)),
        ],
        out_specs=[b3(N_HEADS, HEAD_DIM), b3(WINDOW, KV_DIM), b3(WINDOW, KV_DIM)],
        out_shape=[
            jax.ShapeDtypeStruct((B, N_HEADS, HEAD_DIM), BF16),
            jax.ShapeDtypeStruct((B, WINDOW, KV_DIM), F32),
            jax.ShapeDtypeStruct((B, WINDOW, KV_DIM), F32),
        ],
        compiler_params=_params(1),
        name="attn_dec",
    )(q.reshape(B, N_HEADS, HEAD_DIM), k_new.reshape(B, 1, KV_DIM), v_new.reshape(B, 1, KV_DIM),
      cache_k.reshape(B, WINDOW, KV_DIM), cache_v.reshape(B, WINDOW, KV_DIM),
      sinks.reshape(1, N_HEADS, 1), norm_g.reshape(1, N_HEADS, HEAD_DIM))


def _mix_tail(attn_ref, gb_ref, uc, um1, um2, cw_ref, gconv_ref, wout_ref, bout_ref, x_ref, gate1_ref, ln1g_ref, ln1b_ref,
              shift2_ref, scale2_ref, wr_ref, br_ref, x1_ref, u2_ref, lg_ref):
    tm = uc.shape[0]
    n_split = max(tm // MIX_CHAIN_M, 1)
    h = tm // n_split
    cw = cw_ref[...]
    for s in range(n_split):
        r = slice(s * h, (s + 1) * h)

        def rows(ref):
            return ref[r, :] if ref.shape[0] == tm else ref[...]

        conv = cw[0:1, :] * um2[r] + cw[1:2, :] * um1[r] + cw[2:3, :] * uc[r]
        conv_n = _rms(gb_ref[r, :] * conv, gconv_ref[...]).astype(BF16)
        mixed = (jnp.dot(attn_ref[r, :], wout_ref[0:ATTN_DIM, :], preferred_element_type=F32)
                 + jnp.dot(conv_n, wout_ref[ATTN_DIM:D_MODEL, :], preferred_element_type=F32) + bout_ref[...])
        x1 = _standardise(DEEPNORM_ALPHA * x_ref[r, :] + rows(gate1_ref) * mixed) * ln1g_ref[...] + ln1b_ref[...]
        x1_ref[r, :] = x1
        u2 = _standardise(x1) * (1.0 + rows(scale2_ref)) + rows(shift2_ref)
        u2_ref[r, :] = u2
        lg_ref[:, r] = lax.dot_general(wr_ref[...], u2.astype(BF16), (((1,), (1,)), ((), ())),
                                       preferred_element_type=F32) + br_ref[...]


def _mix_seq_kernel(attn_ref, gb_ref, uc_ref, halo_ref, hist_ref, *rest):
    i = pl.program_id(0)
    uc = uc_ref[...]
    tm = uc.shape[0]
    above = jnp.where(i == 0, hist_ref[...], halo_ref[...])
    row = lax.broadcasted_iota(jnp.int32, (tm, 1), 0)
    um1 = jnp.where(row == 0, above[7:8, :], pltpu.roll(uc, 1, axis=0))
    um2 = jnp.where(row == 0, above[6:7, :], jnp.where(row == 1, above[7:8, :], pltpu.roll(uc, 2, axis=0)))
    _mix_tail(attn_ref, gb_ref, uc, um1, um2, *rest)


def _mix_tok_kernel(attn_ref, gb_ref, uc_ref, um1_ref, um2_ref, *rest):
    _mix_tail(attn_ref, gb_ref, uc_ref[...], um1_ref[...], um2_ref[...], *rest)


def _mix(attn_n, gb, uc, conv_prev, conv_w, norm_conv_g, w_out_bf, b_out, x, gate1, ln1_g, ln1_b, shift2, scale2,
         w_router_t_bf, b_router, tm, sequential):
    T = x.shape[0]
    per_row_mod = gate1.shape[0] != 1

    def row_spec(width):
        return pl.BlockSpec((tm, width), lambda i: (i, 0))

    def const_spec(rows, width):
        return pl.BlockSpec((rows, width), lambda i: (0, 0), pipeline_mode=pl.Buffered(1))

    mod_spec = row_spec(D_MODEL) if per_row_mod else const_spec(1, D_MODEL)
    if sequential:
        hist8 = jnp.concatenate([jnp.zeros((SUBLANES - 2, CONV_DIM), F32), conv_prev], axis=0)
        halo_blocks = tm // SUBLANES
        conv_specs = [pl.BlockSpec((SUBLANES, CONV_DIM), lambda i: (jnp.maximum(i * halo_blocks - 1, 0), 0)),
                      const_spec(SUBLANES, CONV_DIM)]
        conv_args = (uc, hist8)
        body = _mix_seq_kernel
    else:
        conv_specs = [row_spec(CONV_DIM), row_spec(CONV_DIM)]
        conv_args = conv_prev
        body = _mix_tok_kernel
    return pl.pallas_call(
        body,
        grid=(T // tm,),
        in_specs=[row_spec(ATTN_DIM), row_spec(CONV_DIM), row_spec(CONV_DIM), *conv_specs,
                  const_spec(CONV_WIDTH, CONV_DIM), const_spec(1, CONV_DIM),
                  const_spec(D_MODEL, D_MODEL), const_spec(1, D_MODEL),
                  row_spec(D_MODEL), mod_spec, const_spec(1, D_MODEL), const_spec(1, D_MODEL),
                  mod_spec, mod_spec,
                  const_spec(N_EXPERTS, D_MODEL), const_spec(N_EXPERTS, 1)],
        out_specs=[row_spec(D_MODEL), row_spec(D_MODEL), pl.BlockSpec((N_EXPERTS, tm), lambda i: (0, i))],
        out_shape=[jax.ShapeDtypeStruct((T, D_MODEL), F32), jax.ShapeDtypeStruct((T, D_MODEL), F32),
                   jax.ShapeDtypeStruct((N_EXPERTS, T), F32)],
        compiler_params=_params(1),
        name="mix_seq" if sequential else "mix_tok",
    )(attn_n, gb, uc, *conv_args, conv_w, norm_conv_g.reshape(1, CONV_DIM), w_out_bf, b_out.reshape(1, D_MODEL),
      x, gate1, ln1_g.reshape(1, D_MODEL), ln1_b.reshape(1, D_MODEL), shift2, scale2,
      w_router_t_bf, b_router.reshape(N_EXPERTS, 1))


def _route_kernel(lg_ref, eidx_ref, gate_ref, rank_ref, cnt_ref):
    n_tok = lg_ref.shape[1]
    r = lax.broadcasted_iota(jnp.int32, (TOKEN_TILE, TOKEN_TILE), 0)
    c = lax.broadcasted_iota(jnp.int32, (TOKEN_TILE, TOKEN_TILE), 1)
    earlier = (r < c).astype(BF16)
    expert = lax.broadcasted_iota(jnp.int32, (N_EXPERTS, TOKEN_TILE), 0).astype(F32)

    def body(ci, count):
        off = pl.multiple_of(ci * TOKEN_TILE, TOKEN_TILE)
        l = lg_ref[:, pl.ds(off, TOKEN_TILE)]
        vals, idxs, sels = [], [], []
        for _ in range(TOP_K):
            m = jnp.max(l, axis=0, keepdims=True)
            idx = jnp.min(jnp.where(l == m, expert, float(N_EXPERTS)), axis=0, keepdims=True)
            sel = expert == idx
            vals.append(m)
            idxs.append(idx)
            sels.append(sel)
            l = jnp.where(sel, -jnp.inf, l)
        chosen = jnp.where(sels[0] | sels[1] | sels[2] | sels[3], 1.0, 0.0)
        before = jnp.dot(chosen.astype(BF16), earlier, preferred_element_type=F32) + count
        ranks = [jnp.sum(jnp.where(s, before, 0.0), axis=0, keepdims=True) for s in sels]
        es = [jnp.exp(v - vals[0]) for v in vals]
        den = es[0] + es[1] + es[2] + es[3]
        eidx_ref[:, pl.ds(off, TOKEN_TILE)] = jnp.concatenate(idxs, axis=0).astype(jnp.int32)
        rank_ref[:, pl.ds(off, TOKEN_TILE)] = jnp.concatenate(ranks, axis=0).astype(jnp.int32)
        gate_ref[:, pl.ds(off, TOKEN_TILE)] = jnp.concatenate([e / den for e in es], axis=0)
        return count + jnp.sum(chosen, axis=1, keepdims=True)

    count = lax.fori_loop(0, n_tok // TOKEN_TILE, body, jnp.zeros((N_EXPERTS, 1), F32))
    cnt_ref[...] = jnp.broadcast_to(count, (N_EXPERTS, LANES)).astype(jnp.int32)


def _route(logits_t):
    n_tok = logits_t.shape[1]
    return pl.pallas_call(
        _route_kernel,
        out_shape=[jax.ShapeDtypeStruct((TOP_K, n_tok), jnp.int32), jax.ShapeDtypeStruct((TOP_K, n_tok), F32),
                   jax.ShapeDtypeStruct((TOP_K, n_tok), jnp.int32), jax.ShapeDtypeStruct((N_EXPERTS, LANES), jnp.int32)],
        compiler_params=pltpu.CompilerParams(vmem_limit_bytes=VMEM_LIMIT_BYTES),
        name="route",
    )(logits_t)


def _dest_kernel(start_ref, eidx_ref, rank_ref, dest_ref):
    e = eidx_ref[...]
    base = jnp.zeros(e.shape, jnp.int32)
    for x in range(N_EXPERTS):
        base = jnp.where(e == x, start_ref[x], base)
    dest_ref[...] = base + rank_ref[...]


def _dest(group_start, eidx, rank):
    return pl.pallas_call(
        _dest_kernel,
        in_specs=[pl.BlockSpec(memory_space=pltpu.SMEM), pl.BlockSpec(memory_space=pltpu.VMEM),
                  pl.BlockSpec(memory_space=pltpu.VMEM)],
        out_specs=pl.BlockSpec(memory_space=pltpu.VMEM),
        out_shape=jax.ShapeDtypeStruct(eidx.shape, jnp.int32),
        name="dest",
    )(group_start, eidx, rank)


def _dispatch_kernel(n_prompt_tiles, n_tiles, gend_ref, gsize_ref, dest_ref, up_ref, us_ref, xs_ref,
                     zero_buf, tiles, zsem, load_sem, scat_sem):
    i = pl.program_id(0)
    slot = lax.rem(i, DISPATCH_SLOTS)

    def start_load(tile, s):
        @pl.when(tile < n_prompt_tiles)
        def _():
            r = pl.multiple_of(tile * TOKEN_TILE, TOKEN_TILE)
            pltpu.make_async_copy(up_ref.at[pl.ds(r, TOKEN_TILE), :], tiles.at[s], load_sem.at[s]).start()

        @pl.when(tile >= n_prompt_tiles)
        def _():
            r = pl.multiple_of((tile - n_prompt_tiles) * TOKEN_TILE, TOKEN_TILE)
            pltpu.make_async_copy(us_ref.at[pl.ds(r, TOKEN_TILE), :], tiles.at[s], load_sem.at[s]).start()

    def wait_load(s):
        pltpu.make_async_copy(up_ref.at[pl.ds(0, TOKEN_TILE), :], tiles.at[s], load_sem.at[s]).wait()

    def wait_scatter(s):
        for _ in range(TOP_K):
            pltpu.make_async_copy(tiles.at[s], xs_ref.at[pl.ds(0, TOKEN_TILE), :], scat_sem.at[s]).wait()

    @pl.when(i == 0)
    def _():
        start_load(0, 0)
        zero_buf[...] = jnp.zeros_like(zero_buf)
        for e in range(N_EXPERTS):
            @pl.when(gsize_ref[e] > 0)
            def _():
                r0 = pl.multiple_of(gend_ref[e] - ROW_BLOCK, ROW_BLOCK)
                pltpu.make_async_copy(zero_buf, xs_ref.at[pl.ds(r0, ROW_BLOCK), :], zsem).start()
        for e in range(N_EXPERTS):
            @pl.when(gsize_ref[e] > 0)
            def _():
                pltpu.make_async_copy(zero_buf, xs_ref.at[pl.ds(0, ROW_BLOCK), :], zsem).wait()

        first_unused = gend_ref[N_EXPERTS - 1] // ROW_BLOCK
        n_blocks = xs_ref.shape[0] // ROW_BLOCK

        def zero_block(b, carry):
            r0 = pl.multiple_of(b * ROW_BLOCK, ROW_BLOCK)
            pltpu.make_async_copy(zero_buf, xs_ref.at[pl.ds(r0, ROW_BLOCK), :], zsem).start()
            return carry

        def zero_wait(b, carry):
            pltpu.make_async_copy(zero_buf, xs_ref.at[pl.ds(0, ROW_BLOCK), :], zsem).wait()
            return carry

        lax.fori_loop(first_unused, n_blocks, zero_block, 0)
        lax.fori_loop(first_unused, n_blocks, zero_wait, 0)

    @pl.when(i + 1 < n_tiles)
    def _():
        nxt = lax.rem(i + 1, DISPATCH_SLOTS)

        @pl.when(i + 1 >= DISPATCH_SLOTS)
        def _():
            wait_scatter(nxt)
        start_load(i + 1, nxt)

    wait_load(slot)
    for s in range(DISPATCH_SLOTS):
        @pl.when(slot == s)
        def _():
            for j in range(TOP_K * TOKEN_TILE):
                d = dest_ref[0, 0, j]
                pltpu.make_async_copy(tiles.at[s, pl.ds(j % TOKEN_TILE, 1), :], xs_ref.at[pl.ds(d, 1), :],
                                      scat_sem.at[s]).start(priority=j % 2)

    @pl.when(i == n_tiles - 1)
    def _():
        for tile in range(max(n_tiles - DISPATCH_SLOTS, 0), n_tiles):
            wait_scatter(tile % DISPATCH_SLOTS)


def _dispatch(group_end, group_size, dest_tiles, u2_p, u2_s, n_rows):
    n_p = u2_p.shape[0] // TOKEN_TILE
    n_s = u2_s.shape[0] // TOKEN_TILE
    grid_spec = pltpu.PrefetchScalarGridSpec(
        num_scalar_prefetch=2,
        grid=(n_p + n_s,),
        in_specs=[
            pl.BlockSpec((1, 1, TOP_K * TOKEN_TILE), lambda i, *_: (i, 0, 0), memory_space=pltpu.SMEM),
            pl.BlockSpec(memory_space=pl.ANY),
            pl.BlockSpec(memory_space=pl.ANY),
        ],
        out_specs=pl.BlockSpec(memory_space=pl.ANY),
        scratch_shapes=[pltpu.VMEM((ROW_BLOCK, D_MODEL), F32), pltpu.VMEM((DISPATCH_SLOTS, TOKEN_TILE, D_MODEL), F32),
                        pltpu.SemaphoreType.DMA(()), pltpu.SemaphoreType.DMA((DISPATCH_SLOTS,)),
                        pltpu.SemaphoreType.DMA((DISPATCH_SLOTS,))],
    )
    return pl.pallas_call(
        functools.partial(_dispatch_kernel, n_p, n_p + n_s),
        grid_spec=grid_spec,
        out_shape=jax.ShapeDtypeStruct((n_rows, D_MODEL), F32),
        compiler_params=_params(1),
        name="dispatch",
    )(group_end, group_size, dest_tiles, u2_p, u2_s)


_FIRST, _MIDDLE, _LAST = 0, 1, 2


def _experts_kernel(row0_ref, nblk_ref, exp_ref, total_ref, xs_ref, wg_ref, wl_ref, wd_ref, bg_ref, bl_ref, bd_ref,
                    ys_ref, xbuf, acc, in_sem, out_sem):
    i = pl.program_id(0)
    f = pl.program_id(1)
    n_items = pl.num_programs(0)
    last_f = pl.num_programs(1) - 1
    nblk = nblk_ref[i]
    row0 = row0_ref[i]
    n_started = jnp.minimum(jnp.where(i > 0, nblk_ref[jnp.maximum(i - 1, 0)], 0), nblk)
    nxt = jnp.minimum(i + 1, n_items - 1)
    row0_next = row0_ref[nxt]
    nblk_next = jnp.where(i + 1 < n_items, nblk_ref[nxt], 0)

    def local(j):
        return pl.ds(pl.multiple_of(j * ROW_BLOCK, ROW_BLOCK), ROW_BLOCK)

    def in_hbm(r0, j):
        return pl.ds(pl.multiple_of(r0 + j * ROW_BLOCK, ROW_BLOCK), ROW_BLOCK)

    def x_copy(j, r0=row0):
        return pltpu.make_async_copy(xs_ref.at[in_hbm(r0, j), :], xbuf.at[local(j), :], in_sem.at[j])

    def y_copy(j):
        return pltpu.make_async_copy(acc.at[local(j), :], ys_ref.at[in_hbm(row0, j), :], out_sem.at[j])

    def prefetch_next(j):
        @pl.when(j < nblk_next)
        def _():
            x_copy(j, row0_next).start()

    def body(chunks, phase):
        wg = wg_ref[...].astype(BF16)
        wl = wl_ref[...].astype(BF16)
        wd = wd_ref[...].astype(BF16)
        for j0, m in chunks:
            rows = pl.ds(pl.multiple_of(j0 * ROW_BLOCK, ROW_BLOCK), m * ROW_BLOCK)
            if phase == _FIRST:
                for j in range(m):
                    x_copy(j0 + j).wait()
            x = xbuf[rows, :].astype(BF16)
            glu = jnp.dot(x, wg, preferred_element_type=F32) + bg_ref[...]
            lin = jnp.dot(x, wl, preferred_element_type=F32) + bl_ref[...]
            glu = jnp.minimum(glu, SWIGLU_LIMIT)
            lin = jnp.clip(lin, -SWIGLU_LIMIT, SWIGLU_LIMIT)
            act = glu * jax.nn.sigmoid(SWIGLU_ALPHA * glu) * (lin + 1.0)
            y = jnp.dot(act.astype(BF16), wd, preferred_element_type=F32)
            if phase == _FIRST:
                acc[rows, :] = y + bd_ref[...]
            else:
                acc[rows, :] += y
            if phase == _LAST:
                for j in range(m):
                    y_copy(j0 + j).start()

    def run(phase):
        def after(j0, m):
            if phase == _LAST:
                for j in range(m):
                    prefetch_next(j0 + j)

        def quad(q, carry):
            body(((4 * q, 2), (4 * q + 2, 2)), phase)
            after(4 * q, 4)
            return carry

        n_quads = lax.shift_right_logical(nblk, 2)
        lax.fori_loop(0, n_quads, quad, 0)
        done = lax.shift_left(n_quads, 2)

        @pl.when(jnp.bitwise_and(nblk, 2) == 2)
        def _():
            body(((done, 1), (done + 1, 1)), phase)
            after(done, 2)

        @pl.when(jnp.bitwise_and(nblk, 1) == 1)
        def _():
            body(((nblk - 1, 1),), phase)
            after(nblk - 1, 1)

    @pl.when(f == 0)
    def _():
        def start(j, carry):
            x_copy(j).start()
            return carry
        lax.fori_loop(n_started, nblk, start, 0)
        run(_FIRST)

    @pl.when((f > 0) & (f < last_f))
    def _():
        run(_MIDDLE)

    @pl.when(f == last_f)
    def _():
        run(_LAST)

        def drain(j, carry):
            y_copy(j).wait()
            return carry
        lax.fori_loop(0, nblk, drain, 0)

        @pl.when(i == n_items - 1)
        def _():
            first_unused = total_ref[0] // ROW_BLOCK
            n_blocks = ys_ref.shape[0] // ROW_BLOCK
            acc[local(0), :] = jnp.zeros((ROW_BLOCK, D_MODEL), F32)

            def zero_copy(b):
                r0 = pl.multiple_of(b * ROW_BLOCK, ROW_BLOCK)
                return pltpu.make_async_copy(acc.at[local(0), :], ys_ref.at[pl.ds(r0, ROW_BLOCK), :], out_sem.at[0])

            def zero_block(b, carry):
                zero_copy(b).start()
                return carry

            def zero_wait(b, carry):
                zero_copy(b).wait()
                return carry

            lax.fori_loop(first_unused, n_blocks, zero_block, 0)
            lax.fori_loop(first_unused, n_blocks, zero_wait, 0)


def _experts(items, total_rows, xs, w_up, b_up, w_down, b_down):
    row0, nblk, exp = items
    n_items = row0.shape[0]
    n_rows = xs.shape[0]
    n_f = D_FF // FF_TILE
    assert n_f >= 2

    def ff(i, f, nb):
        return jnp.where(nb[i] > 0, f, n_f - 1)

    max_rows = EXPERT_MAX_BLOCKS * ROW_BLOCK
    grid_spec = pltpu.PrefetchScalarGridSpec(
        num_scalar_prefetch=4,
        grid=(n_items, n_f),
        in_specs=[
            pl.BlockSpec(memory_space=pl.ANY),
            pl.BlockSpec((None, D_MODEL, FF_TILE), lambda i, f, r0, nb, ex, tot: (ex[i], 0, ff(i, f, nb))),
            pl.BlockSpec((None, D_MODEL, FF_TILE), lambda i, f, r0, nb, ex, tot: (ex[i], 0, n_f + ff(i, f, nb))),
            pl.BlockSpec((None, FF_TILE, D_MODEL), lambda i, f, r0, nb, ex, tot: (ex[i], ff(i, f, nb), 0)),
            pl.BlockSpec((None, 1, FF_TILE), lambda i, f, r0, nb, ex, tot: (ex[i], 0, ff(i, f, nb))),
            pl.BlockSpec((None, 1, FF_TILE), lambda i, f, r0, nb, ex, tot: (ex[i], 0, n_f + ff(i, f, nb))),
            pl.BlockSpec((None, 1, D_MODEL), lambda i, f, r0, nb, ex, tot: (ex[i], 0, 0)),
        ],
        out_specs=pl.BlockSpec(memory_space=pl.ANY),
        scratch_shapes=[pltpu.VMEM((max_rows, D_MODEL), F32), pltpu.VMEM((max_rows, D_MODEL), F32),
                        pltpu.SemaphoreType.DMA((EXPERT_MAX_BLOCKS,)), pltpu.SemaphoreType.DMA((EXPERT_MAX_BLOCKS,))],
    )
    return pl.pallas_call(
        _experts_kernel,
        grid_spec=grid_spec,
        out_shape=jax.ShapeDtypeStruct((n_rows, D_MODEL), F32),
        compiler_params=_params(2),
        name="experts",
    )(row0, nblk, exp, total_rows.reshape(1), xs, w_up, w_up, w_down,
      b_up.reshape(N_EXPERTS, 1, 2 * D_FF), b_up.reshape(N_EXPERTS, 1, 2 * D_FF), b_down.reshape(N_EXPERTS, 1, D_MODEL))


def _combine_kernel(n_prompt_tiles, dcur_ref, dnxt_ref, gates_ref, x1p_ref, x1s_ref, g2p_ref, g2s_ref, lng_ref, lnb_ref,
                    ys_ref, yp_ref, ysm_ref, buf, sem):
    i = pl.program_id(0)
    n = pl.num_programs(0)
    slot = i % 2

    def gather(idx_ref, s):
        for j in range(TOP_K * TOKEN_TILE):
            d = idx_ref[0, 0, j]
            pltpu.make_async_copy(ys_ref.at[pl.ds(d, 1), :],
                                  buf.at[s, j // TOKEN_TILE, pl.ds(j % TOKEN_TILE, 1), :], sem.at[s]).start()

    @pl.when(i == 0)
    def _():
        gather(dcur_ref, 0)

    for s in range(2):
        @pl.when((i + 1 < n) & (slot == 1 - s))
        def _():
            gather(dnxt_ref, s)

    for k in range(TOP_K):
        pltpu.make_async_copy(ys_ref.at[pl.ds(0, TOKEN_TILE), :], buf.at[slot, k], sem.at[slot]).wait()

    gates = gates_ref[...]
    ffn = buf[slot, 0] * gates[:, 0:1]
    for k in range(1, TOP_K):
        ffn = ffn + buf[slot, k] * gates[:, k:k + 1]

    def finish(x1, gate2, out_ref):
        out_ref[...] = _standardise(DEEPNORM_ALPHA * x1 + gate2 * ffn) * lng_ref[...] + lnb_ref[...]

    @pl.when(i < n_prompt_tiles)
    def _():
        finish(x1p_ref[...], g2p_ref[...], yp_ref)

    @pl.when(i >= n_prompt_tiles)
    def _():
        finish(x1s_ref[...], g2s_ref[...], ysm_ref)


def _combine(dest_tiles, gates_tok, x1_p, x1_s, gate2_p, gate2_s, ln2_g, ln2_b, ys):
    n_p = x1_p.shape[0] // TOKEN_TILE
    n_s = x1_s.shape[0] // TOKEN_TILE
    n = n_p + n_s

    def p_idx(i):
        return jnp.minimum(i, n_p - 1)

    def s_idx(i):
        return jnp.maximum(i - n_p, 0)

    smem_tile = (1, 1, TOP_K * TOKEN_TILE)
    return pl.pallas_call(
        functools.partial(_combine_kernel, n_p),
        grid=(n,),
        in_specs=[
            pl.BlockSpec(smem_tile, lambda i: (i, 0, 0), memory_space=pltpu.SMEM),
            pl.BlockSpec(smem_tile, lambda i: (jnp.minimum(i + 1, n - 1), 0, 0), memory_space=pltpu.SMEM),
            pl.BlockSpec((TOKEN_TILE, TOP_K), lambda i: (i, 0)),
            pl.BlockSpec((TOKEN_TILE, D_MODEL), lambda i: (p_idx(i), 0)),
            pl.BlockSpec((TOKEN_TILE, D_MODEL), lambda i: (s_idx(i), 0)),
            pl.BlockSpec((1, D_MODEL), lambda i: (0, 0)),
            pl.BlockSpec((TOKEN_TILE, D_MODEL), lambda i: (s_idx(i), 0)),
            pl.BlockSpec((1, D_MODEL), lambda i: (0, 0)),
            pl.BlockSpec((1, D_MODEL), lambda i: (0, 0)),
            pl.BlockSpec(memory_space=pl.ANY),
        ],
        out_specs=[pl.BlockSpec((TOKEN_TILE, D_MODEL), lambda i: (p_idx(i), 0)),
                   pl.BlockSpec((TOKEN_TILE, D_MODEL), lambda i: (s_idx(i), 0))],
        out_shape=[jax.ShapeDtypeStruct(x1_p.shape, F32), jax.ShapeDtypeStruct(x1_s.shape, F32)],
        scratch_shapes=[pltpu.VMEM((2, TOP_K, TOKEN_TILE, D_MODEL), F32), pltpu.SemaphoreType.DMA((2,))],
        compiler_params=_params(1),
        name="combine",
    )(dest_tiles, dest_tiles, gates_tok, x1_p, x1_s, gate2_p, gate2_s,
      ln2_g.reshape(1, D_MODEL), ln2_b.reshape(1, D_MODEL), ys)


def _rope_tables(pos):
    half = HEAD_DIM // 2
    inv_freq = ROPE_THETA ** (-jnp.arange(half, dtype=F32) / half)
    ang = pos.astype(F32)[:, None] * inv_freq
    cos, sin = jnp.cos(ang), jnp.sin(ang)
    reps = LANES // HEAD_DIM
    return (jnp.tile(jnp.concatenate([cos, cos], axis=1), (1, reps)),
            jnp.tile(jnp.concatenate([-sin, sin], axis=1), (1, reps)))


def _work_items(group_start, group_size, n_items):
    chunk_rows = EXPERT_MAX_BLOCKS * ROW_BLOCK
    ex = jnp.arange(N_EXPERTS, dtype=jnp.int32)
    n_chunks = (group_size + chunk_rows - 1) // chunk_rows
    chunk_end = jnp.sum(jnp.where(ex[None, :] <= ex[:, None], n_chunks[None, :], 0), axis=1)
    chunk_start = chunk_end - n_chunks
    item = jnp.arange(n_items, dtype=jnp.int32)
    live = item < chunk_end[-1]
    it = jnp.minimum(item, chunk_end[-1] - 1)
    exp = jnp.minimum(jnp.sum((chunk_end[None, :] <= it[:, None]).astype(jnp.int32), axis=1), N_EXPERTS - 1)
    own = exp[:, None] == ex[None, :]

    def pick(per_expert):
        return jnp.sum(jnp.where(own, per_expert[None, :], 0), axis=1)

    chunk = it - pick(chunk_start)
    blocks = pick(group_size) // ROW_BLOCK
    per_chunk = (blocks + jnp.maximum(pick(n_chunks), 1) - 1) // jnp.maximum(pick(n_chunks), 1)
    row0 = pick(group_start) + chunk * per_chunk * ROW_BLOCK
    nblk = jnp.clip(blocks - chunk * per_chunk, 0, per_chunk)
    nblk = jnp.where(live, nblk, 0)
    return row0.astype(jnp.int32), nblk.astype(jnp.int32), exp.astype(jnp.int32)


def _moe(u2_p, u2_s, logits_t, x1_p, x1_s, gate2_p, gate2_s, ln2_g, ln2_b, w_up, b_up, w_down, b_down):
    n_tok = logits_t.shape[1]
    eidx, gates, rank, counts = _route(logits_t)
    count = counts[:, 0]
    group_size = (count + ROW_BLOCK - 1) // ROW_BLOCK * ROW_BLOCK
    ex = jnp.arange(N_EXPERTS, dtype=jnp.int32)
    group_end = jnp.sum(jnp.where(ex[None, :] <= ex[:, None], group_size[None, :], 0), axis=1)
    group_start = group_end - group_size
    dest = _dest(group_start, eidx, rank)
    n_tiles = n_tok // TOKEN_TILE
    dest_tiles = dest.reshape(TOP_K, n_tiles, TOKEN_TILE).transpose(1, 0, 2).reshape(n_tiles, 1, TOP_K * TOKEN_TILE)
    max_rows = n_tok * TOP_K + N_EXPERTS * (ROW_BLOCK - 1)
    n_rows = (max_rows + ROW_BLOCK - 1) // ROW_BLOCK * ROW_BLOCK
    n_items = N_EXPERTS + max_rows // (EXPERT_MAX_BLOCKS * ROW_BLOCK)
    xs = _dispatch(group_end, group_size, dest_tiles, u2_p, u2_s, n_rows)
    ys = _experts(_work_items(group_start, group_size, n_items), group_end[-1], xs, w_up, b_up, w_down, b_down)
    return _combine(dest_tiles, gates.T, x1_p, x1_s, gate2_p, gate2_s, ln2_g, ln2_b, ys)


def _layer(x_p, x_s, cache_k, cache_v, state_conv, c_p, c_s,
           w_ada, b_ada, w_in, b_in, conv_w, sinks, norm_attn_g, norm_conv_g, w_out, b_out,
           ln1_g, ln1_b, w_router, b_router, w_up, b_up, w_down, b_down, ln2_g, ln2_b, past_len):
    T = x_p.shape[0]
    B = x_s.shape[0]
    n_c = 1 + B
    pad_c = (-n_c) % SUBLANES
    c_all = jnp.concatenate([c_p, c_s, jnp.zeros((pad_c, D_MODEL), F32)], axis=0)
    mod = _ada(c_all, w_ada, b_ada)
    shift1, scale1, gate1, shift2, scale2, gate2 = [mod[:, j * D_MODEL:(j + 1) * D_MODEL] for j in range(6)]

    def prompt(a):
        return a[0:1]

    def sample(a):
        return a[1:n_c]

    w_in_bf = w_in.astype(BF16)
    w_out_bf = w_out.astype(BF16)
    w_router_t_bf = w_router.T.astype(BF16)

    cos_p, sin_p = _rope_tables(jnp.arange(T, dtype=jnp.int32))
    q_p, k_p, v_p, gb_p, uc_p = _inproj(x_p, prompt(shift1), prompt(scale1), cos_p, sin_p, w_in_bf, b_in,
                                        min(INPROJ_TILE_M, T))
    attn_p = _attn_prompt(q_p, k_p, v_p, sinks, norm_attn_g)
    x1_p, u2_p, lg_p = _mix(attn_p, gb_p, uc_p, jnp.zeros((CONV_WIDTH - 1, CONV_DIM), F32), conv_w, norm_conv_g,
                            w_out_bf, b_out, x_p, prompt(gate1), ln1_g, ln1_b, prompt(shift2), prompt(scale2),
                            w_router_t_bf, b_router, min(MIX_TILE_M, T), True)

    cos_s, sin_s = _rope_tables(jnp.full((1,), past_len, jnp.int32))
    q_s, k_s, v_s, gb_s, uc_s = _inproj(x_s, sample(shift1), sample(scale1), cos_s, sin_s, w_in_bf, b_in, B)
    attn_s, new_k_s, new_v_s = _attn_decode(q_s, k_s, v_s, cache_k, cache_v, sinks, norm_attn_g)
    x1_s, u2_s, lg_s = _mix(attn_s.reshape(B, ATTN_DIM), gb_s, uc_s, (state_conv[:, 1], state_conv[:, 0]), conv_w,
                            norm_conv_g, w_out_bf, b_out, x_s, sample(gate1), ln1_g, ln1_b, sample(shift2),
                            sample(scale2), w_router_t_bf, b_router, B, False)

    y_p, y_s = _moe(u2_p, u2_s, jnp.concatenate([lg_p, lg_s], axis=1), x1_p, x1_s, prompt(gate2), sample(gate2),
                    ln2_g, ln2_b, w_up, b_up, w_down, b_down)

    new_k_p = k_p[T - WINDOW:].reshape(WINDOW, N_KV_HEADS, HEAD_DIM)
    new_v_p = v_p[T - WINDOW:].reshape(WINDOW, N_KV_HEADS, HEAD_DIM)
    new_conv_p = uc_p[T - (CONV_WIDTH - 1):]
    new_conv_s = jnp.stack([state_conv[:, 1], uc_s], axis=1)
    return (y_p, y_s, new_k_p, new_v_p, new_conv_p,
            new_k_s.reshape(B, WINDOW, N_KV_HEADS, HEAD_DIM), new_v_s.reshape(B, WINDOW, N_KV_HEADS, HEAD_DIM), new_conv_s)


def kernel(x_prompt, x_sample, cache_k, cache_v, state_conv, c_prompt, c_sample, w_ada, b_ada, w_in, b_in, conv_w, sinks, norm_attn_g, norm_conv_g, w_out, b_out, ln1_g, ln1_b, w_router, b_router, w_up, b_up, w_down, b_down, ln2_g, ln2_b):
    assert x_prompt.shape[0] == 1 and x_sample.shape[1] == 1 and w_ada.shape[0] == DEPTH == 1
    B = x_sample.shape[0]
    (y_p, y_s, nk_p, nv_p, nc_p, nk_s, nv_s, nc_s) = _layer(
        x_prompt[0], x_sample[:, 0], cache_k[0], cache_v[0], state_conv[0], c_prompt, c_sample,
        w_ada[0], b_ada[0], w_in[0], b_in[0], conv_w[0], sinks[0], norm_attn_g[0], norm_conv_g[0], w_out[0], b_out[0],
        ln1_g[0], ln1_b[0], w_router[0], b_router[0], w_up[0], b_up[0], w_down[0], b_down[0], ln2_g[0], ln2_b[0],
        PAST_LEN)
    return (y_p[None], y_s.reshape(B, 1, D_MODEL), nk_p[None, None], nv_p[None, None], nc_p[None, None],
            nk_s[None], nv_s[None], nc_s[None])
```

```python
import functools

import jax
import jax.numpy as jnp
from jax import lax
from jax.experimental import pallas as pl
from jax.experimental.pallas import tpu as pltpu

F32 = jnp.float32
BF16 = jnp.bfloat16

D_MODEL = 2048
HEAD_DIM = 64
N_HEADS = 16
N_KV_HEADS = 4
GQA_GROUP = N_HEADS // N_KV_HEADS
ATTN_DIM = N_HEADS * HEAD_DIM
KV_DIM = N_KV_HEADS * HEAD_DIM
CONV_DIM = D_MODEL - ATTN_DIM
CONV_WIDTH = 3
IN_DIM = ATTN_DIM + 2 * KV_DIM + 3 * CONV_DIM
WINDOW = 128
PAST_LEN = 16384
ROPE_THETA = 10000.0
N_EXPERTS = 32
TOP_K = 4
D_FF = D_MODEL
SWIGLU_LIMIT = 7.0
SWIGLU_ALPHA = 1.702
DEPTH = 1
DEEPNORM_ALPHA = (2.0 * DEPTH) ** 0.25
LN_EPS = 1e-5
RMS_EPS = 1e-6
COL_K = ATTN_DIM
COL_V = COL_K + KV_DIM
COL_B = COL_V + KV_DIM
COL_C = COL_B + CONV_DIM
COL_X = COL_C + CONV_DIM

LANES = 128
SUBLANES = 8
VMEM_LIMIT_BYTES = 60 * 1024 * 1024

TOKEN_TILE = 128
DISPATCH_SLOTS = 3
ADA_TILE_N = 1024
INPROJ_TILE_M = 512
INPROJ_CHUNK_N = 512
ATTN_BLOCKS = 4
MIX_TILE_M = 512
MIX_CHAIN_M = 256
DEC_TILE_B = 16
ROW_BLOCK = 256
EXPERT_MAX_BLOCKS = 6
FF_TILE = 512
NEG_BIG = -1e30


def _params(n_axes, vmem=VMEM_LIMIT_BYTES):
    return pltpu.CompilerParams(dimension_semantics=("arbitrary",) * n_axes, vmem_limit_bytes=vmem)


def _standardise(x):
    mu = jnp.mean(x, axis=-1, keepdims=True)
    xc = x - mu
    var = jnp.mean(xc * xc, axis=-1, keepdims=True)
    return xc * lax.rsqrt(var + LN_EPS)


def _rms(x, g):
    return x * lax.rsqrt(jnp.mean(x * x, axis=-1, keepdims=True) + RMS_EPS) * g


def _ada_kernel(c_ref, w_ref, b_ref, o_ref):
    c = c_ref[...]
    s = (c * jax.nn.sigmoid(c)).astype(BF16)
    o_ref[...] = jnp.dot(s, w_ref[...].astype(BF16), preferred_element_type=F32) + b_ref[...]


def _ada(c_all, w_ada, b_ada):
    rows = c_all.shape[0]
    n_out = w_ada.shape[1]
    return pl.pallas_call(
        _ada_kernel,
        grid=(n_out // ADA_TILE_N,),
        in_specs=[
            pl.BlockSpec((rows, D_MODEL), lambda j: (0, 0)),
            pl.BlockSpec((D_MODEL, ADA_TILE_N), lambda j: (0, j)),
            pl.BlockSpec((1, ADA_TILE_N), lambda j: (0, j)),
        ],
        out_specs=pl.BlockSpec((rows, ADA_TILE_N), lambda j: (0, j)),
        out_shape=jax.ShapeDtypeStruct((rows, n_out), F32),
        compiler_params=_params(1),
        name="ada",
    )(c_all, w_ada, b_ada.reshape(1, n_out))


def _inproj_kernel(x_ref, shift_ref, scale_ref, cos_ref, sin_ref, w_ref, b_ref,
                   q_ref, k_ref, v_ref, gb_ref, uc_ref):
    u = (_standardise(x_ref[...]) * (1.0 + scale_ref[...]) + shift_ref[...]).astype(BF16)
    cos = cos_ref[...]
    sin = sin_ref[...]
    lane = lax.broadcasted_iota(jnp.int32, (1, LANES), 1)
    first_half = (lane % HEAD_DIM) < (HEAD_DIM // 2)

    def rope(z):
        partner = jnp.where(first_half, pltpu.roll(z, LANES - HEAD_DIM // 2, axis=1),
                            pltpu.roll(z, HEAD_DIM // 2, axis=1))
        return z * cos + partner * sin

    def proj(c0):
        w = w_ref[:, c0:c0 + INPROJ_CHUNK_N]
        return jnp.dot(u, w, preferred_element_type=F32) + b_ref[:, c0:c0 + INPROJ_CHUNK_N]

    groups = INPROJ_CHUNK_N // LANES
    for j in range(ATTN_DIM // INPROJ_CHUNK_N):
        z = proj(j * INPROJ_CHUNK_N)
        for g in range(groups):
            c0 = j * INPROJ_CHUNK_N + g * LANES
            q_ref[:, c0:c0 + LANES] = (rope(z[:, g * LANES:(g + 1) * LANES]) * (HEAD_DIM ** -0.5)).astype(BF16)
    z = proj(COL_K)
    for g in range(KV_DIM // LANES):
        k_ref[:, g * LANES:(g + 1) * LANES] = rope(z[:, g * LANES:(g + 1) * LANES])
    v_ref[...] = z[:, KV_DIM:2 * KV_DIM]
    for j in range(CONV_DIM // INPROJ_CHUNK_N):
        sl = slice(j * INPROJ_CHUNK_N, (j + 1) * INPROJ_CHUNK_N)
        gb_ref[:, sl] = proj(COL_B + j * INPROJ_CHUNK_N)
        uc_ref[:, sl] = proj(COL_C + j * INPROJ_CHUNK_N) * proj(COL_X + j * INPROJ_CHUNK_N)


def _inproj(x, shift, scale, cos, sin, w_in_bf, b_in, tm):
    T = x.shape[0]
    per_row_mod = shift.shape[0] != 1
    per_row_pos = cos.shape[0] != 1
    mod_spec = pl.BlockSpec((tm, D_MODEL), lambda i: (i, 0)) if per_row_mod else pl.BlockSpec((1, D_MODEL), lambda i: (0, 0))
    pos_spec = pl.BlockSpec((tm, LANES), lambda i: (i, 0)) if per_row_pos else pl.BlockSpec((1, LANES), lambda i: (0, 0))

    def row_spec(width):
        return pl.BlockSpec((tm, width), lambda i: (i, 0))

    return pl.pallas_call(
        _inproj_kernel,
        grid=(T // tm,),
        in_specs=[
            row_spec(D_MODEL), mod_spec, mod_spec, pos_spec, pos_spec,
            pl.BlockSpec((D_MODEL, IN_DIM), lambda i: (0, 0), pipeline_mode=pl.Buffered(1)),
            pl.BlockSpec((1, IN_DIM), lambda i: (0, 0)),
        ],
        out_specs=[row_spec(ATTN_DIM), row_spec(KV_DIM), row_spec(KV_DIM), row_spec(CONV_DIM), row_spec(CONV_DIM)],
        out_shape=[
            jax.ShapeDtypeStruct((T, ATTN_DIM), BF16),
            jax.ShapeDtypeStruct((T, KV_DIM), F32),
            jax.ShapeDtypeStruct((T, KV_DIM), F32),
            jax.ShapeDtypeStruct((T, CONV_DIM), F32),
            jax.ShapeDtypeStruct((T, CONV_DIM), F32),
        ],
        compiler_params=_params(1),
        name="inproj",
    )(x, shift, scale, cos, sin, w_in_bf, b_in.reshape(1, IN_DIM))


def _attn_kernel(q_ref, kp_ref, kc_ref, vp_ref, vc_ref, bias_ref, sink_ref, g_ref, o_ref):
    n = pl.program_id(0)
    n_blocks = q_ref.shape[0] // WINDOW
    k = jnp.concatenate([kp_ref[...], kc_ref[...]], axis=0).astype(BF16)
    v = jnp.concatenate([vp_ref[...], vc_ref[...]], axis=0).astype(BF16)
    for b in range(n_blocks):
        q = q_ref[b * WINDOW:(b + 1) * WINDOW, :]
        kb = k[b * WINDOW:(b + 2) * WINDOW, :]
        vb = v[b * WINDOW:(b + 2) * WINDOW, :]
        bias = bias_ref[jnp.minimum(n, 1)] if b == 0 else bias_ref[1]
        outs = []
        for g in range(N_KV_HEADS):
            heads = range(g * GQA_GROUP, (g + 1) * GQA_GROUP)
            qg = jnp.concatenate([q[:, h * HEAD_DIM:(h + 1) * HEAD_DIM] for h in heads], axis=0)
            kg = kb[:, g * HEAD_DIM:(g + 1) * HEAD_DIM]
            vg = vb[:, g * HEAD_DIM:(g + 1) * HEAD_DIM]
            sink = sink_ref[g]
            s = lax.dot_general(kg, qg, (((1,), (1,)), ((), ())), preferred_element_type=F32) + bias
            m = jnp.maximum(jnp.max(s, axis=0, keepdims=True), sink)
            e = jnp.exp(s - m)
            den = jnp.sum(e, axis=0, keepdims=True) + jnp.exp(sink - m)
            o = lax.dot_general(vg, e.astype(BF16), (((0,), (0,)), ((), ())), preferred_element_type=F32)
            o = o / den
            outs.extend(o[:, j * WINDOW:(j + 1) * WINDOW] for j in range(GQA_GROUP))
        o = jnp.concatenate(outs, axis=0).T
        o_ref[b * WINDOW:(b + 1) * WINDOW, :] = _rms(o, g_ref[...]).astype(BF16)


def _attn_mask_bias():
    qi = jnp.arange(GQA_GROUP * WINDOW, dtype=jnp.int32)[None, :] % WINDOW
    ks = jnp.arange(2 * WINDOW, dtype=jnp.int32)[:, None]
    later = (ks >= qi) & (ks <= qi + WINDOW)
    first = later & (ks >= WINDOW)
    return jnp.where(jnp.stack([first, later]), 0.0, NEG_BIG).astype(F32)


def _attn_prompt(q, k, v, sinks, norm_g):
    T = q.shape[0]
    nb = ATTN_BLOCKS if T % (ATTN_BLOCKS * WINDOW) == 0 else 1

    def cur(width):
        return pl.BlockSpec((nb * WINDOW, width), lambda n: (n, 0))

    def prev(width):
        return pl.BlockSpec((WINDOW, width), lambda n: (jnp.maximum(n * nb - 1, 0), 0))

    return pl.pallas_call(
        _attn_kernel,
        grid=(T // (nb * WINDOW),),
        in_specs=[
            cur(ATTN_DIM), prev(KV_DIM), cur(KV_DIM), prev(KV_DIM), cur(KV_DIM),
            pl.BlockSpec((2, 2 * WINDOW, GQA_GROUP * WINDOW), lambda n: (0, 0, 0)),
            pl.BlockSpec((N_KV_HEADS, 1, GQA_GROUP * WINDOW), lambda n: (0, 0, 0)),
            pl.BlockSpec((1, ATTN_DIM), lambda n: (0, 0)),
        ],
        out_specs=cur(ATTN_DIM),
        out_shape=jax.ShapeDtypeStruct((T, ATTN_DIM), BF16),
        compiler_params=_params(1),
        name="attn",
    )(q, k, k, v, v, _attn_mask_bias(),
      jnp.repeat(sinks, WINDOW).reshape(N_KV_HEADS, 1, GQA_GROUP * WINDOW),
      norm_g.reshape(1, ATTN_DIM))


def _attn_dec_kernel(q_ref, kn_ref, vn_ref, ck_ref, cv_ref, sink_ref, g_ref, o_ref, nk_ref, nv_ref):
    tb = q_ref.shape[0]
    q = q_ref[...].astype(F32)
    lane_group = lax.broadcasted_iota(jnp.int32, (1, N_HEADS, KV_DIM), 2) // HEAD_DIM
    head_group = lax.broadcasted_iota(jnp.int32, (1, N_HEADS, KV_DIM), 1) // GQA_GROUP
    own = lane_group == head_group
    qe = jnp.where(own, jnp.concatenate([q] * N_KV_HEADS, axis=2), 0.0)
    ck = ck_ref[...]
    cv = cv_ref[...]
    kn = kn_ref[...]
    vn = vn_ref[...]
    s = jnp.einsum("bhc,bwc->bhw", qe.astype(BF16), ck.astype(BF16), preferred_element_type=F32)
    s_new = jnp.sum(qe.astype(BF16).astype(F32) * kn.astype(BF16).astype(F32), axis=2, keepdims=True)
    sink = sink_ref[...]
    m = jnp.maximum(jnp.maximum(jnp.max(s, axis=2, keepdims=True), s_new), sink)
    e = jnp.exp(s - m)
    e_new = jnp.exp(s_new - m)
    den = jnp.sum(e, axis=2, keepdims=True) + e_new + jnp.exp(sink - m)
    p = (e / den).astype(BF16)
    p_new = (e_new / den).astype(BF16).astype(F32)
    o = jnp.einsum("bhw,bwc->bhc", p, cv.astype(BF16), preferred_element_type=F32)
    o = o + p_new * vn.astype(BF16).astype(F32)
    o = jnp.where(own, o, 0.0)
    oh = o[:, :, 0:HEAD_DIM]
    for g in range(1, N_KV_HEADS):
        oh = oh + o[:, :, g * HEAD_DIM:(g + 1) * HEAD_DIM]
    ms = jnp.sum(jnp.sum(oh * oh, axis=2, keepdims=True), axis=1, keepdims=True) / ATTN_DIM
    o_ref[...] = (oh * lax.rsqrt(ms + RMS_EPS) * g_ref[...]).astype(BF16)
    row = lax.broadcasted_iota(jnp.int32, (1, WINDOW, 1), 1)
    nk_ref[...] = jnp.where(row == WINDOW - 1, kn, pltpu.roll(ck, WINDOW - 1, axis=1))
    nv_ref[...] = jnp.where(row == WINDOW - 1, vn, pltpu.roll(cv, WINDOW - 1, axis=1))


def _attn_decode(q, k_new, v_new, cache_k, cache_v, sinks, norm_g):
    B = q.shape[0]
    tb = DEC_TILE_B

    def b3(d1, d2):
        return pl.BlockSpec((tb, d1, d2), lambda i: (i, 0, 0))

    return pl.pallas_call(
        _attn_dec_kernel,
        grid=(B // tb,),
        in_specs=[
            b3(N_HEADS, HEAD_DIM), b3(1, KV_DIM), b3(1, KV_DIM), b3(WINDOW, KV_DIM), b3(WINDOW, KV_DIM),
            pl.BlockSpec((1, N_HEADS, 1), lambda i: (0, 0, 0)),
            pl.BlockSpec((1, N_HEADS, HEAD_DIM), lambda i: (0, 0, 0)),
        ],
        out_specs=[b3(N_HEADS, HEAD_DIM), b3(WINDOW, KV_DIM), b3(WINDOW, KV_DIM)],
        out_shape=[
            jax.ShapeDtypeStruct((B, N_HEADS, HEAD_DIM), BF16),
            jax.ShapeDtypeStruct((B, WINDOW, KV_DIM), F32),
            jax.ShapeDtypeStruct((B, WINDOW, KV_DIM), F32),
        ],
        compiler_params=_params(1),
        name="attn_dec",
    )(q.reshape(B, N_HEADS, HEAD_DIM), k_new.reshape(B, 1, KV_DIM), v_new.reshape(B, 1, KV_DIM),
      cache_k.reshape(B, WINDOW, KV_DIM), cache_v.reshape(B, WINDOW, KV_DIM),
      sinks.reshape(1, N_HEADS, 1), norm_g.reshape(1, N_HEADS, HEAD_DIM))


def _mix_tail(attn_ref, gb_ref, uc, um1, um2, cw_ref, gconv_ref, wout_ref, bout_ref, x_ref, gate1_ref, ln1g_ref, ln1b_ref,
              shift2_ref, scale2_ref, wr_ref, br_ref, x1_ref, u2_ref, lg_ref):
    tm = uc.shape[0]
    n_split = max(tm // MIX_CHAIN_M, 1)
    h = tm // n_split
    cw = cw_ref[...]
    for s in range(n_split):
        r = slice(s * h, (s + 1) * h)

        def rows(ref):
            return ref[r, :] if ref.shape[0] == tm else ref[...]

        conv = cw[0:1, :] * um2[r] + cw[1:2, :] * um1[r] + cw[2:3, :] * uc[r]
        conv_n = _rms(gb_ref[r, :] * conv, gconv_ref[...]).astype(BF16)
        mixed = (jnp.dot(attn_ref[r, :], wout_ref[0:ATTN_DIM, :], preferred_element_type=F32)
                 + jnp.dot(conv_n, wout_ref[ATTN_DIM:D_MODEL, :], preferred_element_type=F32) + bout_ref[...])
        x1 = _standardise(DEEPNORM_ALPHA * x_ref[r, :] + rows(gate1_ref) * mixed) * ln1g_ref[...] + ln1b_ref[...]
        x1_ref[r, :] = x1
        u2 = _standardise(x1) * (1.0 + rows(scale2_ref)) + rows(shift2_ref)
        u2_ref[r, :] = u2
        lg_ref[:, r] = lax.dot_general(wr_ref[...], u2.astype(BF16), (((1,), (1,)), ((), ())),
                                       preferred_element_type=F32) + br_ref[...]


def _mix_seq_kernel(attn_ref, gb_ref, uc_ref, halo_ref, hist_ref, *rest):
    i = pl.program_id(0)
    uc = uc_ref[...]
    tm = uc.shape[0]
    above = jnp.where(i == 0, hist_ref[...], halo_ref[...])
    row = lax.broadcasted_iota(jnp.int32, (tm, 1), 0)
    um1 = jnp.where(row == 0, above[7:8, :], pltpu.roll(uc, 1, axis=0))
    um2 = jnp.where(row == 0, above[6:7, :], jnp.where(row == 1, above[7:8, :], pltpu.roll(uc, 2, axis=0)))
    _mix_tail(attn_ref, gb_ref, uc, um1, um2, *rest)


def _mix_tok_kernel(attn_ref, gb_ref, uc_ref, um1_ref, um2_ref, *rest):
    _mix_tail(attn_ref, gb_ref, uc_ref[...], um1_ref[...], um2_ref[...], *rest)


def _mix(attn_n, gb, uc, conv_prev, conv_w, norm_conv_g, w_out_bf, b_out, x, gate1, ln1_g, ln1_b, shift2, scale2,
         w_router_t_bf, b_router, tm, sequential):
    T = x.shape[0]
    per_row_mod = gate1.shape[0] != 1

    def row_spec(width):
        return pl.BlockSpec((tm, width), lambda i: (i, 0))

    def const_spec(rows, width):
        return pl.BlockSpec((rows, width), lambda i: (0, 0), pipeline_mode=pl.Buffered(1))

    mod_spec = row_spec(D_MODEL) if per_row_mod else const_spec(1, D_MODEL)
    if sequential:
        hist8 = jnp.concatenate([jnp.zeros((SUBLANES - 2, CONV_DIM), F32), conv_prev], axis=0)
        halo_blocks = tm // SUBLANES
        conv_specs = [pl.BlockSpec((SUBLANES, CONV_DIM), lambda i: (jnp.maximum(i * halo_blocks - 1, 0), 0)),
                      const_spec(SUBLANES, CONV_DIM)]
        conv_args = (uc, hist8)
        body = _mix_seq_kernel
    else:
        conv_specs = [row_spec(CONV_DIM), row_spec(CONV_DIM)]
        conv_args = conv_prev
        body = _mix_tok_kernel
    return pl.pallas_call(
        body,
        grid=(T // tm,),
        in_specs=[row_spec(ATTN_DIM), row_spec(CONV_DIM), row_spec(CONV_DIM), *conv_specs,
                  const_spec(CONV_WIDTH, CONV_DIM), const_spec(1, CONV_DIM),
                  const_spec(D_MODEL, D_MODEL), const_spec(1, D_MODEL),
                  row_spec(D_MODEL), mod_spec, const_spec(1, D_MODEL), const_spec(1, D_MODEL),
                  mod_spec, mod_spec,
                  const_spec(N_EXPERTS, D_MODEL), const_spec(N_EXPERTS, 1)],
        out_specs=[row_spec(D_MODEL), row_spec(D_MODEL), pl.BlockSpec((N_EXPERTS, tm), lambda i: (0, i))],
        out_shape=[jax.ShapeDtypeStruct((T, D_MODEL), F32), jax.ShapeDtypeStruct((T, D_MODEL), F32),
                   jax.ShapeDtypeStruct((N_EXPERTS, T), F32)],
        compiler_params=_params(1),
        name="mix_seq" if sequential else "mix_tok",
    )(attn_n, gb, uc, *conv_args, conv_w, norm_conv_g.reshape(1, CONV_DIM), w_out_bf, b_out.reshape(1, D_MODEL),
      x, gate1, ln1_g.reshape(1, D_MODEL), ln1_b.reshape(1, D_MODEL), shift2, scale2,
      w_router_t_bf, b_router.reshape(N_EXPERTS, 1))


def _route_kernel(lg_ref, eidx_ref, gate_ref, rank_ref, cnt_ref):
    n_tok = lg_ref.shape[1]
    r = lax.broadcasted_iota(jnp.int32, (TOKEN_TILE, TOKEN_TILE), 0)
    c = lax.broadcasted_iota(jnp.int32, (TOKEN_TILE, TOKEN_TILE), 1)
    earlier = (r < c).astype(BF16)
    expert = lax.broadcasted_iota(jnp.int32, (N_EXPERTS, TOKEN_TILE), 0).astype(F32)

    def body(ci, count):
        off = pl.multiple_of(ci * TOKEN_TILE, TOKEN_TILE)
        l = lg_ref[:, pl.ds(off, TOKEN_TILE)]
        vals, idxs, sels = [], [], []
        for _ in range(TOP_K):
            m = jnp.max(l, axis=0, keepdims=True)
            idx = jnp.min(jnp.where(l == m, expert, float(N_EXPERTS)), axis=0, keepdims=True)
            sel = expert == idx
            vals.append(m)
            idxs.append(idx)
            sels.append(sel)
            l = jnp.where(sel, -jnp.inf, l)
        chosen = jnp.where(sels[0] | sels[1] | sels[2] | sels[3], 1.0, 0.0)
        before = jnp.dot(chosen.astype(BF16), earlier, preferred_element_type=F32) + count
        ranks = [jnp.sum(jnp.where(s, before, 0.0), axis=0, keepdims=True) for s in sels]
        es = [jnp.exp(v - vals[0]) for v in vals]
        den = es[0] + es[1] + es[2] + es[3]
        eidx_ref[:, pl.ds(off, TOKEN_TILE)] = jnp.concatenate(idxs, axis=0).astype(jnp.int32)
        rank_ref[:, pl.ds(off, TOKEN_TILE)] = jnp.concatenate(ranks, axis=0).astype(jnp.int32)
        gate_ref[:, pl.ds(off, TOKEN_TILE)] = jnp.concatenate([e / den for e in es], axis=0)
        return count + jnp.sum(chosen, axis=1, keepdims=True)

    count = lax.fori_loop(0, n_tok // TOKEN_TILE, body, jnp.zeros((N_EXPERTS, 1), F32))
    cnt_ref[...] = jnp.broadcast_to(count, (N_EXPERTS, LANES)).astype(jnp.int32)


def _route(logits_t):
    n_tok = logits_t.shape[1]
    return pl.pallas_call(
        _route_kernel,
        out_shape=[jax.ShapeDtypeStruct((TOP_K, n_tok), jnp.int32), jax.ShapeDtypeStruct((TOP_K, n_tok), F32),
                   jax.ShapeDtypeStruct((TOP_K, n_tok), jnp.int32), jax.ShapeDtypeStruct((N_EXPERTS, LANES), jnp.int32)],
        compiler_params=pltpu.CompilerParams(vmem_limit_bytes=VMEM_LIMIT_BYTES),
        name="route",
    )(logits_t)


def _dest_kernel(start_ref, eidx_ref, rank_ref, dest_ref):
    e = eidx_ref[...]
    base = jnp.zeros(e.shape, jnp.int32)
    for x in range(N_EXPERTS):
        base = jnp.where(e == x, start_ref[x], base)
    dest_ref[...] = base + rank_ref[...]


def _dest(group_start, eidx, rank):
    return pl.pallas_call(
        _dest_kernel,
        in_specs=[pl.BlockSpec(memory_space=pltpu.SMEM), pl.BlockSpec(memory_space=pltpu.VMEM),
                  pl.BlockSpec(memory_space=pltpu.VMEM)],
        out_specs=pl.BlockSpec(memory_space=pltpu.VMEM),
        out_shape=jax.ShapeDtypeStruct(eidx.shape, jnp.int32),
        name="dest",
    )(group_start, eidx, rank)


def _dispatch_kernel(n_prompt_tiles, n_tiles, gend_ref, gsize_ref, dest_ref, up_ref, us_ref, xs_ref,
                     zero_buf, tiles, zsem, load_sem, scat_sem):
    i = pl.program_id(0)
    slot = lax.rem(i, DISPATCH_SLOTS)

    def start_load(tile, s):
        @pl.when(tile < n_prompt_tiles)
        def _():
            r = pl.multiple_of(tile * TOKEN_TILE, TOKEN_TILE)
            pltpu.make_async_copy(up_ref.at[pl.ds(r, TOKEN_TILE), :], tiles.at[s], load_sem.at[s]).start()

        @pl.when(tile >= n_prompt_tiles)
        def _():
            r = pl.multiple_of((tile - n_prompt_tiles) * TOKEN_TILE, TOKEN_TILE)
            pltpu.make_async_copy(us_ref.at[pl.ds(r, TOKEN_TILE), :], tiles.at[s], load_sem.at[s]).start()

    def wait_load(s):
        pltpu.make_async_copy(up_ref.at[pl.ds(0, TOKEN_TILE), :], tiles.at[s], load_sem.at[s]).wait()

    def wait_scatter(s):
        for _ in range(TOP_K):
            pltpu.make_async_copy(tiles.at[s], xs_ref.at[pl.ds(0, TOKEN_TILE), :], scat_sem.at[s]).wait()

    @pl.when(i == 0)
    def _():
        start_load(0, 0)
        zero_buf[...] = jnp.zeros_like(zero_buf)
        for e in range(N_EXPERTS):
            @pl.when(gsize_ref[e] > 0)
            def _():
                r0 = pl.multiple_of(gend_ref[e] - ROW_BLOCK, ROW_BLOCK)
                pltpu.make_async_copy(zero_buf, xs_ref.at[pl.ds(r0, ROW_BLOCK), :], zsem).start()
        for e in range(N_EXPERTS):
            @pl.when(gsize_ref[e] > 0)
            def _():
                pltpu.make_async_copy(zero_buf, xs_ref.at[pl.ds(0, ROW_BLOCK), :], zsem).wait()

        first_unused = gend_ref[N_EXPERTS - 1] // ROW_BLOCK
        n_blocks = xs_ref.shape[0] // ROW_BLOCK

        def zero_block(b, carry):
            r0 = pl.multiple_of(b * ROW_BLOCK, ROW_BLOCK)
            pltpu.make_async_copy(zero_buf, xs_ref.at[pl.ds(r0, ROW_BLOCK), :], zsem).start()
            return carry

        def zero_wait(b, carry):
            pltpu.make_async_copy(zero_buf, xs_ref.at[pl.ds(0, ROW_BLOCK), :], zsem).wait()
            return carry

        lax.fori_loop(first_unused, n_blocks, zero_block, 0)
        lax.fori_loop(first_unused, n_blocks, zero_wait, 0)

    @pl.when(i + 1 < n_tiles)
    def _():
        nxt = lax.rem(i + 1, DISPATCH_SLOTS)

        @pl.when(i + 1 >= DISPATCH_SLOTS)
        def _():
            wait_scatter(nxt)
        start_load(i + 1, nxt)

    wait_load(slot)
    for s in range(DISPATCH_SLOTS):
        @pl.when(slot == s)
        def _():
            for j in range(TOP_K * TOKEN_TILE):
                d = dest_ref[0, 0, j]
                pltpu.make_async_copy(tiles.at[s, pl.ds(j % TOKEN_TILE, 1), :], xs_ref.at[pl.ds(d, 1), :],
                                      scat_sem.at[s]).start(priority=j % 2)

    @pl.when(i == n_tiles - 1)
    def _():
        for tile in range(max(n_tiles - DISPATCH_SLOTS, 0), n_tiles):
            wait_scatter(tile % DISPATCH_SLOTS)


def _dispatch(group_end, group_size, dest_tiles, u2_p, u2_s, n_rows):
    n_p = u2_p.shape[0] // TOKEN_TILE
    n_s = u2_s.shape[0] // TOKEN_TILE
    grid_spec = pltpu.PrefetchScalarGridSpec(
        num_scalar_prefetch=2,
        grid=(n_p + n_s,),
        in_specs=[
            pl.BlockSpec((1, 1, TOP_K * TOKEN_TILE), lambda i, *_: (i, 0, 0), memory_space=pltpu.SMEM),
            pl.BlockSpec(memory_space=pl.ANY),
            pl.BlockSpec(memory_space=pl.ANY),
        ],
        out_specs=pl.BlockSpec(memory_space=pl.ANY),
        scratch_shapes=[pltpu.VMEM((ROW_BLOCK, D_MODEL), F32), pltpu.VMEM((DISPATCH_SLOTS, TOKEN_TILE, D_MODEL), F32),
                        pltpu.SemaphoreType.DMA(()), pltpu.SemaphoreType.DMA((DISPATCH_SLOTS,)),
                        pltpu.SemaphoreType.DMA((DISPATCH_SLOTS,))],
    )
    return pl.pallas_call(
        functools.partial(_dispatch_kernel, n_p, n_p + n_s),
        grid_spec=grid_spec,
        out_shape=jax.ShapeDtypeStruct((n_rows, D_MODEL), F32),
        compiler_params=_params(1),
        name="dispatch",
    )(group_end, group_size, dest_tiles, u2_p, u2_s)


_FIRST, _MIDDLE, _LAST = 0, 1, 2


def _experts_kernel(row0_ref, nblk_ref, exp_ref, total_ref, xs_ref, wg_ref, wl_ref, wd_ref, bg_ref, bl_ref, bd_ref,
                    ys_ref, xbuf, acc, in_sem, out_sem):
    i = pl.program_id(0)
    f = pl.program_id(1)
    n_items = pl.num_programs(0)
    last_f = pl.num_programs(1) - 1
    nblk = nblk_ref[i]
    row0 = row0_ref[i]
    n_started = jnp.minimum(jnp.where(i > 0, nblk_ref[jnp.maximum(i - 1, 0)], 0), nblk)
    nxt = jnp.minimum(i + 1, n_items - 1)
    row0_next = row0_ref[nxt]
    nblk_next = jnp.where(i + 1 < n_items, nblk_ref[nxt], 0)

    def local(j):
        return pl.ds(pl.multiple_of(j * ROW_BLOCK, ROW_BLOCK), ROW_BLOCK)

    def in_hbm(r0, j):
        return pl.ds(pl.multiple_of(r0 + j * ROW_BLOCK, ROW_BLOCK), ROW_BLOCK)

    def x_copy(j, r0=row0):
        return pltpu.make_async_copy(xs_ref.at[in_hbm(r0, j), :], xbuf.at[local(j), :], in_sem.at[j])

    def y_copy(j):
        return pltpu.make_async_copy(acc.at[local(j), :], ys_ref.at[in_hbm(row0, j), :], out_sem.at[j])

    def prefetch_next(j):
        @pl.when(j < nblk_next)
        def _():
            x_copy(j, row0_next).start()

    def body(chunks, phase):
        wg = wg_ref[...].astype(BF16)
        wl = wl_ref[...].astype(BF16)
        wd = wd_ref[...].astype(BF16)
        for j0, m in chunks:
            rows = pl.ds(pl.multiple_of(j0 * ROW_BLOCK, ROW_BLOCK), m * ROW_BLOCK)
            if phase == _FIRST:
                for j in range(m):
                    x_copy(j0 + j).wait()
            x = xbuf[rows, :].astype(BF16)
            glu = jnp.dot(x, wg, preferred_element_type=F32) + bg_ref[...]
            lin = jnp.dot(x, wl, preferred_element_type=F32) + bl_ref[...]
            glu = jnp.minimum(glu, SWIGLU_LIMIT)
            lin = jnp.clip(lin, -SWIGLU_LIMIT, SWIGLU_LIMIT)
            act = glu * jax.nn.sigmoid(SWIGLU_ALPHA * glu) * (lin + 1.0)
            y = jnp.dot(act.astype(BF16), wd, preferred_element_type=F32)
            if phase == _FIRST:
                acc[rows, :] = y + bd_ref[...]
            else:
                acc[rows, :] += y
            if phase == _LAST:
                for j in range(m):
                    y_copy(j0 + j).start()

    def run(phase):
        def after(j0, m):
            if phase == _LAST:
                for j in range(m):
                    prefetch_next(j0 + j)

        def quad(q, carry):
            body(((4 * q, 2), (4 * q + 2, 2)), phase)
            after(4 * q, 4)
            return carry

        n_quads = lax.shift_right_logical(nblk, 2)
        lax.fori_loop(0, n_quads, quad, 0)
        done = lax.shift_left(n_quads, 2)

        @pl.when(jnp.bitwise_and(nblk, 2) == 2)
        def _():
            body(((done, 1), (done + 1, 1)), phase)
            after(done, 2)

        @pl.when(jnp.bitwise_and(nblk, 1) == 1)
        def _():
            body(((nblk - 1, 1),), phase)
            after(nblk - 1, 1)

    @pl.when(f == 0)
    def _():
        def start(j, carry):
            x_copy(j).start()
            return carry
        lax.fori_loop(n_started, nblk, start, 0)
        run(_FIRST)

    @pl.when((f > 0) & (f < last_f))
    def _():
        run(_MIDDLE)

    @pl.when(f == last_f)
    def _():
        run(_LAST)

        def drain(j, carry):
            y_copy(j).wait()
            return carry
        lax.fori_loop(0, nblk, drain, 0)

        @pl.when(i == n_items - 1)
        def _():
            first_unused = total_ref[0] // ROW_BLOCK
            n_blocks = ys_ref.shape[0] // ROW_BLOCK
            acc[local(0), :] = jnp.zeros((ROW_BLOCK, D_MODEL), F32)

            def zero_copy(b):
                r0 = pl.multiple_of(b * ROW_BLOCK, ROW_BLOCK)
                return pltpu.make_async_copy(acc.at[local(0), :], ys_ref.at[pl.ds(r0, ROW_BLOCK), :], out_sem.at[0])

            def zero_block(b, carry):
                zero_copy(b).start()
                return carry

            def zero_wait(b, carry):
                zero_copy(b).wait()
                return carry

            lax.fori_loop(first_unused, n_blocks, zero_block, 0)
            lax.fori_loop(first_unused, n_blocks, zero_wait, 0)


def _experts(items, total_rows, xs, w_up, b_up, w_down, b_down):
    row0, nblk, exp = items
    n_items = row0.shape[0]
    n_rows = xs.shape[0]
    n_f = D_FF // FF_TILE
    assert n_f >= 2

    def ff(i, f, nb):
        return jnp.where(nb[i] > 0, f, n_f - 1)

    max_rows = EXPERT_MAX_BLOCKS * ROW_BLOCK
    grid_spec = pltpu.PrefetchScalarGridSpec(
        num_scalar_prefetch=4,
        grid=(n_items, n_f),
        in_specs=[
            pl.BlockSpec(memory_space=pl.ANY),
            pl.BlockSpec((None, D_MODEL, FF_TILE), lambda i, f, r0, nb, ex, tot: (ex[i], 0, ff(i, f, nb))),
            pl.BlockSpec((None, D_MODEL, FF_TILE), lambda i, f, r0, nb, ex, tot: (ex[i], 0, n_f + ff(i, f, nb))),
            pl.BlockSpec((None, FF_TILE, D_MODEL), lambda i, f, r0, nb, ex, tot: (ex[i], ff(i, f, nb), 0)),
            pl.BlockSpec((None, 1, FF_TILE), lambda i, f, r0, nb, ex, tot: (ex[i], 0, ff(i, f, nb))),
            pl.BlockSpec((None, 1, FF_TILE), lambda i, f, r0, nb, ex, tot: (ex[i], 0, n_f + ff(i, f, nb))),
            pl.BlockSpec((None, 1, D_MODEL), lambda i, f, r0, nb, ex, tot: (ex[i], 0, 0)),
        ],
        out_specs=pl.BlockSpec(memory_space=pl.ANY),
        scratch_shapes=[pltpu.VMEM((max_rows, D_MODEL), F32), pltpu.VMEM((max_rows, D_MODEL), F32),
                        pltpu.SemaphoreType.DMA((EXPERT_MAX_BLOCKS,)), pltpu.SemaphoreType.DMA((EXPERT_MAX_BLOCKS,))],
    )
    return pl.pallas_call(
        _experts_kernel,
        grid_spec=grid_spec,
        out_shape=jax.ShapeDtypeStruct((n_rows, D_MODEL), F32),
        compiler_params=_params(2),
        name="experts",
    )(row0, nblk, exp, total_rows.reshape(1), xs, w_up, w_up, w_down,
      b_up.reshape(N_EXPERTS, 1, 2 * D_FF), b_up.reshape(N_EXPERTS, 1, 2 * D_FF), b_down.reshape(N_EXPERTS, 1, D_MODEL))


def _combine_kernel(n_prompt_tiles, dcur_ref, dnxt_ref, gates_ref, x1p_ref, x1s_ref, g2p_ref, g2s_ref, lng_ref, lnb_ref,
                    ys_ref, yp_ref, ysm_ref, buf, sem):
    i = pl.program_id(0)
    n = pl.num_programs(0)
    slot = i % 2

    def gather(idx_ref, s):
        for j in range(TOP_K * TOKEN_TILE):
            d = idx_ref[0, 0, j]
            pltpu.make_async_copy(ys_ref.at[pl.ds(d, 1), :],
                                  buf.at[s, j // TOKEN_TILE, pl.ds(j % TOKEN_TILE, 1), :], sem.at[s]).start()

    @pl.when(i == 0)
    def _():
        gather(dcur_ref, 0)

    for s in range(2):
        @pl.when((i + 1 < n) & (slot == 1 - s))
        def _():
            gather(dnxt_ref, s)

    for k in range(TOP_K):
        pltpu.make_async_copy(ys_ref.at[pl.ds(0, TOKEN_TILE), :], buf.at[slot, k], sem.at[slot]).wait()

    gates = gates_ref[...]
    ffn = buf[slot, 0] * gates[:, 0:1]
    for k in range(1, TOP_K):
        ffn = ffn + buf[slot, k] * gates[:, k:k + 1]

    def finish(x1, gate2, out_ref):
        out_ref[...] = _standardise(DEEPNORM_ALPHA * x1 + gate2 * ffn) * lng_ref[...] + lnb_ref[...]

    @pl.when(i < n_prompt_tiles)
    def _():
        finish(x1p_ref[...], g2p_ref[...], yp_ref)

    @pl.when(i >= n_prompt_tiles)
    def _():
        finish(x1s_ref[...], g2s_ref[...], ysm_ref)


def _combine(dest_tiles, gates_tok, x1_p, x1_s, gate2_p, gate2_s, ln2_g, ln2_b, ys):
    n_p = x1_p.shape[0] // TOKEN_TILE
    n_s = x1_s.shape[0] // TOKEN_TILE
    n = n_p + n_s

    def p_idx(i):
        return jnp.minimum(i, n_p - 1)

    def s_idx(i):
        return jnp.maximum(i - n_p, 0)

    smem_tile = (1, 1, TOP_K * TOKEN_TILE)
    return pl.pallas_call(
        functools.partial(_combine_kernel, n_p),
        grid=(n,),
        in_specs=[
            pl.BlockSpec(smem_tile, lambda i: (i, 0, 0), memory_space=pltpu.SMEM),
            pl.BlockSpec(smem_tile, lambda i: (jnp.minimum(i + 1, n - 1), 0, 0), memory_space=pltpu.SMEM),
            pl.BlockSpec((TOKEN_TILE, TOP_K), lambda i: (i, 0)),
            pl.BlockSpec((TOKEN_TILE, D_MODEL), lambda i: (p_idx(i), 0)),
            pl.BlockSpec((TOKEN_TILE, D_MODEL), lambda i: (s_idx(i), 0)),
            pl.BlockSpec((1, D_MODEL), lambda i: (0, 0)),
            pl.BlockSpec((TOKEN_TILE, D_MODEL), lambda i: (s_idx(i), 0)),
            pl.BlockSpec((1, D_MODEL), lambda i: (0, 0)),
            pl.BlockSpec((1, D_MODEL), lambda i: (0, 0)),
            pl.BlockSpec(memory_space=pl.ANY),
        ],
        out_specs=[pl.BlockSpec((TOKEN_TILE, D_MODEL), lambda i: (p_idx(i), 0)),
                   pl.BlockSpec((TOKEN_TILE, D_MODEL), lambda i: (s_idx(i), 0))],
        out_shape=[jax.ShapeDtypeStruct(x1_p.shape, F32), jax.ShapeDtypeStruct(x1_s.shape, F32)],
        scratch_shapes=[pltpu.VMEM((2, TOP_K, TOKEN_TILE, D_MODEL), F32), pltpu.SemaphoreType.DMA((2,))],
        compiler_params=_params(1),
        name="combine",
    )(dest_tiles, dest_tiles, gates_tok, x1_p, x1_s, gate2_p, gate2_s,
      ln2_g.reshape(1, D_MODEL), ln2_b.reshape(1, D_MODEL), ys)


def _rope_tables(pos):
    half = HEAD_DIM // 2
    inv_freq = ROPE_THETA ** (-jnp.arange(half, dtype=F32) / half)
    ang = pos.astype(F32)[:, None] * inv_freq
    cos, sin = jnp.cos(ang), jnp.sin(ang)
    reps = LANES // HEAD_DIM
    return (jnp.tile(jnp.concatenate([cos, cos], axis=1), (1, reps)),
            jnp.tile(jnp.concatenate([-sin, sin], axis=1), (1, reps)))


def _work_items(group_start, group_size, n_items):
    chunk_rows = EXPERT_MAX_BLOCKS * ROW_BLOCK
    ex = jnp.arange(N_EXPERTS, dtype=jnp.int32)
    n_chunks = (group_size + chunk_rows - 1) // chunk_rows
    chunk_end = jnp.sum(jnp.where(ex[None, :] <= ex[:, None], n_chunks[None, :], 0), axis=1)
    chunk_start = chunk_end - n_chunks
    item = jnp.arange(n_items, dtype=jnp.int32)
    live = item < chunk_end[-1]
    it = jnp.minimum(item, chunk_end[-1] - 1)
    exp = jnp.minimum(jnp.sum((chunk_end[None, :] <= it[:, None]).astype(jnp.int32), axis=1), N_EXPERTS - 1)
    own = exp[:, None] == ex[None, :]

    def pick(per_expert):
        return jnp.sum(jnp.where(own, per_expert[None, :], 0), axis=1)

    chunk = it - pick(chunk_start)
    blocks = pick(group_size) // ROW_BLOCK
    per_chunk = (blocks + jnp.maximum(pick(n_chunks), 1) - 1) // jnp.maximum(pick(n_chunks), 1)
    row0 = pick(group_start) + chunk * per_chunk * ROW_BLOCK
    nblk = jnp.clip(blocks - chunk * per_chunk, 0, per_chunk)
    nblk = jnp.where(live, nblk, 0)
    return row0.astype(jnp.int32), nblk.astype(jnp.int32), exp.astype(jnp.int32)


def _moe(u2_p, u2_s, logits_t, x1_p, x1_s, gate2_p, gate2_s, ln2_g, ln2_b, w_up, b_up, w_down, b_down):
    n_tok = logits_t.shape[1]
    eidx, gates, rank, counts = _route(logits_t)
    count = counts[:, 0]
    group_size = (count + ROW_BLOCK - 1) // ROW_BLOCK * ROW_BLOCK
    ex = jnp.arange(N_EXPERTS, dtype=jnp.int32)
    group_end = jnp.sum(jnp.where(ex[None, :] <= ex[:, None], group_size[None, :], 0), axis=1)
    group_start = group_end - group_size
    dest = _dest(group_start, eidx, rank)
    n_tiles = n_tok // TOKEN_TILE
    dest_tiles = dest.reshape(TOP_K, n_tiles, TOKEN_TILE).transpose(1, 0, 2).reshape(n_tiles, 1, TOP_K * TOKEN_TILE)
    max_rows = n_tok * TOP_K + N_EXPERTS * (ROW_BLOCK - 1)
    n_rows = (max_rows + ROW_BLOCK - 1) // ROW_BLOCK * ROW_BLOCK
    n_items = N_EXPERTS + max_rows // (EXPERT_MAX_BLOCKS * ROW_BLOCK)
    xs = _dispatch(group_end, group_size, dest_tiles, u2_p, u2_s, n_rows)
    ys = _experts(_work_items(group_start, group_size, n_items), group_end[-1], xs, w_up, b_up, w_down, b_down)
    return _combine(dest_tiles, gates.T, x1_p, x1_s, gate2_p, gate2_s, ln2_g, ln2_b, ys)


def _layer(x_p, x_s, cache_k, cache_v, state_conv, c_p, c_s,
           w_ada, b_ada, w_in, b_in, conv_w, sinks, norm_attn_g, norm_conv_g, w_out, b_out,
           ln1_g, ln1_b, w_router, b_router, w_up, b_up, w_down, b_down, ln2_g, ln2_b, past_len):
    T = x_p.shape[0]
    B = x_s.shape[0]
    n_c = 1 + B
    pad_c = (-n_c) % SUBLANES
    c_all = jnp.concatenate([c_p, c_s, jnp.zeros((pad_c, D_MODEL), F32)], axis=0)
    mod = _ada(c_all, w_ada, b_ada)
    shift1, scale1, gate1, shift2, scale2, gate2 = [mod[:, j * D_MODEL:(j + 1) * D_MODEL] for j in range(6)]

    def prompt(a):
        return a[0:1]

    def sample(a):
        return a[1:n_c]

    w_in_bf = w_in.astype(BF16)
    w_out_bf = w_out.astype(BF16)
    w_router_t_bf = w_router.T.astype(BF16)

    cos_p, sin_p = _rope_tables(jnp.arange(T, dtype=jnp.int32))
    q_p, k_p, v_p, gb_p, uc_p = _inproj(x_p, prompt(shift1), prompt(scale1), cos_p, sin_p, w_in_bf, b_in,
                                        min(INPROJ_TILE_M, T))
    attn_p = _attn_prompt(q_p, k_p, v_p, sinks, norm_attn_g)
    x1_p, u2_p, lg_p = _mix(attn_p, gb_p, uc_p, jnp.zeros((CONV_WIDTH - 1, CONV_DIM), F32), conv_w, norm_conv_g,
                            w_out_bf, b_out, x_p, prompt(gate1), ln1_g, ln1_b, prompt(shift2), prompt(scale2),
                            w_router_t_bf, b_router, min(MIX_TILE_M, T), True)

    cos_s, sin_s = _rope_tables(jnp.full((1,), past_len, jnp.int32))
    q_s, k_s, v_s, gb_s, uc_s = _inproj(x_s, sample(shift1), sample(scale1), cos_s, sin_s, w_in_bf, b_in, B)
    attn_s, new_k_s, new_v_s = _attn_decode(q_s, k_s, v_s, cache_k, cache_v, sinks, norm_attn_g)
    x1_s, u2_s, lg_s = _mix(attn_s.reshape(B, ATTN_DIM), gb_s, uc_s, (state_conv[:, 1], state_conv[:, 0]), conv_w,
                            norm_conv_g, w_out_bf, b_out, x_s, sample(gate1), ln1_g, ln1_b, sample(shift2),
                            sample(scale2), w_router_t_bf, b_router, B, False)

    y_p, y_s = _moe(u2_p, u2_s, jnp.concatenate([lg_p, lg_s], axis=1), x1_p, x1_s, prompt(gate2), sample(gate2),
                    ln2_g, ln2_b, w_up, b_up, w_down, b_down)

    new_k_p = k_p[T - WINDOW:].reshape(WINDOW, N_KV_HEADS, HEAD_DIM)
    new_v_p = v_p[T - WINDOW:].reshape(WINDOW, N_KV_HEADS, HEAD_DIM)
    new_conv_p = uc_p[T - (CONV_WIDTH - 1):]
    new_conv_s = jnp.stack([state_conv[:, 1], uc_s], axis=1)
    return (y_p, y_s, new_k_p, new_v_p, new_conv_p,
            new_k_s.reshape(B, WINDOW, N_KV_HEADS, HEAD_DIM), new_v_s.reshape(B, WINDOW, N_KV_HEADS, HEAD_DIM), new_conv_s)


def kernel(x_prompt, x_sample, cache_k, cache_v, state_conv, c_prompt, c_sample, w_ada, b_ada, w_in, b_in, conv_w, sinks, norm_attn_g, norm_conv_g, w_out, b_out, ln1_g, ln1_b, w_router, b_router, w_up, b_up, w_down, b_down, ln2_g, ln2_b):
    assert x_prompt.shape[0] == 1 and x_sample.shape[1] == 1 and w_ada.shape[0] == DEPTH == 1
    B = x_sample.shape[0]
    (y_p, y_s, nk_p, nv_p, nc_p, nk_s, nv_s, nc_s) = _layer(
        x_prompt[0], x_sample[:, 0], cache_k[0], cache_v[0], state_conv[0], c_prompt, c_sample,
        w_ada[0], b_ada[0], w_in[0], b_in[0], conv_w[0], sinks[0], norm_attn_g[0], norm_conv_g[0], w_out[0], b_out[0],
        ln1_g[0], ln1_b[0], w_router[0], b_router[0], w_up[0], b_up[0], w_down[0], b_down[0], ln2_g[0], ln2_b[0],
        PAST_LEN)
    return (y_p[None], y_s.reshape(B, 1, D_MODEL), nk_p[None, None], nv_p[None, None], nc_p[None, None],
            nk_s[None], nv_s[None], nc_s[None])
```

```python
import functools

import jax
import jax.numpy as jnp
from jax import lax
from jax.experimental import pallas as pl
from jax.experimental.pallas import tpu as pltpu

F32 = jnp.float32
BF16 = jnp.bfloat16

D_MODEL = 2048
HEAD_DIM = 64
N_HEADS = 16
N_KV_HEADS = 4
GQA_GROUP = N_HEADS // N_KV_HEADS
ATTN_DIM = N_HEADS * HEAD_DIM
KV_DIM = N_KV_HEADS * HEAD_DIM
CONV_DIM = D_MODEL - ATTN_DIM
CONV_WIDTH = 3
IN_DIM = ATTN_DIM + 2 * KV_DIM + 3 * CONV_DIM
WINDOW = 128
PAST_LEN = 16384
ROPE_THETA = 10000.0
N_EXPERTS = 32
TOP_K = 4
D_FF = D_MODEL
SWIGLU_LIMIT = 7.0
SWIGLU_ALPHA = 1.702
DEPTH = 1
DEEPNORM_ALPHA = (2.0 * DEPTH) ** 0.25
LN_EPS = 1e-5
RMS_EPS = 1e-6
COL_K = ATTN_DIM
COL_V = COL_K + KV_DIM
COL_B = COL_V + KV_DIM
COL_C = COL_B + CONV_DIM
COL_X = COL_C + CONV_DIM

LANES = 128
SUBLANES = 8
VMEM_LIMIT_BYTES = 60 * 1024 * 1024

TOKEN_TILE = 128
DISPATCH_SLOTS = 3
ADA_TILE_N = 1024
INPROJ_TILE_M = 512
INPROJ_CHUNK_N = 512
ATTN_BLOCKS = 4
MIX_TILE_M = 512
MIX_CHAIN_M = 256
DEC_TILE_B = 16
ROW_BLOCK = 256
EXPERT_MAX_BLOCKS = 6
FF_TILE = 512
NEG_BIG = -1e30


def _params(n_axes, vmem=VMEM_LIMIT_BYTES):
    return pltpu.CompilerParams(dimension_semantics=("arbitrary",) * n_axes, vmem_limit_bytes=vmem)


def _standardise(x):
    mu = jnp.mean(x, axis=-1, keepdims=True)
    xc = x - mu
    var = jnp.mean(xc * xc, axis=-1, keepdims=True)
    return xc * lax.rsqrt(var + LN_EPS)


def _rms(x, g):
    return x * lax.rsqrt(jnp.mean(x * x, axis=-1, keepdims=True) + RMS_EPS) * g


def _ada_kernel(c_ref, w_ref, b_ref, o_ref):
    c = c_ref[...]
    s = (c * jax.nn.sigmoid(c)).astype(BF16)
    o_ref[...] = jnp.dot(s, w_ref[...].astype(BF16), preferred_element_type=F32) + b_ref[...]


def _ada(c_all, w_ada, b_ada):
    rows = c_all.shape[0]
    n_out = w_ada.shape[1]
    return pl.pallas_call(
        _ada_kernel,
        grid=(n_out // ADA_TILE_N,),
        in_specs=[
            pl.BlockSpec((rows, D_MODEL), lambda j: (0, 0)),
            pl.BlockSpec((D_MODEL, ADA_TILE_N), lambda j: (0, j)),
            pl.BlockSpec((1, ADA_TILE_N), lambda j: (0, j)),
        ],
        out_specs=pl.BlockSpec((rows, ADA_TILE_N), lambda j: (0, j)),
        out_shape=jax.ShapeDtypeStruct((rows, n_out), F32),
        compiler_params=_params(1),
        name="ada",
    )(c_all, w_ada, b_ada.reshape(1, n_out))


def _inproj_kernel(x_ref, shift_ref, scale_ref, cos_ref, sin_ref, w_ref, b_ref,
                   q_ref, k_ref, v_ref, gb_ref, uc_ref):
    u = (_standardise(x_ref[...]) * (1.0 + scale_ref[...]) + shift_ref[...]).astype(BF16)
    cos = cos_ref[...]
    sin = sin_ref[...]
    lane = lax.broadcasted_iota(jnp.int32, (1, LANES), 1)
    first_half = (lane % HEAD_DIM) < (HEAD_DIM // 2)

    def rope(z):
        partner = jnp.where(first_half, pltpu.roll(z, LANES - HEAD_DIM // 2, axis=1),
                            pltpu.roll(z, HEAD_DIM // 2, axis=1))
        return z * cos + partner * sin

    def proj(c0):
        w = w_ref[:, c0:c0 + INPROJ_CHUNK_N]
        return jnp.dot(u, w, preferred_element_type=F32) + b_ref[:, c0:c0 + INPROJ_CHUNK_N]

    groups = INPROJ_CHUNK_N // LANES
    for j in range(ATTN_DIM // INPROJ_CHUNK_N):
        z = proj(j * INPROJ_CHUNK_N)
        for g in range(groups):
            c0 = j * INPROJ_CHUNK_N + g * LANES
            q_ref[:, c0:c0 + LANES] = (rope(z[:, g * LANES:(g + 1) * LANES]) * (HEAD_DIM ** -0.5)).astype(BF16)
    z = proj(COL_K)
    for g in range(KV_DIM // LANES):
        k_ref[:, g * LANES:(g + 1) * LANES] = rope(z[:, g * LANES:(g + 1) * LANES])
    v_ref[...] = z[:, KV_DIM:2 * KV_DIM]
    for j in range(CONV_DIM // INPROJ_CHUNK_N):
        sl = slice(j * INPROJ_CHUNK_N, (j + 1) * INPROJ_CHUNK_N)
        gb_ref[:, sl] = proj(COL_B + j * INPROJ_CHUNK_N)
        uc_ref[:, sl] = proj(COL_C + j * INPROJ_CHUNK_N) * proj(COL_X + j * INPROJ_CHUNK_N)


def _inproj(x, shift, scale, cos, sin, w_in_bf, b_in, tm):
    T = x.shape[0]
    per_row_mod = shift.shape[0] != 1
    per_row_pos = cos.shape[0] != 1
    mod_spec = pl.BlockSpec((tm, D_MODEL), lambda i: (i, 0)) if per_row_mod else pl.BlockSpec((1, D_MODEL), lambda i: (0, 0))
    pos_spec = pl.BlockSpec((tm, LANES), lambda i: (i, 0)) if per_row_pos else pl.BlockSpec((1, LANES), lambda i: (0, 0))

    def row_spec(width):
        return pl.BlockSpec((tm, width), lambda i: (i, 0))

    return pl.pallas_call(
        _inproj_kernel,
        grid=(T // tm,),
        in_specs=[
            row_spec(D_MODEL), mod_spec, mod_spec, pos_spec, pos_spec,
            pl.BlockSpec((D_MODEL, IN_DIM), lambda i: (0, 0), pipeline_mode=pl.Buffered(1)),
            pl.BlockSpec((1, IN_DIM), lambda i: (0, 0)),
        ],
        out_specs=[row_spec(ATTN_DIM), row_spec(KV_DIM), row_spec(KV_DIM), row_spec(CONV_DIM), row_spec(CONV_DIM)],
        out_shape=[
            jax.ShapeDtypeStruct((T, ATTN_DIM), BF16),
            jax.ShapeDtypeStruct((T, KV_DIM), F32),
            jax.ShapeDtypeStruct((T, KV_DIM), F32),
            jax.ShapeDtypeStruct((T, CONV_DIM), F32),
            jax.ShapeDtypeStruct((T, CONV_DIM), F32),
        ],
        compiler_params=_params(1),
        name="inproj",
    )(x, shift, scale, cos, sin, w_in_bf, b_in.reshape(1, IN_DIM))


def _attn_kernel(q_ref, kp_ref, kc_ref, vp_ref, vc_ref, bias_ref, sink_ref, g_ref, o_ref):
    n = pl.program_id(0)
    n_blocks = q_ref.shape[0] // WINDOW
    k = jnp.concatenate([kp_ref[...], kc_ref[...]], axis=0).astype(BF16)
    v = jnp.concatenate([vp_ref[...], vc_ref[...]], axis=0).astype(BF16)
    for b in range(n_blocks):
        q = q_ref[b * WINDOW:(b + 1) * WINDOW, :]
        kb = k[b * WINDOW:(b + 2) * WINDOW, :]
        vb = v[b * WINDOW:(b + 2) * WINDOW, :]
        bias = bias_ref[jnp.minimum(n, 1)] if b == 0 else bias_ref[1]
        outs = []
        for g in range(N_KV_HEADS):
            heads = range(g * GQA_GROUP, (g + 1) * GQA_GROUP)
            qg = jnp.concatenate([q[:, h * HEAD_DIM:(h + 1) * HEAD_DIM] for h in heads], axis=0)
            kg = kb[:, g * HEAD_DIM:(g + 1) * HEAD_DIM]
            vg = vb[:, g * HEAD_DIM:(g + 1) * HEAD_DIM]
            sink = sink_ref[g]
            s = lax.dot_general(kg, qg, (((1,), (1,)), ((), ())), preferred_element_type=F32) + bias
            m = jnp.maximum(jnp.max(s, axis=0, keepdims=True), sink)
            e = jnp.exp(s - m)
            den = jnp.sum(e, axis=0, keepdims=True) + jnp.exp(sink - m)
            o = lax.dot_general(vg, e.astype(BF16), (((0,), (0,)), ((), ())), preferred_element_type=F32)
            o = o / den
            outs.extend(o[:, j * WINDOW:(j + 1) * WINDOW] for j in range(GQA_GROUP))
        o = jnp.concatenate(outs, axis=0).T
        o_ref[b * WINDOW:(b + 1) * WINDOW, :] = _rms(o, g_ref[...]).astype(BF16)


def _attn_mask_bias():
    qi = jnp.arange(GQA_GROUP * WINDOW, dtype=jnp.int32)[None, :] % WINDOW
    ks = jnp.arange(2 * WINDOW, dtype=jnp.int32)[:, None]
    later = (ks >= qi) & (ks <= qi + WINDOW)
    first = later & (ks >= WINDOW)
    return jnp.where(jnp.stack([first, later]), 0.0, NEG_BIG).astype(F32)


def _attn_prompt(q, k, v, sinks, norm_g):
    T = q.shape[0]
    nb = ATTN_BLOCKS if T % (ATTN_BLOCKS * WINDOW) == 0 else 1

    def cur(width):
        return pl.BlockSpec((nb * WINDOW, width), lambda n: (n, 0))

    def prev(width):
        return pl.BlockSpec((WINDOW, width), lambda n: (jnp.maximum(n * nb - 1, 0), 0))

    return pl.pallas_call(
        _attn_kernel,
        grid=(T // (nb * WINDOW),),
        in_specs=[
            cur(ATTN_DIM), prev(KV_DIM), cur(KV_DIM), prev(KV_DIM), cur(KV_DIM),
            pl.BlockSpec((2, 2 * WINDOW, GQA_GROUP * WINDOW), lambda n: (0, 0, 0)),
            pl.BlockSpec((N_KV_HEADS, 1, GQA_GROUP * WINDOW), lambda n: (0, 0, 0)),
            pl.BlockSpec((1, ATTN_DIM), lambda n: (0, 0)),
        ],
        out_specs=cur(ATTN_DIM),
        out_shape=jax.ShapeDtypeStruct((T, ATTN_DIM), BF16),
        compiler_params=_params(1),
        name="attn",
    )(q, k, k, v, v, _attn_mask_bias(),
      jnp.repeat(sinks, WINDOW).reshape(N_KV_HEADS, 1, GQA_GROUP * WINDOW),
      norm_g.reshape(1, ATTN_DIM))


def _attn_dec_kernel(q_ref, kn_ref, vn_ref, ck_ref, cv_ref, sink_ref, g_ref, o_ref, nk_ref, nv_ref):
    tb = q_ref.shape[0]
    q = q_ref[...].astype(F32)
    lane_group = lax.broadcasted_iota(jnp.int32, (1, N_HEADS, KV_DIM), 2) // HEAD_DIM
    head_group = lax.broadcasted_iota(jnp.int32, (1, N_HEADS, KV_DIM), 1) // GQA_GROUP
    own = lane_group == head_group
    qe = jnp.where(own, jnp.concatenate([q] * N_KV_HEADS, axis=2), 0.0)
    ck = ck_ref[...]
    cv = cv_ref[...]
    kn = kn_ref[...]
    vn = vn_ref[...]
    s = jnp.einsum("bhc,bwc->bhw", qe.astype(BF16), ck.astype(BF16), preferred_element_type=F32)
    s_new = jnp.sum(qe.astype(BF16).astype(F32) * kn.astype(BF16).astype(F32), axis=2, keepdims=True)
    sink = sink_ref[...]
    m = jnp.maximum(jnp.maximum(jnp.max(s, axis=2, keepdims=True), s_new), sink)
    e = jnp.exp(s - m)
    e_new = jnp.exp(s_new - m)
    den = jnp.sum(e, axis=2, keepdims=True) + e_new + jnp.exp(sink - m)
    p = (e / den).astype(BF16)
    p_new = (e_new / den).astype(BF16).astype(F32)
    o = jnp.einsum("bhw,bwc->bhc", p, cv.astype(BF16), preferred_element_type=F32)
    o = o + p_new * vn.astype(BF16).astype(F32)
    o = jnp.where(own, o, 0.0)
    oh = o[:, :, 0:HEAD_DIM]
    for g in range(1, N_KV_HEADS):
        oh = oh + o[:, :, g * HEAD_DIM:(g + 1) * HEAD_DIM]
    ms = jnp.sum(jnp.sum(oh * oh, axis=2, keepdims=True), axis=1, keepdims=True) / ATTN_DIM
    o_ref[...] = (oh * lax.rsqrt(ms + RMS_EPS) * g_ref[...]).astype(BF16)
    row = lax.broadcasted_iota(jnp.int32, (1, WINDOW, 1), 1)
    nk_ref[...] = jnp.where(row == WINDOW - 1, kn, pltpu.roll(ck, WINDOW - 1, axis=1))
    nv_ref[...] = jnp.where(row == WINDOW - 1, vn, pltpu.roll(cv, WINDOW - 1, axis=1))


def _attn_decode(q, k_new, v_new, cache_k, cache_v, sinks, norm_g):
    B = q.shape[0]
    tb = DEC_TILE_B

    def b3(d1, d2):
        return pl.BlockSpec((tb, d1, d2), lambda i: (i, 0, 0))

    return pl.pallas_call(
        _attn_dec_kernel,
        grid=(B // tb,),
        in_specs=[
            b3(N_HEADS, HEAD_DIM), b3(1, KV_DIM), b3(1, KV_DIM), b3(WINDOW, KV_DIM), b3(WINDOW, KV_DIM),
            pl.BlockSpec((1, N_HEADS, 1), lambda i: (0, 0, 0)),
            pl.BlockSpec((1, N_HEADS, HEAD_DIM), lambda i: (0, 0, 0)),
        ],
        out_specs=[b3(N_HEADS, HEAD_DIM), b3(WINDOW, KV_DIM), b3(WINDOW, KV_DIM)],
        out_shape=[
            jax.ShapeDtypeStruct((B, N_HEADS, HEAD_DIM), BF16),
            jax.ShapeDtypeStruct((B, WINDOW, KV_DIM), F32),
            jax.ShapeDtypeStruct((B, WINDOW, KV_DIM), F32),
        ],
        compiler_params=_params(1),
        name="attn_dec",
    )(q.reshape(B, N_HEADS, HEAD_DIM), k_new.reshape(B, 1, KV_DIM), v_new.reshape(B, 1, KV_DIM),
      cache_k.reshape(B, WINDOW, KV_DIM), cache_v.reshape(B, WINDOW, KV_DIM),
      sinks.reshape(1, N_HEADS, 1), norm_g.reshape(1, N_HEADS, HEAD_DIM))


def _mix_tail(attn_ref, gb_ref, uc, um1, um2, cw_ref, gconv_ref, wout_ref, bout_ref, x_ref, gate1_ref, ln1g_ref, ln1b_ref,
              shift2_ref, scale2_ref, wr_ref, br_ref, x1_ref, u2_ref, lg_ref):
    tm = uc.shape[0]
    n_split = max(tm // MIX_CHAIN_M, 1)
    h = tm // n_split
    cw = cw_ref[...]
    for s in range(n_split):
        r = slice(s * h, (s + 1) * h)

        def rows(ref):
            return ref[r, :] if ref.shape[0] == tm else ref[...]

        conv = cw[0:1, :] * um2[r] + cw[1:2, :] * um1[r] + cw[2:3, :] * uc[r]
        conv_n = _rms(gb_ref[r, :] * conv, gconv_ref[...]).astype(BF16)
        mixed = (jnp.dot(attn_ref[r, :], wout_ref[0:ATTN_DIM, :], preferred_element_type=F32)
                 + jnp.dot(conv_n, wout_ref[ATTN_DIM:D_MODEL, :], preferred_element_type=F32) + bout_ref[...])
        x1 = _standardise(DEEPNORM_ALPHA * x_ref[r, :] + rows(gate1_ref) * mixed) * ln1g_ref[...] + ln1b_ref[...]
        x1_ref[r, :] = x1
        u2 = _standardise(x1) * (1.0 + rows(scale2_ref)) + rows(shift2_ref)
        u2_ref[r, :] = u2
        lg_ref[:, r] = lax.dot_general(wr_ref[...], u2.astype(BF16), (((1,), (1,)), ((), ())),
                                       preferred_element_type=F32) + br_ref[...]


def _mix_seq_kernel(attn_ref, gb_ref, uc_ref, halo_ref, hist_ref, *rest):
    i = pl.program_id(0)
    uc = uc_ref[...]
    tm = uc.shape[0]
    above = jnp.where(i == 0, hist_ref[...], halo_ref[...])
    row = lax.broadcasted_iota(jnp.int32, (tm, 1), 0)
    um1 = jnp.where(row == 0, above[7:8, :], pltpu.roll(uc, 1, axis=0))
    um2 = jnp.where(row == 0, above[6:7, :], jnp.where(row == 1, above[7:8, :], pltpu.roll(uc, 2, axis=0)))
    _mix_tail(attn_ref, gb_ref, uc, um1, um2, *rest)


def _mix_tok_kernel(attn_ref, gb_ref, uc_ref, um1_ref, um2_ref, *rest):
    _mix_tail(attn_ref, gb_ref, uc_ref[...], um1_ref[...], um2_ref[...], *rest)


def _mix(attn_n, gb, uc, conv_prev, conv_w, norm_conv_g, w_out_bf, b_out, x, gate1, ln1_g, ln1_b, shift2, scale2,
         w_router_t_bf, b_router, tm, sequential):
    T = x.shape[0]
    per_row_mod = gate1.shape[0] != 1

    def row_spec(width):
        return pl.BlockSpec((tm, width), lambda i: (i, 0))

    def const_spec(rows, width):
        return pl.BlockSpec((rows, width), lambda i: (0, 0), pipeline_mode=pl.Buffered(1))

    mod_spec = row_spec(D_MODEL) if per_row_mod else const_spec(1, D_MODEL)
    if sequential:
        hist8 = jnp.concatenate([jnp.zeros((SUBLANES - 2, CONV_DIM), F32), conv_prev], axis=0)
        halo_blocks = tm // SUBLANES
        conv_specs = [pl.BlockSpec((SUBLANES, CONV_DIM), lambda i: (jnp.maximum(i * halo_blocks - 1, 0), 0)),
                      const_spec(SUBLANES, CONV_DIM)]
        conv_args = (uc, hist8)
        body = _mix_seq_kernel
    else:
        conv_specs = [row_spec(CONV_DIM), row_spec(CONV_DIM)]
        conv_args = conv_prev
        body = _mix_tok_kernel
    return pl.pallas_call(
        body,
        grid=(T // tm,),
        in_specs=[row_spec(ATTN_DIM), row_spec(CONV_DIM), row_spec(CONV_DIM), *conv_specs,
                  const_spec(CONV_WIDTH, CONV_DIM), const_spec(1, CONV_DIM),
                  const_spec(D_MODEL, D_MODEL), const_spec(1, D_MODEL),
                  row_spec(D_MODEL), mod_spec, const_spec(1, D_MODEL), const_spec(1, D_MODEL),
                  mod_spec, mod_spec,
                  const_spec(N_EXPERTS, D_MODEL), const_spec(N_EXPERTS, 1)],
        out_specs=[row_spec(D_MODEL), row_spec(D_MODEL), pl.BlockSpec((N_EXPERTS, tm), lambda i: (0, i))],
        out_shape=[jax.ShapeDtypeStruct((T, D_MODEL), F32), jax.ShapeDtypeStruct((T, D_MODEL), F32),
                   jax.ShapeDtypeStruct((N_EXPERTS, T), F32)],
        compiler_params=_params(1),
        name="mix_seq" if sequential else "mix_tok",
    )(attn_n, gb, uc, *conv_args, conv_w, norm_conv_g.reshape(1, CONV_DIM), w_out_bf, b_out.reshape(1, D_MODEL),
      x, gate1, ln1_g.reshape(1, D_MODEL), ln1_b.reshape(1, D_MODEL), shift2, scale2,
      w_router_t_bf, b_router.reshape(N_EXPERTS, 1))


def _route_kernel(lg_ref, eidx_ref, gate_ref, rank_ref, cnt_ref):
    n_tok = lg_ref.shape[1]
    r = lax.broadcasted_iota(jnp.int32, (TOKEN_TILE, TOKEN_TILE), 0)
    c = lax.broadcasted_iota(jnp.int32, (TOKEN_TILE, TOKEN_TILE), 1)
    earlier = (r < c).astype(BF16)
    expert = lax.broadcasted_iota(jnp.int32, (N_EXPERTS, TOKEN_TILE), 0).astype(F32)

    def body(ci, count):
        off = pl.multiple_of(ci * TOKEN_TILE, TOKEN_TILE)
        l = lg_ref[:, pl.ds(off, TOKEN_TILE)]
        vals, idxs, sels = [], [], []
        for _ in range(TOP_K):
            m = jnp.max(l, axis=0, keepdims=True)
            idx = jnp.min(jnp.where(l == m, expert, float(N_EXPERTS)), axis=0, keepdims=True)
            sel = expert == idx
            vals.append(m)
            idxs.append(idx)
            sels.append(sel)
            l = jnp.where(sel, -jnp.inf, l)
        chosen = jnp.where(sels[0] | sels[1] | sels[2] | sels[3], 1.0, 0.0)
        before = jnp.dot(chosen.astype(BF16), earlier, preferred_element_type=F32) + count
        ranks = [jnp.sum(jnp.where(s, before, 0.0), axis=0, keepdims=True) for s in sels]
        es = [jnp.exp(v - vals[0]) for v in vals]
        den = es[0] + es[1] + es[2] + es[3]
        eidx_ref[:, pl.ds(off, TOKEN_TILE)] = jnp.concatenate(idxs, axis=0).astype(jnp.int32)
        rank_ref[:, pl.ds(off, TOKEN_TILE)] = jnp.concatenate(ranks, axis=0).astype(jnp.int32)
        gate_ref[:, pl.ds(off, TOKEN_TILE)] = jnp.concatenate([e / den for e in es], axis=0)
        return count + jnp.sum(chosen, axis=1, keepdims=True)

    count = lax.fori_loop(0, n_tok // TOKEN_TILE, body, jnp.zeros((N_EXPERTS, 1), F32))
    cnt_ref[...] = jnp.broadcast_to(count, (N_EXPERTS, LANES)).astype(jnp.int32)


def _route(logits_t):
    n_tok = logits_t.shape[1]
    return pl.pallas_call(
        _route_kernel,
        out_shape=[jax.ShapeDtypeStruct((TOP_K, n_tok), jnp.int32), jax.ShapeDtypeStruct((TOP_K, n_tok), F32),
                   jax.ShapeDtypeStruct((TOP_K, n_tok), jnp.int32), jax.ShapeDtypeStruct((N_EXPERTS, LANES), jnp.int32)],
        compiler_params=pltpu.CompilerParams(vmem_limit_bytes=VMEM_LIMIT_BYTES),
        name="route",
    )(logits_t)


def _dest_kernel(start_ref, eidx_ref, rank_ref, dest_ref):
    e = eidx_ref[...]
    base = jnp.zeros(e.shape, jnp.int32)
    for x in range(N_EXPERTS):
        base = jnp.where(e == x, start_ref[x], base)
    dest_ref[...] = base + rank_ref[...]


def _dest(group_start, eidx, rank):
    return pl.pallas_call(
        _dest_kernel,
        in_specs=[pl.BlockSpec(memory_space=pltpu.SMEM), pl.BlockSpec(memory_space=pltpu.VMEM),
                  pl.BlockSpec(memory_space=pltpu.VMEM)],
        out_specs=pl.BlockSpec(memory_space=pltpu.VMEM),
        out_shape=jax.ShapeDtypeStruct(eidx.shape, jnp.int32),
        name="dest",
    )(group_start, eidx, rank)


def _dispatch_kernel(n_prompt_tiles, n_tiles, gend_ref, gsize_ref, dest_ref, up_ref, us_ref, xs_ref,
                     zero_buf, tiles, zsem, load_sem, scat_sem):
    i = pl.program_id(0)
    slot = lax.rem(i, DISPATCH_SLOTS)

    def start_load(tile, s):
        @pl.when(tile < n_prompt_tiles)
        def _():
            r = pl.multiple_of(tile * TOKEN_TILE, TOKEN_TILE)
            pltpu.make_async_copy(up_ref.at[pl.ds(r, TOKEN_TILE), :], tiles.at[s], load_sem.at[s]).start()

        @pl.when(tile >= n_prompt_tiles)
        def _():
            r = pl.multiple_of((tile - n_prompt_tiles) * TOKEN_TILE, TOKEN_TILE)
            pltpu.make_async_copy(us_ref.at[pl.ds(r, TOKEN_TILE), :], tiles.at[s], load_sem.at[s]).start()

    def wait_load(s):
        pltpu.make_async_copy(up_ref.at[pl.ds(0, TOKEN_TILE), :], tiles.at[s], load_sem.at[s]).wait()

    def wait_scatter(s):
        for _ in range(TOP_K):
            pltpu.make_async_copy(tiles.at[s], xs_ref.at[pl.ds(0, TOKEN_TILE), :], scat_sem.at[s]).wait()

    @pl.when(i == 0)
    def _():
        start_load(0, 0)
        zero_buf[...] = jnp.zeros_like(zero_buf)
        for e in range(N_EXPERTS):
            @pl.when(gsize_ref[e] > 0)
            def _():
                r0 = pl.multiple_of(gend_ref[e] - ROW_BLOCK, ROW_BLOCK)
                pltpu.make_async_copy(zero_buf, xs_ref.at[pl.ds(r0, ROW_BLOCK), :], zsem).start()
        for e in range(N_EXPERTS):
            @pl.when(gsize_ref[e] > 0)
            def _():
                pltpu.make_async_copy(zero_buf, xs_ref.at[pl.ds(0, ROW_BLOCK), :], zsem).wait()

        first_unused = gend_ref[N_EXPERTS - 1] // ROW_BLOCK
        n_blocks = xs_ref.shape[0] // ROW_BLOCK

        def zero_block(b, carry):
            r0 = pl.multiple_of(b * ROW_BLOCK, ROW_BLOCK)
            pltpu.make_async_copy(zero_buf, xs_ref.at[pl.ds(r0, ROW_BLOCK), :], zsem).start()
            return carry

        def zero_wait(b, carry):
            pltpu.make_async_copy(zero_buf, xs_ref.at[pl.ds(0, ROW_BLOCK), :], zsem).wait()
            return carry

        lax.fori_loop(first_unused, n_blocks, zero_block, 0)
        lax.fori_loop(first_unused, n_blocks, zero_wait, 0)

    @pl.when(i + 1 < n_tiles)
    def _():
        nxt = lax.rem(i + 1, DISPATCH_SLOTS)

        @pl.when(i + 1 >= DISPATCH_SLOTS)
        def _():
            wait_scatter(nxt)
        start_load(i + 1, nxt)

    wait_load(slot)
    for s in range(DISPATCH_SLOTS):
        @pl.when(slot == s)
        def _():
            for j in range(TOP_K * TOKEN_TILE):
                d = dest_ref[0, 0, j]
                pltpu.make_async_copy(tiles.at[s, pl.ds(j % TOKEN_TILE, 1), :], xs_ref.at[pl.ds(d, 1), :],
                                      scat_sem.at[s]).start(priority=j % 2)

    @pl.when(i == n_tiles - 1)
    def _():
        for tile in range(max(n_tiles - DISPATCH_SLOTS, 0), n_tiles):
            wait_scatter(tile % DISPATCH_SLOTS)


def _dispatch(group_end, group_size, dest_tiles, u2_p, u2_s, n_rows):
    n_p = u2_p.shape[0] // TOKEN_TILE
    n_s = u2_s.shape[0] // TOKEN_TILE
    grid_spec = pltpu.PrefetchScalarGridSpec(
        num_scalar_prefetch=2,
        grid=(n_p + n_s,),
        in_specs=[
            pl.BlockSpec((1, 1, TOP_K * TOKEN_TILE), lambda i, *_: (i, 0, 0), memory_space=pltpu.SMEM),
            pl.BlockSpec(memory_space=pl.ANY),
            pl.BlockSpec(memory_space=pl.ANY),
        ],
        out_specs=pl.BlockSpec(memory_space=pl.ANY),
        scratch_shapes=[pltpu.VMEM((ROW_BLOCK, D_MODEL), F32), pltpu.VMEM((DISPATCH_SLOTS, TOKEN_TILE, D_MODEL), F32),
                        pltpu.SemaphoreType.DMA(()), pltpu.SemaphoreType.DMA((DISPATCH_SLOTS,)),
                        pltpu.SemaphoreType.DMA((DISPATCH_SLOTS,))],
    )
    return pl.pallas_call(
        functools.partial(_dispatch_kernel, n_p, n_p + n_s),
        grid_spec=grid_spec,
        out_shape=jax.ShapeDtypeStruct((n_rows, D_MODEL), F32),
        compiler_params=_params(1),
        name="dispatch",
    )(group_end, group_size, dest_tiles, u2_p, u2_s)


_FIRST, _MIDDLE, _LAST = 0, 1, 2


def _experts_kernel(row0_ref, nblk_ref, exp_ref, total_ref, xs_ref, wg_ref, wl_ref, wd_ref, bg_ref, bl_ref, bd_ref,
                    ys_ref, xbuf, acc, in_sem, out_sem):
    i = pl.program_id(0)
    f = pl.program_id(1)
    n_items = pl.num_programs(0)
    last_f = pl.num_programs(1) - 1
    nblk = nblk_ref[i]
    row0 = row0_ref[i]
    nblk_prev = jnp.where(i > 0, nblk_ref[jnp.maximum(i - 1, 0)], 0)
    n_started = jnp.minimum(nblk_prev, nblk)
    nxt = jnp.minimum(i + 1, n_items - 1)
    row0_next = row0_ref[nxt]
    nblk_next = jnp.where(i + 1 < n_items, nblk_ref[nxt], 0)

    def local(j):
        return pl.ds(pl.multiple_of(j * ROW_BLOCK, ROW_BLOCK), ROW_BLOCK)

    def in_hbm(r0, j):
        return pl.ds(pl.multiple_of(r0 + j * ROW_BLOCK, ROW_BLOCK), ROW_BLOCK)

    def x_copy(j, r0=row0):
        return pltpu.make_async_copy(xs_ref.at[in_hbm(r0, j), :], xbuf.at[local(j), :], in_sem.at[j])

    def y_copy(j):
        return pltpu.make_async_copy(acc.at[local(j), :], ys_ref.at[in_hbm(row0, j), :], out_sem.at[j])

    def prefetch_next(j):
        @pl.when(j < nblk_next)
        def _():
            x_copy(j, row0_next).start()

    def body(chunks, phase):
        wg = wg_ref[...].astype(BF16)
        wl = wl_ref[...].astype(BF16)
        wd = wd_ref[...].astype(BF16)
        for j0, m in chunks:
            rows = pl.ds(pl.multiple_of(j0 * ROW_BLOCK, ROW_BLOCK), m * ROW_BLOCK)
            if phase == _FIRST:
                for j in range(m):
                    x_copy(j0 + j).wait()
            x = xbuf[rows, :].astype(BF16)
            glu = jnp.dot(x, wg, preferred_element_type=F32) + bg_ref[...]
            lin = jnp.dot(x, wl, preferred_element_type=F32) + bl_ref[...]
            glu = jnp.minimum(glu, SWIGLU_LIMIT)
            lin = jnp.clip(lin, -SWIGLU_LIMIT, SWIGLU_LIMIT)
            act = glu * jax.nn.sigmoid(SWIGLU_ALPHA * glu) * (lin + 1.0)
            y = jnp.dot(act.astype(BF16), wd, preferred_element_type=F32)
            if phase == _FIRST:
                acc[rows, :] = y + bd_ref[...]
            else:
                acc[rows, :] += y
            if phase == _LAST:
                for j in range(m):
                    y_copy(j0 + j).start()

    def run(phase):
        def before(j0, m):
            if phase == _FIRST:
                for j in range(m):
                    @pl.when(j0 + j < nblk_prev)
                    def _():
                        y_copy(j0 + j).wait()

        def after(j0, m):
            if phase == _LAST:
                for j in range(m):
                    prefetch_next(j0 + j)

        def quad(q, carry):
            before(4 * q, 4)
            body(((4 * q, 2), (4 * q + 2, 2)), phase)
            after(4 * q, 4)
            return carry

        n_quads = lax.shift_right_logical(nblk, 2)
        lax.fori_loop(0, n_quads, quad, 0)
        done = lax.shift_left(n_quads, 2)

        @pl.when(jnp.bitwise_and(nblk, 2) == 2)
        def _():
            before(done, 2)
            body(((done, 1), (done + 1, 1)), phase)
            after(done, 2)

        @pl.when(jnp.bitwise_and(nblk, 1) == 1)
        def _():
            before(nblk - 1, 1)
            body(((nblk - 1, 1),), phase)
            after(nblk - 1, 1)

    def drain(j, carry):
        y_copy(j).wait()
        return carry

    @pl.when(f == 0)
    def _():
        def start(j, carry):
            x_copy(j).start()
            return carry
        lax.fori_loop(n_started, nblk, start, 0)
        run(_FIRST)
        lax.fori_loop(nblk, jnp.where(nblk > 0, nblk_prev, 0), drain, 0)

    @pl.when((f > 0) & (f < last_f))
    def _():
        run(_MIDDLE)

    @pl.when(f == last_f)
    def _():
        run(_LAST)

        @pl.when(nblk_next == 0)
        def _():
            lax.fori_loop(0, nblk, drain, 0)

        @pl.when(i == n_items - 1)
        def _():
            first_unused = total_ref[0] // ROW_BLOCK
            n_blocks = ys_ref.shape[0] // ROW_BLOCK
            acc[local(0), :] = jnp.zeros((ROW_BLOCK, D_MODEL), F32)

            def zero_copy(b):
                r0 = pl.multiple_of(b * ROW_BLOCK, ROW_BLOCK)
                return pltpu.make_async_copy(acc.at[local(0), :], ys_ref.at[pl.ds(r0, ROW_BLOCK), :], out_sem.at[0])

            def zero_block(b, carry):
                zero_copy(b).start()
                return carry

            def zero_wait(b, carry):
                zero_copy(b).wait()
                return carry

            lax.fori_loop(first_unused, n_blocks, zero_block, 0)
            lax.fori_loop(first_unused, n_blocks, zero_wait, 0)


def _experts(items, total_rows, xs, w_up, b_up, w_down, b_down):
    row0, nblk, exp = items
    n_items = row0.shape[0]
    n_rows = xs.shape[0]
    n_f = D_FF // FF_TILE
    assert n_f >= 2

    def ff(i, f, nb):
        return jnp.where(nb[i] > 0, f, n_f - 1)

    max_rows = EXPERT_MAX_BLOCKS * ROW_BLOCK
    grid_spec = pltpu.PrefetchScalarGridSpec(
        num_scalar_prefetch=4,
        grid=(n_items, n_f),
        in_specs=[
            pl.BlockSpec(memory_space=pl.ANY),
            pl.BlockSpec((None, D_MODEL, FF_TILE), lambda i, f, r0, nb, ex, tot: (ex[i], 0, ff(i, f, nb))),
            pl.BlockSpec((None, D_MODEL, FF_TILE), lambda i, f, r0, nb, ex, tot: (ex[i], 0, n_f + ff(i, f, nb))),
            pl.BlockSpec((None, FF_TILE, D_MODEL), lambda i, f, r0, nb, ex, tot: (ex[i], ff(i, f, nb), 0)),
            pl.BlockSpec((None, 1, FF_TILE), lambda i, f, r0, nb, ex, tot: (ex[i], 0, ff(i, f, nb))),
            pl.BlockSpec((None, 1, FF_TILE), lambda i, f, r0, nb, ex, tot: (ex[i], 0, n_f + ff(i, f, nb))),
            pl.BlockSpec((None, 1, D_MODEL), lambda i, f, r0, nb, ex, tot: (ex[i], 0, 0)),
        ],
        out_specs=pl.BlockSpec(memory_space=pl.ANY),
        scratch_shapes=[pltpu.VMEM((max_rows, D_MODEL), F32), pltpu.VMEM((max_rows, D_MODEL), F32),
                        pltpu.SemaphoreType.DMA((EXPERT_MAX_BLOCKS,)), pltpu.SemaphoreType.DMA((EXPERT_MAX_BLOCKS,))],
    )
    return pl.pallas_call(
        _experts_kernel,
        grid_spec=grid_spec,
        out_shape=jax.ShapeDtypeStruct((n_rows, D_MODEL), F32),
        compiler_params=_params(2),
        name="experts",
    )(row0, nblk, exp, total_rows.reshape(1), xs, w_up, w_up, w_down,
      b_up.reshape(N_EXPERTS, 1, 2 * D_FF), b_up.reshape(N_EXPERTS, 1, 2 * D_FF), b_down.reshape(N_EXPERTS, 1, D_MODEL))


def _combine_kernel(n_prompt_tiles, dcur_ref, dnxt_ref, gates_ref, x1p_ref, x1s_ref, g2p_ref, g2s_ref, lng_ref, lnb_ref,
                    ys_ref, yp_ref, ysm_ref, buf, sem):
    i = pl.program_id(0)
    n = pl.num_programs(0)
    slot = i % 2

    def gather(idx_ref, s):
        for j in range(TOP_K * TOKEN_TILE):
            d = idx_ref[0, 0, j]
            pltpu.make_async_copy(ys_ref.at[pl.ds(d, 1), :],
                                  buf.at[s, j // TOKEN_TILE, pl.ds(j % TOKEN_TILE, 1), :], sem.at[s]).start()

    @pl.when(i == 0)
    def _():
        gather(dcur_ref, 0)

    for s in range(2):
        @pl.when((i + 1 < n) & (slot == 1 - s))
        def _():
            gather(dnxt_ref, s)

    for k in range(TOP_K):
        pltpu.make_async_copy(ys_ref.at[pl.ds(0, TOKEN_TILE), :], buf.at[slot, k], sem.at[slot]).wait()

    gates = gates_ref[...]
    ffn = buf[slot, 0] * gates[:, 0:1]
    for k in range(1, TOP_K):
        ffn = ffn + buf[slot, k] * gates[:, k:k + 1]

    def finish(x1, gate2, out_ref):
        out_ref[...] = _standardise(DEEPNORM_ALPHA * x1 + gate2 * ffn) * lng_ref[...] + lnb_ref[...]

    @pl.when(i < n_prompt_tiles)
    def _():
        finish(x1p_ref[...], g2p_ref[...], yp_ref)

    @pl.when(i >= n_prompt_tiles)
    def _():
        finish(x1s_ref[...], g2s_ref[...], ysm_ref)


def _combine(dest_tiles, gates_tok, x1_p, x1_s, gate2_p, gate2_s, ln2_g, ln2_b, ys):
    n_p = x1_p.shape[0] // TOKEN_TILE
    n_s = x1_s.shape[0] // TOKEN_TILE
    n = n_p + n_s

    def p_idx(i):
        return jnp.minimum(i, n_p - 1)

    def s_idx(i):
        return jnp.maximum(i - n_p, 0)

    smem_tile = (1, 1, TOP_K * TOKEN_TILE)
    return pl.pallas_call(
        functools.partial(_combine_kernel, n_p),
        grid=(n,),
        in_specs=[
            pl.BlockSpec(smem_tile, lambda i: (i, 0, 0), memory_space=pltpu.SMEM),
            pl.BlockSpec(smem_tile, lambda i: (jnp.minimum(i + 1, n - 1), 0, 0), memory_space=pltpu.SMEM),
            pl.BlockSpec((TOKEN_TILE, TOP_K), lambda i: (i, 0)),
            pl.BlockSpec((TOKEN_TILE, D_MODEL), lambda i: (p_idx(i), 0)),
            pl.BlockSpec((TOKEN_TILE, D_MODEL), lambda i: (s_idx(i), 0)),
            pl.BlockSpec((1, D_MODEL), lambda i: (0, 0)),
            pl.BlockSpec((TOKEN_TILE, D_MODEL), lambda i: (s_idx(i), 0)),
            pl.BlockSpec((1, D_MODEL), lambda i: (0, 0)),
            pl.BlockSpec((1, D_MODEL), lambda i: (0, 0)),
            pl.BlockSpec(memory_space=pl.ANY),
        ],
        out_specs=[pl.BlockSpec((TOKEN_TILE, D_MODEL), lambda i: (p_idx(i), 0)),
                   pl.BlockSpec((TOKEN_TILE, D_MODEL), lambda i: (s_idx(i), 0))],
        out_shape=[jax.ShapeDtypeStruct(x1_p.shape, F32), jax.ShapeDtypeStruct(x1_s.shape, F32)],
        scratch_shapes=[pltpu.VMEM((2, TOP_K, TOKEN_TILE, D_MODEL), F32), pltpu.SemaphoreType.DMA((2,))],
        compiler_params=_params(1),
        name="combine",
    )(dest_tiles, dest_tiles, gates_tok, x1_p, x1_s, gate2_p, gate2_s,
      ln2_g.reshape(1, D_MODEL), ln2_b.reshape(1, D_MODEL), ys)


def _rope_tables(pos):
    half = HEAD_DIM // 2
    inv_freq = ROPE_THETA ** (-jnp.arange(half, dtype=F32) / half)
    ang = pos.astype(F32)[:, None] * inv_freq
    cos, sin = jnp.cos(ang), jnp.sin(ang)
    reps = LANES // HEAD_DIM
    return (jnp.tile(jnp.concatenate([cos, cos], axis=1), (1, reps)),
            jnp.tile(jnp.concatenate([-sin, sin], axis=1), (1, reps)))


def _work_items(group_start, group_size, n_items):
    chunk_rows = EXPERT_MAX_BLOCKS * ROW_BLOCK
    ex = jnp.arange(N_EXPERTS, dtype=jnp.int32)
    n_chunks = (group_size + chunk_rows - 1) // chunk_rows
    chunk_end = jnp.sum(jnp.where(ex[None, :] <= ex[:, None], n_chunks[None, :], 0), axis=1)
    chunk_start = chunk_end - n_chunks
    item = jnp.arange(n_items, dtype=jnp.int32)
    live = item < chunk_end[-1]
    it = jnp.minimum(item, chunk_end[-1] - 1)
    exp = jnp.minimum(jnp.sum((chunk_end[None, :] <= it[:, None]).astype(jnp.int32), axis=1), N_EXPERTS - 1)
    own = exp[:, None] == ex[None, :]

    def pick(per_expert):
        return jnp.sum(jnp.where(own, per_expert[None, :], 0), axis=1)

    chunk = it - pick(chunk_start)
    blocks = pick(group_size) // ROW_BLOCK
    per_chunk = (blocks + jnp.maximum(pick(n_chunks), 1) - 1) // jnp.maximum(pick(n_chunks), 1)
    row0 = pick(group_start) + chunk * per_chunk * ROW_BLOCK
    nblk = jnp.clip(blocks - chunk * per_chunk, 0, per_chunk)
    nblk = jnp.where(live, nblk, 0)
    return row0.astype(jnp.int32), nblk.astype(jnp.int32), exp.astype(jnp.int32)


def _moe(u2_p, u2_s, logits_t, x1_p, x1_s, gate2_p, gate2_s, ln2_g, ln2_b, w_up, b_up, w_down, b_down):
    n_tok = logits_t.shape[1]
    eidx, gates, rank, counts = _route(logits_t)
    count = counts[:, 0]
    group_size = (count + ROW_BLOCK - 1) // ROW_BLOCK * ROW_BLOCK
    ex = jnp.arange(N_EXPERTS, dtype=jnp.int32)
    group_end = jnp.sum(jnp.where(ex[None, :] <= ex[:, None], group_size[None, :], 0), axis=1)
    group_start = group_end - group_size
    dest = _dest(group_start, eidx, rank)
    n_tiles = n_tok // TOKEN_TILE
    dest_tiles = dest.reshape(TOP_K, n_tiles, TOKEN_TILE).transpose(1, 0, 2).reshape(n_tiles, 1, TOP_K * TOKEN_TILE)
    max_rows = n_tok * TOP_K + N_EXPERTS * (ROW_BLOCK - 1)
    n_rows = (max_rows + ROW_BLOCK - 1) // ROW_BLOCK * ROW_BLOCK
    n_items = N_EXPERTS + max_rows // (EXPERT_MAX_BLOCKS * ROW_BLOCK)
    xs = _dispatch(group_end, group_size, dest_tiles, u2_p, u2_s, n_rows)
    ys = _experts(_work_items(group_start, group_size, n_items), group_end[-1], xs, w_up, b_up, w_down, b_down)
    return _combine(dest_tiles, gates.T, x1_p, x1_s, gate2_p, gate2_s, ln2_g, ln2_b, ys)


def _layer(x_p, x_s, cache_k, cache_v, state_conv, c_p, c_s,
           w_ada, b_ada, w_in, b_in, conv_w, sinks, norm_attn_g, norm_conv_g, w_out, b_out,
           ln1_g, ln1_b, w_router, b_router, w_up, b_up, w_down, b_down, ln2_g, ln2_b, past_len):
    T = x_p.shape[0]
    B = x_s.shape[0]
    n_c = 1 + B
    pad_c = (-n_c) % SUBLANES
    c_all = jnp.concatenate([c_p, c_s, jnp.zeros((pad_c, D_MODEL), F32)], axis=0)
    mod = _ada(c_all, w_ada, b_ada)
    shift1, scale1, gate1, shift2, scale2, gate2 = [mod[:, j * D_MODEL:(j + 1) * D_MODEL] for j in range(6)]

    def prompt(a):
        return a[0:1]

    def sample(a):
        return a[1:n_c]

    w_in_bf = w_in.astype(BF16)
    w_out_bf = w_out.astype(BF16)
    w_router_t_bf = w_router.T.astype(BF16)

    cos_p, sin_p = _rope_tables(jnp.arange(T, dtype=jnp.int32))
    q_p, k_p, v_p, gb_p, uc_p = _inproj(x_p, prompt(shift1), prompt(scale1), cos_p, sin_p, w_in_bf, b_in,
                                        min(INPROJ_TILE_M, T))
    attn_p = _attn_prompt(q_p, k_p, v_p, sinks, norm_attn_g)
    x1_p, u2_p, lg_p = _mix(attn_p, gb_p, uc_p, jnp.zeros((CONV_WIDTH - 1, CONV_DIM), F32), conv_w, norm_conv_g,
                            w_out_bf, b_out, x_p, prompt(gate1), ln1_g, ln1_b, prompt(shift2), prompt(scale2),
                            w_router_t_bf, b_router, min(MIX_TILE_M, T), True)

    cos_s, sin_s = _rope_tables(jnp.full((1,), past_len, jnp.int32))
    q_s, k_s, v_s, gb_s, uc_s = _inproj(x_s, sample(shift1), sample(scale1), cos_s, sin_s, w_in_bf, b_in, B)
    attn_s, new_k_s, new_v_s = _attn_decode(q_s, k_s, v_s, cache_k, cache_v, sinks, norm_attn_g)
    x1_s, u2_s, lg_s = _mix(attn_s.reshape(B, ATTN_DIM), gb_s, uc_s, (state_conv[:, 1], state_conv[:, 0]), conv_w,
                            norm_conv_g, w_out_bf, b_out, x_s, sample(gate1), ln1_g, ln1_b, sample(shift2),
                            sample(scale2), w_router_t_bf, b_router, B, False)

    y_p, y_s = _moe(u2_p, u2_s, jnp.concatenate([lg_p, lg_s], axis=1), x1_p, x1_s, prompt(gate2), sample(gate2),
                    ln2_g, ln2_b, w_up, b_up, w_down, b_down)

    new_k_p = k_p[T - WINDOW:].reshape(WINDOW, N_KV_HEADS, HEAD_DIM)
    new_v_p = v_p[T - WINDOW:].reshape(WINDOW, N_KV_HEADS, HEAD_DIM)
    new_conv_p = uc_p[T - (CONV_WIDTH - 1):]
    new_conv_s = jnp.stack([state_conv[:, 1], uc_s], axis=1)
    return (y_p, y_s, new_k_p, new_v_p, new_conv_p,
            new_k_s.reshape(B, WINDOW, N_KV_HEADS, HEAD_DIM), new_v_s.reshape(B, WINDOW, N_KV_HEADS, HEAD_DIM), new_conv_s)


def kernel(x_prompt, x_sample, cache_k, cache_v, state_conv, c_prompt, c_sample, w_ada, b_ada, w_in, b_in, conv_w, sinks, norm_attn_g, norm_conv_g, w_out, b_out, ln1_g, ln1_b, w_router, b_router, w_up, b_up, w_down, b_down, ln2_g, ln2_b):
    assert x_prompt.shape[0] == 1 and x_sample.shape[1] == 1 and w_ada.shape[0] == DEPTH == 1
    B = x_sample.shape[0]
    (y_p, y_s, nk_p, nv_p, nc_p, nk_s, nv_s, nc_s) = _layer(
        x_prompt[0], x_sample[:, 0], cache_k[0], cache_v[0], state_conv[0], c_prompt, c_sample,
        w_ada[0], b_ada[0], w_in[0], b_in[0], conv_w[0], sinks[0], norm_attn_g[0], norm_conv_g[0], w_out[0], b_out[0],
        ln1_g[0], ln1_b[0], w_router[0], b_router[0], w_up[0], b_up[0], w_down[0], b_down[0], ln2_g[0], ln2_b[0],
        PAST_LEN)
    return (y_p[None], y_s.reshape(B, 1, D_MODEL), nk_p[None, None], nv_p[None, None], nc_p[None, None],
            nk_s[None], nv_s[None], nc_s[None])
```

```python
import functools

import jax
import jax.numpy as jnp
from jax import lax
from jax.experimental import pallas as pl
from jax.experimental.pallas import tpu as pltpu

F32 = jnp.float32
BF16 = jnp.bfloat16

D_MODEL = 2048
HEAD_DIM = 64
N_HEADS = 16
N_KV_HEADS = 4
GQA_GROUP = N_HEADS // N_KV_HEADS
ATTN_DIM = N_HEADS * HEAD_DIM
KV_DIM = N_KV_HEADS * HEAD_DIM
CONV_DIM = D_MODEL - ATTN_DIM
CONV_WIDTH = 3
IN_DIM = ATTN_DIM + 2 * KV_DIM + 3 * CONV_DIM
WINDOW = 128
PAST_LEN = 16384
ROPE_THETA = 10000.0
N_EXPERTS = 32
TOP_K = 4
D_FF = D_MODEL
SWIGLU_LIMIT = 7.0
SWIGLU_ALPHA = 1.702
DEPTH = 1
DEEPNORM_ALPHA = (2.0 * DEPTH) ** 0.25
LN_EPS = 1e-5
RMS_EPS = 1e-6
COL_K = ATTN_DIM
COL_V = COL_K + KV_DIM
COL_B = COL_V + KV_DIM
COL_C = COL_B + CONV_DIM
COL_X = COL_C + CONV_DIM

LANES = 128
SUBLANES = 8
VMEM_LIMIT_BYTES = 60 * 1024 * 1024

TOKEN_TILE = 128
DISPATCH_SLOTS = 3
ADA_TILE_N = 1024
INPROJ_TILE_M = 512
INPROJ_CHUNK_N = 512
ATTN_BLOCKS = 4
MIX_TILE_M = 512
MIX_CHAIN_M = 256
DEC_TILE_B = 16
ROW_BLOCK = 256
EXPERT_MAX_BLOCKS = 6
FF_TILE = 512
NEG_BIG = -1e30


def _params(n_axes, vmem=VMEM_LIMIT_BYTES):
    return pltpu.CompilerParams(dimension_semantics=("arbitrary",) * n_axes, vmem_limit_bytes=vmem)


def _standardise(x):
    mu = jnp.mean(x, axis=-1, keepdims=True)
    xc = x - mu
    var = jnp.mean(xc * xc, axis=-1, keepdims=True)
    return xc * lax.rsqrt(var + LN_EPS)


def _rms(x, g):
    return x * lax.rsqrt(jnp.mean(x * x, axis=-1, keepdims=True) + RMS_EPS) * g


def _ada_kernel(c_ref, w_ref, b_ref, o_ref):
    c = c_ref[...]
    s = (c * jax.nn.sigmoid(c)).astype(BF16)
    o_ref[...] = jnp.dot(s, w_ref[...].astype(BF16), preferred_element_type=F32) + b_ref[...]


def _ada(c_all, w_ada, b_ada):
    rows = c_all.shape[0]
    n_out = w_ada.shape[1]
    return pl.pallas_call(
        _ada_kernel,
        grid=(n_out // ADA_TILE_N,),
        in_specs=[
            pl.BlockSpec((rows, D_MODEL), lambda j: (0, 0)),
            pl.BlockSpec((D_MODEL, ADA_TILE_N), lambda j: (0, j)),
            pl.BlockSpec((1, ADA_TILE_N), lambda j: (0, j)),
        ],
        out_specs=pl.BlockSpec((rows, ADA_TILE_N), lambda j: (0, j)),
        out_shape=jax.ShapeDtypeStruct((rows, n_out), F32),
        compiler_params=_params(1),
        name="ada",
    )(c_all, w_ada, b_ada.reshape(1, n_out))


def _inproj_kernel(x_ref, shift_ref, scale_ref, cos_ref, sin_ref, w_ref, b_ref,
                   q_ref, k_ref, v_ref, gb_ref, uc_ref):
    u = (_standardise(x_ref[...]) * (1.0 + scale_ref[...]) + shift_ref[...]).astype(BF16)
    cos = cos_ref[...]
    sin = sin_ref[...]
    lane = lax.broadcasted_iota(jnp.int32, (1, LANES), 1)
    first_half = (lane % HEAD_DIM) < (HEAD_DIM // 2)

    def rope(z):
        partner = jnp.where(first_half, pltpu.roll(z, LANES - HEAD_DIM // 2, axis=1),
                            pltpu.roll(z, HEAD_DIM // 2, axis=1))
        return z * cos + partner * sin

    def proj(c0):
        w = w_ref[:, c0:c0 + INPROJ_CHUNK_N]
        return jnp.dot(u, w, preferred_element_type=F32) + b_ref[:, c0:c0 + INPROJ_CHUNK_N]

    groups = INPROJ_CHUNK_N // LANES
    for j in range(ATTN_DIM // INPROJ_CHUNK_N):
        z = proj(j * INPROJ_CHUNK_N)
        for g in range(groups):
            c0 = j * INPROJ_CHUNK_N + g * LANES
            q_ref[:, c0:c0 + LANES] = (rope(z[:, g * LANES:(g + 1) * LANES]) * (HEAD_DIM ** -0.5)).astype(BF16)
    z = proj(COL_K)
    for g in range(KV_DIM // LANES):
        k_ref[:, g * LANES:(g + 1) * LANES] = rope(z[:, g * LANES:(g + 1) * LANES])
    v_ref[...] = z[:, KV_DIM:2 * KV_DIM]
    for j in range(CONV_DIM // INPROJ_CHUNK_N):
        sl = slice(j * INPROJ_CHUNK_N, (j + 1) * INPROJ_CHUNK_N)
        gb_ref[:, sl] = proj(COL_B + j * INPROJ_CHUNK_N)
        uc_ref[:, sl] = proj(COL_C + j * INPROJ_CHUNK_N) * proj(COL_X + j * INPROJ_CHUNK_N)


def _inproj(x, shift, scale, cos, sin, w_in_bf, b_in, tm):
    T = x.shape[0]
    per_row_mod = shift.shape[0] != 1
    per_row_pos = cos.shape[0] != 1
    mod_spec = pl.BlockSpec((tm, D_MODEL), lambda i: (i, 0)) if per_row_mod else pl.BlockSpec((1, D_MODEL), lambda i: (0, 0))
    pos_spec = pl.BlockSpec((tm, LANES), lambda i: (i, 0)) if per_row_pos else pl.BlockSpec((1, LANES), lambda i: (0, 0))

    def row_spec(width):
        return pl.BlockSpec((tm, width), lambda i: (i, 0))

    return pl.pallas_call(
        _inproj_kernel,
        grid=(T // tm,),
        in_specs=[
            row_spec(D_MODEL), mod_spec, mod_spec, pos_spec, pos_spec,
            pl.BlockSpec((D_MODEL, IN_DIM), lambda i: (0, 0), pipeline_mode=pl.Buffered(1)),
            pl.BlockSpec((1, IN_DIM), lambda i: (0, 0)),
        ],
        out_specs=[row_spec(ATTN_DIM), row_spec(KV_DIM), row_spec(KV_DIM), row_spec(CONV_DIM), row_spec(CONV_DIM)],
        out_shape=[
            jax.ShapeDtypeStruct((T, ATTN_DIM), BF16),
            jax.ShapeDtypeStruct((T, KV_DIM), F32),
            jax.ShapeDtypeStruct((T, KV_DIM), F32),
            jax.ShapeDtypeStruct((T, CONV_DIM), F32),
            jax.ShapeDtypeStruct((T, CONV_DIM), F32),
        ],
        compiler_params=_params(1),
        name="inproj",
    )(x, shift, scale, cos, sin, w_in_bf, b_in.reshape(1, IN_DIM))


def _attn_kernel(q_ref, kp_ref, kc_ref, vp_ref, vc_ref, bias_ref, sink_ref, g_ref, o_ref):
    n = pl.program_id(0)
    n_blocks = q_ref.shape[0] // WINDOW
    k = jnp.concatenate([kp_ref[...], kc_ref[...]], axis=0).astype(BF16)
    v = jnp.concatenate([vp_ref[...], vc_ref[...]], axis=0).astype(BF16)
    for b in range(n_blocks):
        q = q_ref[b * WINDOW:(b + 1) * WINDOW, :]
        kb = k[b * WINDOW:(b + 2) * WINDOW, :]
        vb = v[b * WINDOW:(b + 2) * WINDOW, :]
        bias = bias_ref[jnp.minimum(n, 1)] if b == 0 else bias_ref[1]
        outs = []
        for g in range(N_KV_HEADS):
            heads = range(g * GQA_GROUP, (g + 1) * GQA_GROUP)
            qg = jnp.concatenate([q[:, h * HEAD_DIM:(h + 1) * HEAD_DIM] for h in heads], axis=0)
            kg = kb[:, g * HEAD_DIM:(g + 1) * HEAD_DIM]
            vg = vb[:, g * HEAD_DIM:(g + 1) * HEAD_DIM]
            sink = sink_ref[g]
            s = lax.dot_general(kg, qg, (((1,), (1,)), ((), ())), preferred_element_type=F32) + bias
            m = jnp.maximum(jnp.max(s, axis=0, keepdims=True), sink)
            e = jnp.exp(s - m)
            den = jnp.sum(e, axis=0, keepdims=True) + jnp.exp(sink - m)
            o = lax.dot_general(vg, e.astype(BF16), (((0,), (0,)), ((), ())), preferred_element_type=F32)
            o = o / den
            outs.extend(o[:, j * WINDOW:(j + 1) * WINDOW] for j in range(GQA_GROUP))
        o = jnp.concatenate(outs, axis=0).T
        o_ref[b * WINDOW:(b + 1) * WINDOW, :] = _rms(o, g_ref[...]).astype(BF16)


def _attn_mask_bias():
    qi = jnp.arange(GQA_GROUP * WINDOW, dtype=jnp.int32)[None, :] % WINDOW
    ks = jnp.arange(2 * WINDOW, dtype=jnp.int32)[:, None]
    later = (ks >= qi) & (ks <= qi + WINDOW)
    first = later & (ks >= WINDOW)
    return jnp.where(jnp.stack([first, later]), 0.0, NEG_BIG).astype(F32)


def _attn_prompt(q, k, v, sinks, norm_g):
    T = q.shape[0]
    nb = ATTN_BLOCKS if T % (ATTN_BLOCKS * WINDOW) == 0 else 1

    def cur(width):
        return pl.BlockSpec((nb * WINDOW, width), lambda n: (n, 0))

    def prev(width):
        return pl.BlockSpec((WINDOW, width), lambda n: (jnp.maximum(n * nb - 1, 0), 0))

    return pl.pallas_call(
        _attn_kernel,
        grid=(T // (nb * WINDOW),),
        in_specs=[
            cur(ATTN_DIM), prev(KV_DIM), cur(KV_DIM), prev(KV_DIM), cur(KV_DIM),
            pl.BlockSpec((2, 2 * WINDOW, GQA_GROUP * WINDOW), lambda n: (0, 0, 0)),
            pl.BlockSpec((N_KV_HEADS, 1, GQA_GROUP * WINDOW), lambda n: (0, 0, 0)),
            pl.BlockSpec((1, ATTN_DIM), lambda n: (0, 0)),
        ],
        out_specs=cur(ATTN_DIM),
        out_shape=jax.ShapeDtypeStruct((T, ATTN_DIM), BF16),
        compiler_params=_params(1),
        name="attn",
    )(q, k, k, v, v, _attn_mask_bias(),
      jnp.repeat(sinks, WINDOW).reshape(N_KV_HEADS, 1, GQA_GROUP * WINDOW),
      norm_g.reshape(1, ATTN_DIM))


def _attn_dec_kernel(q_ref, kn_ref, vn_ref, ck_ref, cv_ref, sink_ref, g_ref, o_ref, nk_ref, nv_ref):
    tb = q_ref.shape[0]
    q = q_ref[...].astype(F32)
    lane_group = lax.broadcasted_iota(jnp.int32, (1, N_HEADS, KV_DIM), 2) // HEAD_DIM
    head_group = lax.broadcasted_iota(jnp.int32, (1, N_HEADS, KV_DIM), 1) // GQA_GROUP
    own = lane_group == head_group
    qe = jnp.where(own, jnp.concatenate([q] * N_KV_HEADS, axis=2), 0.0)
    ck = ck_ref[...]
    cv = cv_ref[...]
    kn = kn_ref[...]
    vn = vn_ref[...]
    s = jnp.einsum("bhc,bwc->bhw", qe.astype(BF16), ck.astype(BF16), preferred_element_type=F32)
    s_new = jnp.sum(qe.astype(BF16).astype(F32) * kn.astype(BF16).astype(F32), axis=2, keepdims=True)
    sink = sink_ref[...]
    m = jnp.maximum(jnp.maximum(jnp.max(s, axis=2, keepdims=True), s_new), sink)
    e = jnp.exp(s - m)
    e_new = jnp.exp(s_new - m)
    den = jnp.sum(e, axis=2, keepdims=True) + e_new + jnp.exp(sink - m)
    p = (e / den).astype(BF16)
    p_new = (e_new / den).astype(BF16).astype(F32)
    o = jnp.einsum("bhw,bwc->bhc", p, cv.astype(BF16), preferred_element_type=F32)
    o = o + p_new * vn.astype(BF16).astype(F32)
    o = jnp.where(own, o, 0.0)
    oh = o[:, :, 0:HEAD_DIM]
    for g in range(1, N_KV_HEADS):
        oh = oh + o[:, :, g * HEAD_DIM:(g + 1) * HEAD_DIM]
    ms = jnp.sum(jnp.sum(oh * oh, axis=2, keepdims=True), axis=1, keepdims=True) / ATTN_DIM
    o_ref[...] = (oh * lax.rsqrt(ms + RMS_EPS) * g_ref[...]).astype(BF16)
    row = lax.broadcasted_iota(jnp.int32, (1, WINDOW, 1), 1)
    nk_ref[...] = jnp.where(row == WINDOW - 1, kn, pltpu.roll(ck, WINDOW - 1, axis=1))
    nv_ref[...] = jnp.where(row == WINDOW - 1, vn, pltpu.roll(cv, WINDOW - 1, axis=1))


def _attn_decode(q, k_new, v_new, cache_k, cache_v, sinks, norm_g):
    B = q.shape[0]
    tb = DEC_TILE_B

    def b3(d1, d2):
        return pl.BlockSpec((tb, d1, d2), lambda i: (i, 0, 0))

    return pl.pallas_call(
        _attn_dec_kernel,
        grid=(B // tb,),
        in_specs=[
            b3(N_HEADS, HEAD_DIM), b3(1, KV_DIM), b3(1, KV_DIM), b3(WINDOW, KV_DIM), b3(WINDOW, KV_DIM),
            pl.BlockSpec((1, N_HEADS, 1), lambda i: (0, 0, 0)),
            pl.BlockSpec((1, N_HEADS, HEAD_DIM), lambda i: (0, 0, 0)),
        ],
        out_specs=[b3(N_HEADS, HEAD_DIM), b3(WINDOW, KV_DIM), b3(WINDOW, KV_DIM)],
        out_shape=[
            jax.ShapeDtypeStruct((B, N_HEADS, HEAD_DIM), BF16),
            jax.ShapeDtypeStruct((B, WINDOW, KV_DIM), F32),
            jax.ShapeDtypeStruct((B, WINDOW, KV_DIM), F32),
        ],
        compiler_params=_params(1),
        name="attn_dec",
    )(q.reshape(B, N_HEADS, HEAD_DIM), k_new.reshape(B, 1, KV_DIM), v_new.reshape(B, 1, KV_DIM),
      cache_k.reshape(B, WINDOW, KV_DIM), cache_v.reshape(B, WINDOW, KV_DIM),
      sinks.reshape(1, N_HEADS, 1), norm_g.reshape(1, N_HEADS, HEAD_DIM))


def _mix_tail(attn_ref, gb_ref, uc, um1, um2, cw_ref, gconv_ref, wout_ref, bout_ref, x_ref, gate1_ref, ln1g_ref, ln1b_ref,
              shift2_ref, scale2_ref, wr_ref, br_ref, x1_ref, u2_ref, lg_ref):
    tm = uc.shape[0]
    n_split = max(tm // MIX_CHAIN_M, 1)
    h = tm // n_split
    cw = cw_ref[...]
    for s in range(n_split):
        r = slice(s * h, (s + 1) * h)

        def rows(ref):
            return ref[r, :] if ref.shape[0] == tm else ref[...]

        conv = cw[0:1, :] * um2[r] + cw[1:2, :] * um1[r] + cw[2:3, :] * uc[r]
        conv_n = _rms(gb_ref[r, :] * conv, gconv_ref[...]).astype(BF16)
        mixed = (jnp.dot(attn_ref[r, :], wout_ref[0:ATTN_DIM, :], preferred_element_type=F32)
                 + jnp.dot(conv_n, wout_ref[ATTN_DIM:D_MODEL, :], preferred_element_type=F32) + bout_ref[...])
        x1 = _standardise(DEEPNORM_ALPHA * x_ref[r, :] + rows(gate1_ref) * mixed) * ln1g_ref[...] + ln1b_ref[...]
        x1_ref[r, :] = x1
        u2 = _standardise(x1) * (1.0 + rows(scale2_ref)) + rows(shift2_ref)
        u2_ref[r, :] = u2
        lg_ref[:, r] = lax.dot_general(wr_ref[...], u2.astype(BF16), (((1,), (1,)), ((), ())),
                                       preferred_element_type=F32) + br_ref[...]


def _mix_seq_kernel(attn_ref, gb_ref, uc_ref, halo_ref, hist_ref, *rest):
    i = pl.program_id(0)
    uc = uc_ref[...]
    tm = uc.shape[0]
    above = jnp.where(i == 0, hist_ref[...], halo_ref[...])
    row = lax.broadcasted_iota(jnp.int32, (tm, 1), 0)
    um1 = jnp.where(row == 0, above[7:8, :], pltpu.roll(uc, 1, axis=0))
    um2 = jnp.where(row == 0, above[6:7, :], jnp.where(row == 1, above[7:8, :], pltpu.roll(uc, 2, axis=0)))
    _mix_tail(attn_ref, gb_ref, uc, um1, um2, *rest)


def _mix_tok_kernel(attn_ref, gb_ref, uc_ref, um1_ref, um2_ref, *rest):
    _mix_tail(attn_ref, gb_ref, uc_ref[...], um1_ref[...], um2_ref[...], *rest)


def _mix(attn_n, gb, uc, conv_prev, conv_w, norm_conv_g, w_out_bf, b_out, x, gate1, ln1_g, ln1_b, shift2, scale2,
         w_router_t_bf, b_router, tm, sequential):
    T = x.shape[0]
    per_row_mod = gate1.shape[0] != 1

    def row_spec(width):
        return pl.BlockSpec((tm, width), lambda i: (i, 0))

    def const_spec(rows, width):
        return pl.BlockSpec((rows, width), lambda i: (0, 0), pipeline_mode=pl.Buffered(1))

    mod_spec = row_spec(D_MODEL) if per_row_mod else const_spec(1, D_MODEL)
    if sequential:
        hist8 = jnp.concatenate([jnp.zeros((SUBLANES - 2, CONV_DIM), F32), conv_prev], axis=0)
        halo_blocks = tm // SUBLANES
        conv_specs = [pl.BlockSpec((SUBLANES, CONV_DIM), lambda i: (jnp.maximum(i * halo_blocks - 1, 0), 0)),
                      const_spec(SUBLANES, CONV_DIM)]
        conv_args = (uc, hist8)
        body = _mix_seq_kernel
    else:
        conv_specs = [row_spec(CONV_DIM), row_spec(CONV_DIM)]
        conv_args = conv_prev
        body = _mix_tok_kernel
    return pl.pallas_call(
        body,
        grid=(T // tm,),
        in_specs=[row_spec(ATTN_DIM), row_spec(CONV_DIM), row_spec(CONV_DIM), *conv_specs,
                  const_spec(CONV_WIDTH, CONV_DIM), const_spec(1, CONV_DIM),
                  const_spec(D_MODEL, D_MODEL), const_spec(1, D_MODEL),
                  row_spec(D_MODEL), mod_spec, const_spec(1, D_MODEL), const_spec(1, D_MODEL),
                  mod_spec, mod_spec,
                  const_spec(N_EXPERTS, D_MODEL), const_spec(N_EXPERTS, 1)],
        out_specs=[row_spec(D_MODEL), row_spec(D_MODEL), pl.BlockSpec((N_EXPERTS, tm), lambda i: (0, i))],
        out_shape=[jax.ShapeDtypeStruct((T, D_MODEL), F32), jax.ShapeDtypeStruct((T, D_MODEL), F32),
                   jax.ShapeDtypeStruct((N_EXPERTS, T), F32)],
        compiler_params=_params(1),
        name="mix_seq" if sequential else "mix_tok",
    )(attn_n, gb, uc, *conv_args, conv_w, norm_conv_g.reshape(1, CONV_DIM), w_out_bf, b_out.reshape(1, D_MODEL),
      x, gate1, ln1_g.reshape(1, D_MODEL), ln1_b.reshape(1, D_MODEL), shift2, scale2,
      w_router_t_bf, b_router.reshape(N_EXPERTS, 1))


def _route_kernel(lg_ref, eidx_ref, gate_ref, rank_ref, cnt_ref):
    n_tok = lg_ref.shape[1]
    r = lax.broadcasted_iota(jnp.int32, (TOKEN_TILE, TOKEN_TILE), 0)
    c = lax.broadcasted_iota(jnp.int32, (TOKEN_TILE, TOKEN_TILE), 1)
    earlier = (r < c).astype(BF16)
    expert = lax.broadcasted_iota(jnp.int32, (N_EXPERTS, TOKEN_TILE), 0).astype(F32)

    def body(ci, count):
        off = pl.multiple_of(ci * TOKEN_TILE, TOKEN_TILE)
        l = lg_ref[:, pl.ds(off, TOKEN_TILE)]
        vals, idxs, sels = [], [], []
        for _ in range(TOP_K):
            m = jnp.max(l, axis=0, keepdims=True)
            idx = jnp.min(jnp.where(l == m, expert, float(N_EXPERTS)), axis=0, keepdims=True)
            sel = expert == idx
            vals.append(m)
            idxs.append(idx)
            sels.append(sel)
            l = jnp.where(sel, -jnp.inf, l)
        chosen = jnp.where(sels[0] | sels[1] | sels[2] | sels[3], 1.0, 0.0)
        before = jnp.dot(chosen.astype(BF16), earlier, preferred_element_type=F32) + count
        ranks = [jnp.sum(jnp.where(s, before, 0.0), axis=0, keepdims=True) for s in sels]
        es = [jnp.exp(v - vals[0]) for v in vals]
        den = es[0] + es[1] + es[2] + es[3]
        eidx_ref[:, pl.ds(off, TOKEN_TILE)] = jnp.concatenate(idxs, axis=0).astype(jnp.int32)
        rank_ref[:, pl.ds(off, TOKEN_TILE)] = jnp.concatenate(ranks, axis=0).astype(jnp.int32)
        gate_ref[:, pl.ds(off, TOKEN_TILE)] = jnp.concatenate([e / den for e in es], axis=0)
        return count + jnp.sum(chosen, axis=1, keepdims=True)

    count = lax.fori_loop(0, n_tok // TOKEN_TILE, body, jnp.zeros((N_EXPERTS, 1), F32))
    cnt_ref[...] = jnp.broadcast_to(count, (N_EXPERTS, LANES)).astype(jnp.int32)


def _route(logits_t):
    n_tok = logits_t.shape[1]
    return pl.pallas_call(
        _route_kernel,
        out_shape=[jax.ShapeDtypeStruct((TOP_K, n_tok), jnp.int32), jax.ShapeDtypeStruct((TOP_K, n_tok), F32),
                   jax.ShapeDtypeStruct((TOP_K, n_tok), jnp.int32), jax.ShapeDtypeStruct((N_EXPERTS, LANES), jnp.int32)],
        compiler_params=pltpu.CompilerParams(vmem_limit_bytes=VMEM_LIMIT_BYTES),
        name="route",
    )(logits_t)


def _dest_kernel(start_ref, eidx_ref, rank_ref, dest_ref):
    e = eidx_ref[...]
    base = jnp.zeros(e.shape, jnp.int32)
    for x in range(N_EXPERTS):
        base = jnp.where(e == x, start_ref[x], base)
    dest_ref[...] = base + rank_ref[...]


def _dest(group_start, eidx, rank):
    return pl.pallas_call(
        _dest_kernel,
        in_specs=[pl.BlockSpec(memory_space=pltpu.SMEM), pl.BlockSpec(memory_space=pltpu.VMEM),
                  pl.BlockSpec(memory_space=pltpu.VMEM)],
        out_specs=pl.BlockSpec(memory_space=pltpu.VMEM),
        out_shape=jax.ShapeDtypeStruct(eidx.shape, jnp.int32),
        name="dest",
    )(group_start, eidx, rank)


def _dispatch_kernel(n_prompt_tiles, n_tiles, gend_ref, gsize_ref, dest_ref, up_ref, us_ref, xs_ref,
                     zero_buf, tiles, zsem, load_sem, scat_sem):
    i = pl.program_id(0)
    slot = lax.rem(i, DISPATCH_SLOTS)

    def start_load(tile, s):
        @pl.when(tile < n_prompt_tiles)
        def _():
            r = pl.multiple_of(tile * TOKEN_TILE, TOKEN_TILE)
            pltpu.make_async_copy(up_ref.at[pl.ds(r, TOKEN_TILE), :], tiles.at[s], load_sem.at[s]).start()

        @pl.when(tile >= n_prompt_tiles)
        def _():
            r = pl.multiple_of((tile - n_prompt_tiles) * TOKEN_TILE, TOKEN_TILE)
            pltpu.make_async_copy(us_ref.at[pl.ds(r, TOKEN_TILE), :], tiles.at[s], load_sem.at[s]).start()

    def wait_load(s):
        pltpu.make_async_copy(up_ref.at[pl.ds(0, TOKEN_TILE), :], tiles.at[s], load_sem.at[s]).wait()

    def wait_scatter(s):
        for _ in range(TOP_K):
            pltpu.make_async_copy(tiles.at[s], xs_ref.at[pl.ds(0, TOKEN_TILE), :], scat_sem.at[s]).wait()

    @pl.when(i == 0)
    def _():
        start_load(0, 0)
        zero_buf[...] = jnp.zeros_like(zero_buf)
        for e in range(N_EXPERTS):
            @pl.when(gsize_ref[e] > 0)
            def _():
                r0 = pl.multiple_of(gend_ref[e] - ROW_BLOCK, ROW_BLOCK)
                pltpu.make_async_copy(zero_buf, xs_ref.at[pl.ds(r0, ROW_BLOCK), :], zsem).start()
        for e in range(N_EXPERTS):
            @pl.when(gsize_ref[e] > 0)
            def _():
                pltpu.make_async_copy(zero_buf, xs_ref.at[pl.ds(0, ROW_BLOCK), :], zsem).wait()

        first_unused = gend_ref[N_EXPERTS - 1] // ROW_BLOCK
        n_blocks = xs_ref.shape[0] // ROW_BLOCK

        def zero_block(b, carry):
            r0 = pl.multiple_of(b * ROW_BLOCK, ROW_BLOCK)
            pltpu.make_async_copy(zero_buf, xs_ref.at[pl.ds(r0, ROW_BLOCK), :], zsem).start()
            return carry

        def zero_wait(b, carry):
            pltpu.make_async_copy(zero_buf, xs_ref.at[pl.ds(0, ROW_BLOCK), :], zsem).wait()
            return carry

        lax.fori_loop(first_unused, n_blocks, zero_block, 0)
        lax.fori_loop(first_unused, n_blocks, zero_wait, 0)

    @pl.when(i + 1 < n_tiles)
    def _():
        nxt = lax.rem(i + 1, DISPATCH_SLOTS)

        @pl.when(i + 1 >= DISPATCH_SLOTS)
        def _():
            wait_scatter(nxt)
        start_load(i + 1, nxt)

    wait_load(slot)
    for s in range(DISPATCH_SLOTS):
        @pl.when(slot == s)
        def _():
            for j in range(TOP_K * TOKEN_TILE):
                d = dest_ref[0, 0, j]
                pltpu.make_async_copy(tiles.at[s, pl.ds(j % TOKEN_TILE, 1), :], xs_ref.at[pl.ds(d, 1), :],
                                      scat_sem.at[s]).start(priority=j % 2)

    @pl.when(i == n_tiles - 1)
    def _():
        for tile in range(max(n_tiles - DISPATCH_SLOTS, 0), n_tiles):
            wait_scatter(tile % DISPATCH_SLOTS)


def _dispatch(group_end, group_size, dest_tiles, u2_p, u2_s, n_rows):
    n_p = u2_p.shape[0] // TOKEN_TILE
    n_s = u2_s.shape[0] // TOKEN_TILE
    grid_spec = pltpu.PrefetchScalarGridSpec(
        num_scalar_prefetch=2,
        grid=(n_p + n_s,),
        in_specs=[
            pl.BlockSpec((1, 1, TOP_K * TOKEN_TILE), lambda i, *_: (i, 0, 0), memory_space=pltpu.SMEM),
            pl.BlockSpec(memory_space=pl.ANY),
            pl.BlockSpec(memory_space=pl.ANY),
        ],
        out_specs=pl.BlockSpec(memory_space=pl.ANY),
        scratch_shapes=[pltpu.VMEM((ROW_BLOCK, D_MODEL), F32), pltpu.VMEM((DISPATCH_SLOTS, TOKEN_TILE, D_MODEL), F32),
                        pltpu.SemaphoreType.DMA(()), pltpu.SemaphoreType.DMA((DISPATCH_SLOTS,)),
                        pltpu.SemaphoreType.DMA((DISPATCH_SLOTS,))],
    )
    return pl.pallas_call(
        functools.partial(_dispatch_kernel, n_p, n_p + n_s),
        grid_spec=grid_spec,
        out_shape=jax.ShapeDtypeStruct((n_rows, D_MODEL), F32),
        compiler_params=_params(1),
        name="dispatch",
    )(group_end, group_size, dest_tiles, u2_p, u2_s)


_FIRST, _MIDDLE, _LAST = 0, 1, 2


def _experts_kernel(row0_ref, nblk_ref, exp_ref, total_ref, xs_ref, wg_ref, wl_ref, wd_ref, bg_ref, bl_ref, bd_ref,
                    ys_ref, xbuf, acc, in_sem, out_sem):
    i = pl.program_id(0)
    f = pl.program_id(1)
    n_items = pl.num_programs(0)
    last_f = pl.num_programs(1) - 1
    nblk = nblk_ref[i]
    row0 = row0_ref[i]
    nblk_prev = jnp.where(i > 0, nblk_ref[jnp.maximum(i - 1, 0)], 0)
    n_started = jnp.minimum(nblk_prev, nblk)
    nxt = jnp.minimum(i + 1, n_items - 1)
    row0_next = row0_ref[nxt]
    nblk_next = jnp.where(i + 1 < n_items, nblk_ref[nxt], 0)

    def local(j):
        return pl.ds(pl.multiple_of(j * ROW_BLOCK, ROW_BLOCK), ROW_BLOCK)

    def in_hbm(r0, j):
        return pl.ds(pl.multiple_of(r0 + j * ROW_BLOCK, ROW_BLOCK), ROW_BLOCK)

    def x_copy(j, r0=row0):
        return pltpu.make_async_copy(xs_ref.at[in_hbm(r0, j), :], xbuf.at[local(j), :], in_sem.at[j])

    def y_copy(j):
        return pltpu.make_async_copy(acc.at[local(j), :], ys_ref.at[in_hbm(row0, j), :], out_sem.at[j])

    def prefetch_next(j):
        @pl.when(j < nblk_next)
        def _():
            x_copy(j, row0_next).start()

    def body(chunks, phase):
        wg = wg_ref[...].astype(BF16)
        wl = wl_ref[...].astype(BF16)
        wd = wd_ref[...].astype(BF16)
        for j0, m in chunks:
            rows = pl.ds(pl.multiple_of(j0 * ROW_BLOCK, ROW_BLOCK), m * ROW_BLOCK)
            if phase == _FIRST:
                for j in range(m):
                    x_copy(j0 + j).wait()
            x = xbuf[rows, :].astype(BF16)
            glu = jnp.dot(x, wg, preferred_element_type=F32) + bg_ref[...]
            lin = jnp.dot(x, wl, preferred_element_type=F32) + bl_ref[...]
            glu = jnp.minimum(glu, SWIGLU_LIMIT)
            lin = jnp.clip(lin, -SWIGLU_LIMIT, SWIGLU_LIMIT)
            act = glu * jax.nn.sigmoid(SWIGLU_ALPHA * glu) * (lin + 1.0)
            y = jnp.dot(act.astype(BF16), wd, preferred_element_type=F32)
            if phase == _FIRST:
                acc[rows, :] = y + bd_ref[...]
            else:
                acc[rows, :] += y
            if phase == _LAST:
                for j in range(m):
                    y_copy(j0 + j).start()

    def run(phase):
        def before(j0, m):
            if phase == _FIRST:
                for j in range(m):
                    @pl.when(j0 + j < nblk_prev)
                    def _():
                        y_copy(j0 + j).wait()

        def after(j0, m):
            if phase == _LAST:
                for j in range(m):
                    prefetch_next(j0 + j)

        def quad(q, carry):
            before(4 * q, 4)
            body(((4 * q, 2), (4 * q + 2, 2)), phase)
            after(4 * q, 4)
            return carry

        n_quads = lax.shift_right_logical(nblk, 2)
        lax.fori_loop(0, n_quads, quad, 0)
        done = lax.shift_left(n_quads, 2)

        @pl.when(jnp.bitwise_and(nblk, 2) == 2)
        def _():
            before(done, 2)
            body(((done, 1), (done + 1, 1)), phase)
            after(done, 2)

        @pl.when(jnp.bitwise_and(nblk, 1) == 1)
        def _():
            before(nblk - 1, 1)
            body(((nblk - 1, 1),), phase)
            after(nblk - 1, 1)

    def drain(j, carry):
        y_copy(j).wait()
        return carry

    @pl.when(f == 0)
    def _():
        def start(j, carry):
            x_copy(j).start()
            return carry
        lax.fori_loop(n_started, nblk, start, 0)
        run(_FIRST)
        lax.fori_loop(nblk, jnp.where(nblk > 0, nblk_prev, 0), drain, 0)

    @pl.when((f > 0) & (f < last_f))
    def _():
        run(_MIDDLE)

    @pl.when(f == last_f)
    def _():
        run(_LAST)

        @pl.when(nblk_next == 0)
        def _():
            lax.fori_loop(0, nblk, drain, 0)

        @pl.when(i == n_items - 1)
        def _():
            first_unused = total_ref[0] // ROW_BLOCK
            n_blocks = ys_ref.shape[0] // ROW_BLOCK
            acc[local(0), :] = jnp.zeros((ROW_BLOCK, D_MODEL), F32)

            def zero_copy(b):
                r0 = pl.multiple_of(b * ROW_BLOCK, ROW_BLOCK)
                return pltpu.make_async_copy(acc.at[local(0), :], ys_ref.at[pl.ds(r0, ROW_BLOCK), :], out_sem.at[0])

            def zero_block(b, carry):
                zero_copy(b).start()
                return carry

            def zero_wait(b, carry):
                zero_copy(b).wait()
                return carry

            lax.fori_loop(first_unused, n_blocks, zero_block, 0)
            lax.fori_loop(first_unused, n_blocks, zero_wait, 0)


def _experts(items, total_rows, xs, w_up, b_up, w_down, b_down):
    row0, nblk, exp = items
    n_items = row0.shape[0]
    n_rows = xs.shape[0]
    n_f = D_FF // FF_TILE
    assert n_f >= 2

    def ff(i, f, nb):
        return jnp.where(nb[i] > 0, f, n_f - 1)

    max_rows = EXPERT_MAX_BLOCKS * ROW_BLOCK
    grid_spec = pltpu.PrefetchScalarGridSpec(
        num_scalar_prefetch=4,
        grid=(n_items, n_f),
        in_specs=[
            pl.BlockSpec(memory_space=pl.ANY),
            pl.BlockSpec((None, D_MODEL, FF_TILE), lambda i, f, r0, nb, ex, tot: (ex[i], 0, ff(i, f, nb))),
            pl.BlockSpec((None, D_MODEL, FF_TILE), lambda i, f, r0, nb, ex, tot: (ex[i], 0, n_f + ff(i, f, nb))),
            pl.BlockSpec((None, FF_TILE, D_MODEL), lambda i, f, r0, nb, ex, tot: (ex[i], ff(i, f, nb), 0)),
            pl.BlockSpec((None, 1, FF_TILE), lambda i, f, r0, nb, ex, tot: (ex[i], 0, ff(i, f, nb))),
            pl.BlockSpec((None, 1, FF_TILE), lambda i, f, r0, nb, ex, tot: (ex[i], 0, n_f + ff(i, f, nb))),
            pl.BlockSpec((None, 1, D_MODEL), lambda i, f, r0, nb, ex, tot: (ex[i], 0, 0)),
        ],
        out_specs=pl.BlockSpec(memory_space=pl.ANY),
        scratch_shapes=[pltpu.VMEM((max_rows, D_MODEL), F32), pltpu.VMEM((max_rows, D_MODEL), F32),
                        pltpu.SemaphoreType.DMA((EXPERT_MAX_BLOCKS,)), pltpu.SemaphoreType.DMA((EXPERT_MAX_BLOCKS,))],
    )
    return pl.pallas_call(
        _experts_kernel,
        grid_spec=grid_spec,
        out_shape=jax.ShapeDtypeStruct((n_rows, D_MODEL), F32),
        compiler_params=_params(2),
        name="experts",
    )(row0, nblk, exp, total_rows.reshape(1), xs, w_up, w_up, w_down,
      b_up.reshape(N_EXPERTS, 1, 2 * D_FF), b_up.reshape(N_EXPERTS, 1, 2 * D_FF), b_down.reshape(N_EXPERTS, 1, D_MODEL))


def _combine_kernel(n_prompt_tiles, dcur_ref, dnxt_ref, gates_ref, x1p_ref, x1s_ref, g2p_ref, g2s_ref, lng_ref, lnb_ref,
                    ys_ref, yp_ref, ysm_ref, buf, sem):
    i = pl.program_id(0)
    n = pl.num_programs(0)
    slot = i % 2

    def gather(idx_ref, s):
        for j in range(TOP_K * TOKEN_TILE):
            d = idx_ref[0, 0, j]
            pltpu.make_async_copy(ys_ref.at[pl.ds(d, 1), :],
                                  buf.at[s, j // TOKEN_TILE, pl.ds(j % TOKEN_TILE, 1), :], sem.at[s]).start()

    @pl.when(i == 0)
    def _():
        gather(dcur_ref, 0)

    for s in range(2):
        @pl.when((i + 1 < n) & (slot == 1 - s))
        def _():
            gather(dnxt_ref, s)

    for k in range(TOP_K):
        pltpu.make_async_copy(ys_ref.at[pl.ds(0, TOKEN_TILE), :], buf.at[slot, k], sem.at[slot]).wait()

    def finish(x1_ref, g2_ref, out_ref):
        gates = gates_ref[...]
        half = TOKEN_TILE // 2
        for r in (slice(0, half), slice(half, TOKEN_TILE)):
            ffn = buf[slot, 0, r, :] * gates[r, 0:1]
            for k in range(1, TOP_K):
                ffn = ffn + buf[slot, k, r, :] * gates[r, k:k + 1]
            gate2 = g2_ref[r, :] if g2_ref.shape[0] == TOKEN_TILE else g2_ref[...]
            out_ref[r, :] = (_standardise(DEEPNORM_ALPHA * x1_ref[r, :] + gate2 * ffn) * lng_ref[...]
                             + lnb_ref[...])

    @pl.when(i < n_prompt_tiles)
    def _():
        finish(x1p_ref, g2p_ref, yp_ref)

    @pl.when(i >= n_prompt_tiles)
    def _():
        finish(x1s_ref, g2s_ref, ysm_ref)


def _combine(dest_tiles, gates_tok, x1_p, x1_s, gate2_p, gate2_s, ln2_g, ln2_b, ys):
    n_p = x1_p.shape[0] // TOKEN_TILE
    n_s = x1_s.shape[0] // TOKEN_TILE
    n = n_p + n_s

    def p_idx(i):
        return jnp.minimum(i, n_p - 1)

    def s_idx(i):
        return jnp.maximum(i - n_p, 0)

    smem_tile = (1, 1, TOP_K * TOKEN_TILE)
    return pl.pallas_call(
        functools.partial(_combine_kernel, n_p),
        grid=(n,),
        in_specs=[
            pl.BlockSpec(smem_tile, lambda i: (i, 0, 0), memory_space=pltpu.SMEM),
            pl.BlockSpec(smem_tile, lambda i: (jnp.minimum(i + 1, n - 1), 0, 0), memory_space=pltpu.SMEM),
            pl.BlockSpec((TOKEN_TILE, TOP_K), lambda i: (i, 0)),
            pl.BlockSpec((TOKEN_TILE, D_MODEL), lambda i: (p_idx(i), 0)),
            pl.BlockSpec((TOKEN_TILE, D_MODEL), lambda i: (s_idx(i), 0)),
            pl.BlockSpec((1, D_MODEL), lambda i: (0, 0)),
            pl.BlockSpec((TOKEN_TILE, D_MODEL), lambda i: (s_idx(i), 0)),
            pl.BlockSpec((1, D_MODEL), lambda i: (0, 0)),
            pl.BlockSpec((1, D_MODEL), lambda i: (0, 0)),
            pl.BlockSpec(memory_space=pl.ANY),
        ],
        out_specs=[pl.BlockSpec((TOKEN_TILE, D_MODEL), lambda i: (p_idx(i), 0)),
                   pl.BlockSpec((TOKEN_TILE, D_MODEL), lambda i: (s_idx(i), 0))],
        out_shape=[jax.ShapeDtypeStruct(x1_p.shape, F32), jax.ShapeDtypeStruct(x1_s.shape, F32)],
        scratch_shapes=[pltpu.VMEM((2, TOP_K, TOKEN_TILE, D_MODEL), F32), pltpu.SemaphoreType.DMA((2,))],
        compiler_params=_params(1),
        name="combine",
    )(dest_tiles, dest_tiles, gates_tok, x1_p, x1_s, gate2_p, gate2_s,
      ln2_g.reshape(1, D_MODEL), ln2_b.reshape(1, D_MODEL), ys)


def _rope_tables(pos):
    half = HEAD_DIM // 2
    inv_freq = ROPE_THETA ** (-jnp.arange(half, dtype=F32) / half)
    ang = pos.astype(F32)[:, None] * inv_freq
    cos, sin = jnp.cos(ang), jnp.sin(ang)
    reps = LANES // HEAD_DIM
    return (jnp.tile(jnp.concatenate([cos, cos], axis=1), (1, reps)),
            jnp.tile(jnp.concatenate([-sin, sin], axis=1), (1, reps)))


def _work_items(group_start, group_size, n_items):
    chunk_rows = EXPERT_MAX_BLOCKS * ROW_BLOCK
    ex = jnp.arange(N_EXPERTS, dtype=jnp.int32)
    n_chunks = (group_size + chunk_rows - 1) // chunk_rows
    chunk_end = jnp.sum(jnp.where(ex[None, :] <= ex[:, None], n_chunks[None, :], 0), axis=1)
    chunk_start = chunk_end - n_chunks
    item = jnp.arange(n_items, dtype=jnp.int32)
    live = item < chunk_end[-1]
    it = jnp.minimum(item, chunk_end[-1] - 1)
    exp = jnp.minimum(jnp.sum((chunk_end[None, :] <= it[:, None]).astype(jnp.int32), axis=1), N_EXPERTS - 1)
    own = exp[:, None] == ex[None, :]

    def pick(per_expert):
        return jnp.sum(jnp.where(own, per_expert[None, :], 0), axis=1)

    chunk = it - pick(chunk_start)
    blocks = pick(group_size) // ROW_BLOCK
    per_chunk = (blocks + jnp.maximum(pick(n_chunks), 1) - 1) // jnp.maximum(pick(n_chunks), 1)
    row0 = pick(group_start) + chunk * per_chunk * ROW_BLOCK
    nblk = jnp.clip(blocks - chunk * per_chunk, 0, per_chunk)
    nblk = jnp.where(live, nblk, 0)
    return row0.astype(jnp.int32), nblk.astype(jnp.int32), exp.astype(jnp.int32)


def _moe(u2_p, u2_s, logits_t, x1_p, x1_s, gate2_p, gate2_s, ln2_g, ln2_b, w_up, b_up, w_down, b_down):
    n_tok = logits_t.shape[1]
    eidx, gates, rank, counts = _route(logits_t)
    count = counts[:, 0]
    group_size = (count + ROW_BLOCK - 1) // ROW_BLOCK * ROW_BLOCK
    ex = jnp.arange(N_EXPERTS, dtype=jnp.int32)
    group_end = jnp.sum(jnp.where(ex[None, :] <= ex[:, None], group_size[None, :], 0), axis=1)
    group_start = group_end - group_size
    dest = _dest(group_start, eidx, rank)
    n_tiles = n_tok // TOKEN_TILE
    dest_tiles = dest.reshape(TOP_K, n_tiles, TOKEN_TILE).transpose(1, 0, 2).reshape(n_tiles, 1, TOP_K * TOKEN_TILE)
    max_rows = n_tok * TOP_K + N_EXPERTS * (ROW_BLOCK - 1)
    n_rows = (max_rows + ROW_BLOCK - 1) // ROW_BLOCK * ROW_BLOCK
    n_items = N_EXPERTS + max_rows // (EXPERT_MAX_BLOCKS * ROW_BLOCK)
    xs = _dispatch(group_end, group_size, dest_tiles, u2_p, u2_s, n_rows)
    ys = _experts(_work_items(group_start, group_size, n_items), group_end[-1], xs, w_up, b_up, w_down, b_down)
    return _combine(dest_tiles, gates.T, x1_p, x1_s, gate2_p, gate2_s, ln2_g, ln2_b, ys)


def _layer(x_p, x_s, cache_k, cache_v, state_conv, c_p, c_s,
           w_ada, b_ada, w_in, b_in, conv_w, sinks, norm_attn_g, norm_conv_g, w_out, b_out,
           ln1_g, ln1_b, w_router, b_router, w_up, b_up, w_down, b_down, ln2_g, ln2_b, past_len):
    T = x_p.shape[0]
    B = x_s.shape[0]
    n_c = 1 + B
    pad_c = (-n_c) % SUBLANES
    c_all = jnp.concatenate([c_p, c_s, jnp.zeros((pad_c, D_MODEL), F32)], axis=0)
    mod = _ada(c_all, w_ada, b_ada)
    shift1, scale1, gate1, shift2, scale2, gate2 = [mod[:, j * D_MODEL:(j + 1) * D_MODEL] for j in range(6)]

    def prompt(a):
        return a[0:1]

    def sample(a):
        return a[1:n_c]

    w_in_bf = w_in.astype(BF16)
    w_out_bf = w_out.astype(BF16)
    w_router_t_bf = w_router.T.astype(BF16)

    cos_p, sin_p = _rope_tables(jnp.arange(T, dtype=jnp.int32))
    q_p, k_p, v_p, gb_p, uc_p = _inproj(x_p, prompt(shift1), prompt(scale1), cos_p, sin_p, w_in_bf, b_in,
                                        min(INPROJ_TILE_M, T))
    attn_p = _attn_prompt(q_p, k_p, v_p, sinks, norm_attn_g)
    x1_p, u2_p, lg_p = _mix(attn_p, gb_p, uc_p, jnp.zeros((CONV_WIDTH - 1, CONV_DIM), F32), conv_w, norm_conv_g,
                            w_out_bf, b_out, x_p, prompt(gate1), ln1_g, ln1_b, prompt(shift2), prompt(scale2),
                            w_router_t_bf, b_router, min(MIX_TILE_M, T), True)

    cos_s, sin_s = _rope_tables(jnp.full((1,), past_len, jnp.int32))
    q_s, k_s, v_s, gb_s, uc_s = _inproj(x_s, sample(shift1), sample(scale1), cos_s, sin_s, w_in_bf, b_in, B)
    attn_s, new_k_s, new_v_s = _attn_decode(q_s, k_s, v_s, cache_k, cache_v, sinks, norm_attn_g)
    x1_s, u2_s, lg_s = _mix(attn_s.reshape(B, ATTN_DIM), gb_s, uc_s, (state_conv[:, 1], state_conv[:, 0]), conv_w,
                            norm_conv_g, w_out_bf, b_out, x_s, sample(gate1), ln1_g, ln1_b, sample(shift2),
                            sample(scale2), w_router_t_bf, b_router, B, False)

    y_p, y_s = _moe(u2_p, u2_s, jnp.concatenate([lg_p, lg_s], axis=1), x1_p, x1_s, prompt(gate2), sample(gate2),
                    ln2_g, ln2_b, w_up, b_up, w_down, b_down)

    new_k_p = k_p[T - WINDOW:].reshape(WINDOW, N_KV_HEADS, HEAD_DIM)
    new_v_p = v_p[T - WINDOW:].reshape(WINDOW, N_KV_HEADS, HEAD_DIM)
    new_conv_p = uc_p[T - (CONV_WIDTH - 1):]
    new_conv_s = jnp.stack([state_conv[:, 1], uc_s], axis=1)
    return (y_p, y_s, new_k_p, new_v_p, new_conv_p,
            new_k_s.reshape(B, WINDOW, N_KV_HEADS, HEAD_DIM), new_v_s.reshape(B, WINDOW, N_KV_HEADS, HEAD_DIM), new_conv_s)


def kernel(x_prompt, x_sample, cache_k, cache_v, state_conv, c_prompt, c_sample, w_ada, b_ada, w_in, b_in, conv_w, sinks, norm_attn_g, norm_conv_g, w_out, b_out, ln1_g, ln1_b, w_router, b_router, w_up, b_up, w_down, b_down, ln2_g, ln2_b):
    assert x_prompt.shape[0] == 1 and x_sample.shape[1] == 1 and w_ada.shape[0] == DEPTH == 1
    B = x_sample.shape[0]
    (y_p, y_s, nk_p, nv_p, nc_p, nk_s, nv_s, nc_s) = _layer(
        x_prompt[0], x_sample[:, 0], cache_k[0], cache_v[0], state_conv[0], c_prompt, c_sample,
        w_ada[0], b_ada[0], w_in[0], b_in[0], conv_w[0], sinks[0], norm_attn_g[0], norm_conv_g[0], w_out[0], b_out[0],
        ln1_g[0], ln1_b[0], w_router[0], b_router[0], w_up[0], b_up[0], w_down[0], b_down[0], ln2_g[0], ln2_b[0],
        PAST_LEN)
    return (y_p[None], y_s.reshape(B, 1, D_MODEL), nk_p[None, None], nv_p[None, None], nc_p[None, None],
            nk_s[None], nv_s[None], nc_s[None])
```

```python
import functools

import jax
import jax.numpy as jnp
from jax import lax
from jax.experimental import pallas as pl
from jax.experimental.pallas import tpu as pltpu

F32 = jnp.float32
BF16 = jnp.bfloat16

D_MODEL = 2048
HEAD_DIM = 64
N_HEADS = 16
N_KV_HEADS = 4
GQA_GROUP = N_HEADS // N_KV_HEADS
ATTN_DIM = N_HEADS * HEAD_DIM
KV_DIM = N_KV_HEADS * HEAD_DIM
CONV_DIM = D_MODEL - ATTN_DIM
CONV_WIDTH = 3
IN_DIM = ATTN_DIM + 2 * KV_DIM + 3 * CONV_DIM
WINDOW = 128
PAST_LEN = 16384
ROPE_THETA = 10000.0
N_EXPERTS = 32
TOP_K = 4
D_FF = D_MODEL
SWIGLU_LIMIT = 7.0
SWIGLU_ALPHA = 1.702
DEPTH = 1
DEEPNORM_ALPHA = (2.0 * DEPTH) ** 0.25
LN_EPS = 1e-5
RMS_EPS = 1e-6
COL_K = ATTN_DIM
COL_V = COL_K + KV_DIM
COL_B = COL_V + KV_DIM
COL_C = COL_B + CONV_DIM
COL_X = COL_C + CONV_DIM

LANES = 128
SUBLANES = 8
VMEM_LIMIT_BYTES = 60 * 1024 * 1024

TOKEN_TILE = 128
DISPATCH_SLOTS = 3
ADA_TILE_N = 1024
INPROJ_TILE_M = 512
INPROJ_CHUNK_N = 512
ATTN_BLOCKS = 4
MIX_TILE_M = 512
MIX_CHAIN_M = 256
DEC_TILE_B = 16
ROW_BLOCK = 256
EXPERT_MAX_BLOCKS = 6
FF_TILE = 512
NEG_BIG = -1e30


def _params(n_axes, vmem=VMEM_LIMIT_BYTES):
    return pltpu.CompilerParams(dimension_semantics=("arbitrary",) * n_axes, vmem_limit_bytes=vmem)


def _standardise(x):
    mu = jnp.mean(x, axis=-1, keepdims=True)
    xc = x - mu
    var = jnp.mean(xc * xc, axis=-1, keepdims=True)
    return xc * lax.rsqrt(var + LN_EPS)


def _rms(x, g):
    return x * lax.rsqrt(jnp.mean(x * x, axis=-1, keepdims=True) + RMS_EPS) * g


def _ada_kernel(c_ref, w_ref, b_ref, o_ref):
    c = c_ref[...]
    s = (c * jax.nn.sigmoid(c)).astype(BF16)
    o_ref[...] = jnp.dot(s, w_ref[...].astype(BF16), preferred_element_type=F32) + b_ref[...]


def _ada(c_all, w_ada, b_ada):
    rows = c_all.shape[0]
    n_out = w_ada.shape[1]
    return pl.pallas_call(
        _ada_kernel,
        grid=(n_out // ADA_TILE_N,),
        in_specs=[
            pl.BlockSpec((rows, D_MODEL), lambda j: (0, 0)),
            pl.BlockSpec((D_MODEL, ADA_TILE_N), lambda j: (0, j)),
            pl.BlockSpec((1, ADA_TILE_N), lambda j: (0, j)),
        ],
        out_specs=pl.BlockSpec((rows, ADA_TILE_N), lambda j: (0, j)),
        out_shape=jax.ShapeDtypeStruct((rows, n_out), F32),
        compiler_params=_params(1),
        name="ada",
    )(c_all, w_ada, b_ada.reshape(1, n_out))


def _inproj_kernel(x_ref, shift_ref, scale_ref, cos_ref, sin_ref, w_ref, b_ref,
                   q_ref, k_ref, v_ref, gb_ref, uc_ref):
    u = (_standardise(x_ref[...]) * (1.0 + scale_ref[...]) + shift_ref[...]).astype(BF16)
    cos = cos_ref[...]
    sin = sin_ref[...]
    lane = lax.broadcasted_iota(jnp.int32, (1, LANES), 1)
    first_half = (lane % HEAD_DIM) < (HEAD_DIM // 2)

    def rope(z):
        partner = jnp.where(first_half, pltpu.roll(z, LANES - HEAD_DIM // 2, axis=1),
                            pltpu.roll(z, HEAD_DIM // 2, axis=1))
        return z * cos + partner * sin

    def proj(c0):
        w = w_ref[:, c0:c0 + INPROJ_CHUNK_N]
        return jnp.dot(u, w, preferred_element_type=F32) + b_ref[:, c0:c0 + INPROJ_CHUNK_N]

    groups = INPROJ_CHUNK_N // LANES
    for j in range(ATTN_DIM // INPROJ_CHUNK_N):
        z = proj(j * INPROJ_CHUNK_N)
        for g in range(groups):
            c0 = j * INPROJ_CHUNK_N + g * LANES
            q_ref[:, c0:c0 + LANES] = (rope(z[:, g * LANES:(g + 1) * LANES]) * (HEAD_DIM ** -0.5)).astype(BF16)
    z = proj(COL_K)
    for g in range(KV_DIM // LANES):
        k_ref[:, g * LANES:(g + 1) * LANES] = rope(z[:, g * LANES:(g + 1) * LANES])
    v_ref[...] = z[:, KV_DIM:2 * KV_DIM]
    for j in range(CONV_DIM // INPROJ_CHUNK_N):
        sl = slice(j * INPROJ_CHUNK_N, (j + 1) * INPROJ_CHUNK_N)
        gb_ref[:, sl] = proj(COL_B + j * INPROJ_CHUNK_N)
        uc_ref[:, sl] = proj(COL_C + j * INPROJ_CHUNK_N) * proj(COL_X + j * INPROJ_CHUNK_N)


def _inproj(x, shift, scale, cos, sin, w_in_bf, b_in, tm):
    T = x.shape[0]
    per_row_mod = shift.shape[0] != 1
    per_row_pos = cos.shape[0] != 1
    mod_spec = pl.BlockSpec((tm, D_MODEL), lambda i: (i, 0)) if per_row_mod else pl.BlockSpec((1, D_MODEL), lambda i: (0, 0))
    pos_spec = pl.BlockSpec((tm, LANES), lambda i: (i, 0)) if per_row_pos else pl.BlockSpec((1, LANES), lambda i: (0, 0))

    def row_spec(width):
        return pl.BlockSpec((tm, width), lambda i: (i, 0))

    return pl.pallas_call(
        _inproj_kernel,
        grid=(T // tm,),
        in_specs=[
            row_spec(D_MODEL), mod_spec, mod_spec, pos_spec, pos_spec,
            pl.BlockSpec((D_MODEL, IN_DIM), lambda i: (0, 0), pipeline_mode=pl.Buffered(1)),
            pl.BlockSpec((1, IN_DIM), lambda i: (0, 0)),
        ],
        out_specs=[row_spec(ATTN_DIM), row_spec(KV_DIM), row_spec(KV_DIM), row_spec(CONV_DIM), row_spec(CONV_DIM)],
        out_shape=[
            jax.ShapeDtypeStruct((T, ATTN_DIM), BF16),
            jax.ShapeDtypeStruct((T, KV_DIM), F32),
            jax.ShapeDtypeStruct((T, KV_DIM), F32),
            jax.ShapeDtypeStruct((T, CONV_DIM), F32),
            jax.ShapeDtypeStruct((T, CONV_DIM), F32),
        ],
        compiler_params=_params(1),
        name="inproj",
    )(x, shift, scale, cos, sin, w_in_bf, b_in.reshape(1, IN_DIM))


def _attn_kernel(q_ref, kp_ref, kc_ref, vp_ref, vc_ref, bias_ref, sink_ref, g_ref, o_ref):
    n = pl.program_id(0)
    n_blocks = q_ref.shape[0] // WINDOW
    k = jnp.concatenate([kp_ref[...], kc_ref[...]], axis=0).astype(BF16)
    v = jnp.concatenate([vp_ref[...], vc_ref[...]], axis=0).astype(BF16)
    for b in range(n_blocks):
        q = q_ref[b * WINDOW:(b + 1) * WINDOW, :]
        kb = k[b * WINDOW:(b + 2) * WINDOW, :]
        vb = v[b * WINDOW:(b + 2) * WINDOW, :]
        bias = bias_ref[jnp.minimum(n, 1)] if b == 0 else bias_ref[1]
        outs = []
        for g in range(N_KV_HEADS):
            heads = range(g * GQA_GROUP, (g + 1) * GQA_GROUP)
            qg = jnp.concatenate([q[:, h * HEAD_DIM:(h + 1) * HEAD_DIM] for h in heads], axis=0)
            kg = kb[:, g * HEAD_DIM:(g + 1) * HEAD_DIM]
            vg = vb[:, g * HEAD_DIM:(g + 1) * HEAD_DIM]
            sink = sink_ref[g]
            s = lax.dot_general(kg, qg, (((1,), (1,)), ((), ())), preferred_element_type=F32) + bias
            m = jnp.maximum(jnp.max(s, axis=0, keepdims=True), sink)
            e = jnp.exp(s - m)
            den = jnp.sum(e, axis=0, keepdims=True) + jnp.exp(sink - m)
            o = lax.dot_general(vg, e.astype(BF16), (((0,), (0,)), ((), ())), preferred_element_type=F32)
            o = o / den
            outs.extend(o[:, j * WINDOW:(j + 1) * WINDOW] for j in range(GQA_GROUP))
        o = jnp.concatenate(outs, axis=0).T
        o_ref[b * WINDOW:(b + 1) * WINDOW, :] = _rms(o, g_ref[...]).astype(BF16)


def _attn_mask_bias():
    qi = jnp.arange(GQA_GROUP * WINDOW, dtype=jnp.int32)[None, :] % WINDOW
    ks = jnp.arange(2 * WINDOW, dtype=jnp.int32)[:, None]
    later = (ks >= qi) & (ks <= qi + WINDOW)
    first = later & (ks >= WINDOW)
    return jnp.where(jnp.stack([first, later]), 0.0, NEG_BIG).astype(F32)


def _attn_prompt(q, k, v, sinks, norm_g):
    T = q.shape[0]
    nb = ATTN_BLOCKS if T % (ATTN_BLOCKS * WINDOW) == 0 else 1

    def cur(width):
        return pl.BlockSpec((nb * WINDOW, width), lambda n: (n, 0))

    def prev(width):
        return pl.BlockSpec((WINDOW, width), lambda n: (jnp.maximum(n * nb - 1, 0), 0))

    return pl.pallas_call(
        _attn_kernel,
        grid=(T // (nb * WINDOW),),
        in_specs=[
            cur(ATTN_DIM), prev(KV_DIM), cur(KV_DIM), prev(KV_DIM), cur(KV_DIM),
            pl.BlockSpec((2, 2 * WINDOW, GQA_GROUP * WINDOW), lambda n: (0, 0, 0)),
            pl.BlockSpec((N_KV_HEADS, 1, GQA_GROUP * WINDOW), lambda n: (0, 0, 0)),
            pl.BlockSpec((1, ATTN_DIM), lambda n: (0, 0)),
        ],
        out_specs=cur(ATTN_DIM),
        out_shape=jax.ShapeDtypeStruct((T, ATTN_DIM), BF16),
        compiler_params=_params(1),
        name="attn",
    )(q, k, k, v, v, _attn_mask_bias(),
      jnp.repeat(sinks, WINDOW).reshape(N_KV_HEADS, 1, GQA_GROUP * WINDOW),
      norm_g.reshape(1, ATTN_DIM))


def _attn_dec_kernel(q_ref, kn_ref, vn_ref, ck_ref, cv_ref, sink_ref, g_ref, o_ref, nk_ref, nv_ref):
    tb = q_ref.shape[0]
    q = q_ref[...].astype(F32)
    lane_group = lax.broadcasted_iota(jnp.int32, (1, N_HEADS, KV_DIM), 2) // HEAD_DIM
    head_group = lax.broadcasted_iota(jnp.int32, (1, N_HEADS, KV_DIM), 1) // GQA_GROUP
    own = lane_group == head_group
    qe = jnp.where(own, jnp.concatenate([q] * N_KV_HEADS, axis=2), 0.0)
    ck = ck_ref[...]
    cv = cv_ref[...]
    kn = kn_ref[...]
    vn = vn_ref[...]
    s = jnp.einsum("bhc,bwc->bhw", qe.astype(BF16), ck.astype(BF16), preferred_element_type=F32)
    s_new = jnp.sum(qe.astype(BF16).astype(F32) * kn.astype(BF16).astype(F32), axis=2, keepdims=True)
    sink = sink_ref[...]
    m = jnp.maximum(jnp.maximum(jnp.max(s, axis=2, keepdims=True), s_new), sink)
    e = jnp.exp(s - m)
    e_new = jnp.exp(s_new - m)
    den = jnp.sum(e, axis=2, keepdims=True) + e_new + jnp.exp(sink - m)
    p = (e / den).astype(BF16)
    p_new = (e_new / den).astype(BF16).astype(F32)
    o = jnp.einsum("bhw,bwc->bhc", p, cv.astype(BF16), preferred_element_type=F32)
    o = o + p_new * vn.astype(BF16).astype(F32)
    o = jnp.where(own, o, 0.0)
    oh = o[:, :, 0:HEAD_DIM]
    for g in range(1, N_KV_HEADS):
        oh = oh + o[:, :, g * HEAD_DIM:(g + 1) * HEAD_DIM]
    ms = jnp.sum(jnp.sum(oh * oh, axis=2, keepdims=True), axis=1, keepdims=True) / ATTN_DIM
    o_ref[...] = (oh * lax.rsqrt(ms + RMS_EPS) * g_ref[...]).astype(BF16)
    row = lax.broadcasted_iota(jnp.int32, (1, WINDOW, 1), 1)
    nk_ref[...] = jnp.where(row == WINDOW - 1, kn, pltpu.roll(ck, WINDOW - 1, axis=1))
    nv_ref[...] = jnp.where(row == WINDOW - 1, vn, pltpu.roll(cv, WINDOW - 1, axis=1))


def _attn_decode(q, k_new, v_new, cache_k, cache_v, sinks, norm_g):
    B = q.shape[0]
    tb = DEC_TILE_B

    def b3(d1, d2):
        return pl.BlockSpec((tb, d1, d2), lambda i: (i, 0, 0))

    return pl.pallas_call(
        _attn_dec_kernel,
        grid=(B // tb,),
        in_specs=[
            b3(N_HEADS, HEAD_DIM), b3(1, KV_DIM), b3(1, KV_DIM), b3(WINDOW, KV_DIM), b3(WINDOW, KV_DIM),
            pl.BlockSpec((1, N_HEADS, 1), lambda i: (0, 0, 0)),
            pl.BlockSpec((1, N_HEADS, HEAD_DIM), lambda i: (0, 0, 0)),
        ],
        out_specs=[b3(N_HEADS, HEAD_DIM), b3(WINDOW, KV_DIM), b3(WINDOW, KV_DIM)],
        out_shape=[
            jax.ShapeDtypeStruct((B, N_HEADS, HEAD_DIM), BF16),
            jax.ShapeDtypeStruct((B, WINDOW, KV_DIM), F32),
            jax.ShapeDtypeStruct((B, WINDOW, KV_DIM), F32),
        ],
        compiler_params=_params(1),
        name="attn_dec",
    )(q.reshape(B, N_HEADS, HEAD_DIM), k_new.reshape(B, 1, KV_DIM), v_new.reshape(B, 1, KV_DIM),
      cache_k.reshape(B, WINDOW, KV_DIM), cache_v.reshape(B, WINDOW, KV_DIM),
      sinks.reshape(1, N_HEADS, 1), norm_g.reshape(1, N_HEADS, HEAD_DIM))


def _mix_tail(attn_ref, gb_ref, uc, um1, um2, cw_ref, gconv_ref, wout_ref, bout_ref, x_ref, gate1_ref, ln1g_ref, ln1b_ref,
              shift2_ref, scale2_ref, wr_ref, br_ref, x1_ref, u2_ref, lg_ref):
    tm = uc.shape[0]
    n_split = max(tm // MIX_CHAIN_M, 1)
    h = tm // n_split
    cw = cw_ref[...]
    for s in range(n_split):
        r = slice(s * h, (s + 1) * h)

        def rows(ref):
            return ref[r, :] if ref.shape[0] == tm else ref[...]

        conv = cw[0:1, :] * um2[r] + cw[1:2, :] * um1[r] + cw[2:3, :] * uc[r]
        conv_n = _rms(gb_ref[r, :] * conv, gconv_ref[...]).astype(BF16)
        mixed = (jnp.dot(attn_ref[r, :], wout_ref[0:ATTN_DIM, :], preferred_element_type=F32)
                 + jnp.dot(conv_n, wout_ref[ATTN_DIM:D_MODEL, :], preferred_element_type=F32) + bout_ref[...])
        x1 = _standardise(DEEPNORM_ALPHA * x_ref[r, :] + rows(gate1_ref) * mixed) * ln1g_ref[...] + ln1b_ref[...]
        x1_ref[r, :] = x1
        u2 = _standardise(x1) * (1.0 + rows(scale2_ref)) + rows(shift2_ref)
        u2_ref[r, :] = u2
        lg_ref[:, r] = lax.dot_general(wr_ref[...], u2.astype(BF16), (((1,), (1,)), ((), ())),
                                       preferred_element_type=F32) + br_ref[...]


def _mix_seq_kernel(attn_ref, gb_ref, uc_ref, halo_ref, hist_ref, *rest):
    i = pl.program_id(0)
    uc = uc_ref[...]
    tm = uc.shape[0]
    above = jnp.where(i == 0, hist_ref[...], halo_ref[...])
    row = lax.broadcasted_iota(jnp.int32, (tm, 1), 0)
    um1 = jnp.where(row == 0, above[7:8, :], pltpu.roll(uc, 1, axis=0))
    um2 = jnp.where(row == 0, above[6:7, :], jnp.where(row == 1, above[7:8, :], pltpu.roll(uc, 2, axis=0)))
    _mix_tail(attn_ref, gb_ref, uc, um1, um2, *rest)


def _mix_tok_kernel(attn_ref, gb_ref, uc_ref, um1_ref, um2_ref, *rest):
    _mix_tail(attn_ref, gb_ref, uc_ref[...], um1_ref[...], um2_ref[...], *rest)


def _mix(attn_n, gb, uc, conv_prev, conv_w, norm_conv_g, w_out_bf, b_out, x, gate1, ln1_g, ln1_b, shift2, scale2,
         w_router_t_bf, b_router, tm, sequential):
    T = x.shape[0]
    per_row_mod = gate1.shape[0] != 1

    def row_spec(width):
        return pl.BlockSpec((tm, width), lambda i: (i, 0))

    def const_spec(rows, width):
        return pl.BlockSpec((rows, width), lambda i: (0, 0), pipeline_mode=pl.Buffered(1))

    mod_spec = row_spec(D_MODEL) if per_row_mod else const_spec(1, D_MODEL)
    if sequential:
        hist8 = jnp.concatenate([jnp.zeros((SUBLANES - 2, CONV_DIM), F32), conv_prev], axis=0)
        halo_blocks = tm // SUBLANES
        conv_specs = [pl.BlockSpec((SUBLANES, CONV_DIM), lambda i: (jnp.maximum(i * halo_blocks - 1, 0), 0)),
                      const_spec(SUBLANES, CONV_DIM)]
        conv_args = (uc, hist8)
        body = _mix_seq_kernel
    else:
        conv_specs = [row_spec(CONV_DIM), row_spec(CONV_DIM)]
        conv_args = conv_prev
        body = _mix_tok_kernel
    return pl.pallas_call(
        body,
        grid=(T // tm,),
        in_specs=[row_spec(ATTN_DIM), row_spec(CONV_DIM), row_spec(CONV_DIM), *conv_specs,
                  const_spec(CONV_WIDTH, CONV_DIM), const_spec(1, CONV_DIM),
                  const_spec(D_MODEL, D_MODEL), const_spec(1, D_MODEL),
                  row_spec(D_MODEL), mod_spec, const_spec(1, D_MODEL), const_spec(1, D_MODEL),
                  mod_spec, mod_spec,
                  const_spec(N_EXPERTS, D_MODEL), const_spec(N_EXPERTS, 1)],
        out_specs=[row_spec(D_MODEL), row_spec(D_MODEL), pl.BlockSpec((N_EXPERTS, tm), lambda i: (0, i))],
        out_shape=[jax.ShapeDtypeStruct((T, D_MODEL), F32), jax.ShapeDtypeStruct((T, D_MODEL), F32),
                   jax.ShapeDtypeStruct((N_EXPERTS, T), F32)],
        compiler_params=_params(1),
        name="mix_seq" if sequential else "mix_tok",
    )(attn_n, gb, uc, *conv_args, conv_w, norm_conv_g.reshape(1, CONV_DIM), w_out_bf, b_out.reshape(1, D_MODEL),
      x, gate1, ln1_g.reshape(1, D_MODEL), ln1_b.reshape(1, D_MODEL), shift2, scale2,
      w_router_t_bf, b_router.reshape(N_EXPERTS, 1))


def _route_kernel(lg_ref, eidx_ref, gate_ref, rank_ref, cnt_ref):
    n_tok = lg_ref.shape[1]
    r = lax.broadcasted_iota(jnp.int32, (TOKEN_TILE, TOKEN_TILE), 0)
    c = lax.broadcasted_iota(jnp.int32, (TOKEN_TILE, TOKEN_TILE), 1)
    earlier = (r < c).astype(BF16)
    expert = lax.broadcasted_iota(jnp.int32, (N_EXPERTS, TOKEN_TILE), 0).astype(F32)

    def body(ci, count):
        off = pl.multiple_of(ci * TOKEN_TILE, TOKEN_TILE)
        l = lg_ref[:, pl.ds(off, TOKEN_TILE)]
        vals, idxs, sels = [], [], []
        for _ in range(TOP_K):
            m = jnp.max(l, axis=0, keepdims=True)
            idx = jnp.min(jnp.where(l == m, expert, float(N_EXPERTS)), axis=0, keepdims=True)
            sel = expert == idx
            vals.append(m)
            idxs.append(idx)
            sels.append(sel)
            l = jnp.where(sel, -jnp.inf, l)
        chosen = jnp.where(sels[0] | sels[1] | sels[2] | sels[3], 1.0, 0.0)
        before = jnp.dot(chosen.astype(BF16), earlier, preferred_element_type=F32) + count
        ranks = [jnp.sum(jnp.where(s, before, 0.0), axis=0, keepdims=True) for s in sels]
        es = [jnp.exp(v - vals[0]) for v in vals]
        den = es[0] + es[1] + es[2] + es[3]
        eidx_ref[:, pl.ds(off, TOKEN_TILE)] = jnp.concatenate(idxs, axis=0).astype(jnp.int32)
        rank_ref[:, pl.ds(off, TOKEN_TILE)] = jnp.concatenate(ranks, axis=0).astype(jnp.int32)
        gate_ref[:, pl.ds(off, TOKEN_TILE)] = jnp.concatenate([e / den for e in es], axis=0)
        return count + jnp.sum(chosen, axis=1, keepdims=True)

    count = lax.fori_loop(0, n_tok // TOKEN_TILE, body, jnp.zeros((N_EXPERTS, 1), F32))
    cnt_ref[...] = jnp.broadcast_to(count, (N_EXPERTS, LANES)).astype(jnp.int32)


def _route(logits_t):
    n_tok = logits_t.shape[1]
    return pl.pallas_call(
        _route_kernel,
        out_shape=[jax.ShapeDtypeStruct((TOP_K, n_tok), jnp.int32), jax.ShapeDtypeStruct((TOP_K, n_tok), F32),
                   jax.ShapeDtypeStruct((TOP_K, n_tok), jnp.int32), jax.ShapeDtypeStruct((N_EXPERTS, LANES), jnp.int32)],
        compiler_params=pltpu.CompilerParams(vmem_limit_bytes=VMEM_LIMIT_BYTES),
        name="route",
    )(logits_t)


def _dest_kernel(start_ref, eidx_ref, rank_ref, dest_ref):
    e = eidx_ref[...]
    base = jnp.zeros(e.shape, jnp.int32)
    for x in range(N_EXPERTS):
        base = jnp.where(e == x, start_ref[x], base)
    dest_ref[...] = base + rank_ref[...]


def _dest(group_start, eidx, rank):
    return pl.pallas_call(
        _dest_kernel,
        in_specs=[pl.BlockSpec(memory_space=pltpu.SMEM), pl.BlockSpec(memory_space=pltpu.VMEM),
                  pl.BlockSpec(memory_space=pltpu.VMEM)],
        out_specs=pl.BlockSpec(memory_space=pltpu.VMEM),
        out_shape=jax.ShapeDtypeStruct(eidx.shape, jnp.int32),
        name="dest",
    )(group_start, eidx, rank)


def _dispatch_kernel(n_prompt_tiles, n_tiles, gend_ref, gsize_ref, dest_ref, up_ref, us_ref, xs_ref,
                     zero_buf, tiles, zsem, load_sem, scat_sem):
    i = pl.program_id(0)
    slot = lax.rem(i, DISPATCH_SLOTS)

    def start_load(tile, s):
        @pl.when(tile < n_prompt_tiles)
        def _():
            r = pl.multiple_of(tile * TOKEN_TILE, TOKEN_TILE)
            pltpu.make_async_copy(up_ref.at[pl.ds(r, TOKEN_TILE), :], tiles.at[s], load_sem.at[s]).start()

        @pl.when(tile >= n_prompt_tiles)
        def _():
            r = pl.multiple_of((tile - n_prompt_tiles) * TOKEN_TILE, TOKEN_TILE)
            pltpu.make_async_copy(us_ref.at[pl.ds(r, TOKEN_TILE), :], tiles.at[s], load_sem.at[s]).start()

    def wait_load(s):
        pltpu.make_async_copy(up_ref.at[pl.ds(0, TOKEN_TILE), :], tiles.at[s], load_sem.at[s]).wait()

    def wait_scatter(s):
        for _ in range(TOP_K):
            pltpu.make_async_copy(tiles.at[s], xs_ref.at[pl.ds(0, TOKEN_TILE), :], scat_sem.at[s]).wait()

    @pl.when(i == 0)
    def _():
        start_load(0, 0)
        zero_buf[...] = jnp.zeros_like(zero_buf)
        for e in range(N_EXPERTS):
            @pl.when(gsize_ref[e] > 0)
            def _():
                r0 = pl.multiple_of(gend_ref[e] - ROW_BLOCK, ROW_BLOCK)
                pltpu.make_async_copy(zero_buf, xs_ref.at[pl.ds(r0, ROW_BLOCK), :], zsem).start()
        for e in range(N_EXPERTS):
            @pl.when(gsize_ref[e] > 0)
            def _():
                pltpu.make_async_copy(zero_buf, xs_ref.at[pl.ds(0, ROW_BLOCK), :], zsem).wait()

        first_unused = gend_ref[N_EXPERTS - 1] // ROW_BLOCK
        n_blocks = xs_ref.shape[0] // ROW_BLOCK

        def zero_block(b, carry):
            r0 = pl.multiple_of(b * ROW_BLOCK, ROW_BLOCK)
            pltpu.make_async_copy(zero_buf, xs_ref.at[pl.ds(r0, ROW_BLOCK), :], zsem).start()
            return carry

        def zero_wait(b, carry):
            pltpu.make_async_copy(zero_buf, xs_ref.at[pl.ds(0, ROW_BLOCK), :], zsem).wait()
            return carry

        lax.fori_loop(first_unused, n_blocks, zero_block, 0)
        lax.fori_loop(first_unused, n_blocks, zero_wait, 0)

    @pl.when(i + 1 < n_tiles)
    def _():
        nxt = lax.rem(i + 1, DISPATCH_SLOTS)

        @pl.when(i + 1 >= DISPATCH_SLOTS)
        def _():
            wait_scatter(nxt)
        start_load(i + 1, nxt)

    wait_load(slot)
    for s in range(DISPATCH_SLOTS):
        @pl.when(slot == s)
        def _():
            for j in range(TOP_K * TOKEN_TILE):
                d = dest_ref[0, 0, j]
                pltpu.make_async_copy(tiles.at[s, pl.ds(j % TOKEN_TILE, 1), :], xs_ref.at[pl.ds(d, 1), :],
                                      scat_sem.at[s]).start(priority=j % 2)

    @pl.when(i == n_tiles - 1)
    def _():
        for tile in range(max(n_tiles - DISPATCH_SLOTS, 0), n_tiles):
            wait_scatter(tile % DISPATCH_SLOTS)


def _dispatch(group_end, group_size, dest_tiles, u2_p, u2_s, n_rows):
    n_p = u2_p.shape[0] // TOKEN_TILE
    n_s = u2_s.shape[0] // TOKEN_TILE
    grid_spec = pltpu.PrefetchScalarGridSpec(
        num_scalar_prefetch=2,
        grid=(n_p + n_s,),
        in_specs=[
            pl.BlockSpec((1, 1, TOP_K * TOKEN_TILE), lambda i, *_: (i, 0, 0), memory_space=pltpu.SMEM),
            pl.BlockSpec(memory_space=pl.ANY),
            pl.BlockSpec(memory_space=pl.ANY),
        ],
        out_specs=pl.BlockSpec(memory_space=pl.ANY),
        scratch_shapes=[pltpu.VMEM((ROW_BLOCK, D_MODEL), F32), pltpu.VMEM((DISPATCH_SLOTS, TOKEN_TILE, D_MODEL), F32),
                        pltpu.SemaphoreType.DMA(()), pltpu.SemaphoreType.DMA((DISPATCH_SLOTS,)),
                        pltpu.SemaphoreType.DMA((DISPATCH_SLOTS,))],
    )
    return pl.pallas_call(
        functools.partial(_dispatch_kernel, n_p, n_p + n_s),
        grid_spec=grid_spec,
        out_shape=jax.ShapeDtypeStruct((n_rows, D_MODEL), F32),
        compiler_params=_params(1),
        name="dispatch",
    )(group_end, group_size, dest_tiles, u2_p, u2_s)


_FIRST, _MIDDLE, _LAST = 0, 1, 2


def _experts_kernel(row0_ref, nblk_ref, exp_ref, total_ref, xs_ref, wg_ref, wl_ref, wd_ref, bg_ref, bl_ref, bd_ref,
                    ys_ref, xbuf, acc, in_sem, out_sem):
    i = pl.program_id(0)
    f = pl.program_id(1)
    n_items = pl.num_programs(0)
    last_f = pl.num_programs(1) - 1
    nblk = nblk_ref[i]
    row0 = row0_ref[i]
    nblk_prev = jnp.where(i > 0, nblk_ref[jnp.maximum(i - 1, 0)], 0)
    n_started = jnp.minimum(nblk_prev, nblk)
    nxt = jnp.minimum(i + 1, n_items - 1)
    row0_next = row0_ref[nxt]
    nblk_next = jnp.where(i + 1 < n_items, nblk_ref[nxt], 0)

    def local(j):
        return pl.ds(pl.multiple_of(j * ROW_BLOCK, ROW_BLOCK), ROW_BLOCK)

    def in_hbm(r0, j):
        return pl.ds(pl.multiple_of(r0 + j * ROW_BLOCK, ROW_BLOCK), ROW_BLOCK)

    def x_copy(j, r0=row0):
        return pltpu.make_async_copy(xs_ref.at[in_hbm(r0, j), :], xbuf.at[local(j), :], in_sem.at[j])

    def y_copy(j):
        return pltpu.make_async_copy(acc.at[local(j), :], ys_ref.at[in_hbm(row0, j), :], out_sem.at[j])

    def prefetch_next(j):
        @pl.when(j < nblk_next)
        def _():
            x_copy(j, row0_next).start()

    def body(chunks, phase):
        wg = wg_ref[...].astype(BF16)
        wl = wl_ref[...].astype(BF16)
        wd = wd_ref[...].astype(BF16)
        for j0, m in chunks:
            rows = pl.ds(pl.multiple_of(j0 * ROW_BLOCK, ROW_BLOCK), m * ROW_BLOCK)
            if phase == _FIRST:
                for j in range(m):
                    x_copy(j0 + j).wait()
            x = xbuf[rows, :].astype(BF16)
            glu = jnp.dot(x, wg, preferred_element_type=F32) + bg_ref[...]
            lin = jnp.dot(x, wl, preferred_element_type=F32) + bl_ref[...]
            glu = jnp.minimum(glu, SWIGLU_LIMIT)
            lin = jnp.clip(lin, -SWIGLU_LIMIT, SWIGLU_LIMIT)
            act = glu * jax.nn.sigmoid(SWIGLU_ALPHA * glu) * (lin + 1.0)
            y = jnp.dot(act.astype(BF16), wd, preferred_element_type=F32)
            if phase == _FIRST:
                acc[rows, :] = y + bd_ref[...]
            else:
                acc[rows, :] += y
            if phase == _LAST:
                for j in range(m):
                    y_copy(j0 + j).start()

    def run(phase):
        def before(j0, m):
            if phase == _FIRST:
                for j in range(m):
                    @pl.when(j0 + j < nblk_prev)
                    def _():
                        y_copy(j0 + j).wait()

        def after(j0, m):
            if phase == _LAST:
                for j in range(m):
                    prefetch_next(j0 + j)

        def quad(q, carry):
            before(4 * q, 4)
            body(((4 * q, 2), (4 * q + 2, 2)), phase)
            after(4 * q, 4)
            return carry

        n_quads = lax.shift_right_logical(nblk, 2)
        lax.fori_loop(0, n_quads, quad, 0)
        done = lax.shift_left(n_quads, 2)

        @pl.when(jnp.bitwise_and(nblk, 2) == 2)
        def _():
            before(done, 2)
            body(((done, 1), (done + 1, 1)), phase)
            after(done, 2)

        @pl.when(jnp.bitwise_and(nblk, 1) == 1)
        def _():
            before(nblk - 1, 1)
            body(((nblk - 1, 1),), phase)
            after(nblk - 1, 1)

    def drain(j, carry):
        y_copy(j).wait()
        return carry

    @pl.when(f == 0)
    def _():
        def start(j, carry):
            x_copy(j).start()
            return carry
        lax.fori_loop(n_started, nblk, start, 0)
        run(_FIRST)
        lax.fori_loop(nblk, jnp.where(nblk > 0, nblk_prev, 0), drain, 0)

    @pl.when((f > 0) & (f < last_f))
    def _():
        run(_MIDDLE)

    @pl.when(f == last_f)
    def _():
        run(_LAST)

        @pl.when(nblk_next == 0)
        def _():
            lax.fori_loop(0, nblk, drain, 0)

        @pl.when(i == n_items - 1)
        def _():
            first_unused = total_ref[0] // ROW_BLOCK
            n_blocks = ys_ref.shape[0] // ROW_BLOCK
            acc[local(0), :] = jnp.zeros((ROW_BLOCK, D_MODEL), F32)

            def zero_copy(b):
                r0 = pl.multiple_of(b * ROW_BLOCK, ROW_BLOCK)
                return pltpu.make_async_copy(acc.at[local(0), :], ys_ref.at[pl.ds(r0, ROW_BLOCK), :], out_sem.at[0])

            def zero_block(b, carry):
                zero_copy(b).start()
                return carry

            def zero_wait(b, carry):
                zero_copy(b).wait()
                return carry

            lax.fori_loop(first_unused, n_blocks, zero_block, 0)
            lax.fori_loop(first_unused, n_blocks, zero_wait, 0)


def _experts(items, total_rows, xs, w_up, b_up, w_down, b_down):
    row0, nblk, exp = items
    n_items = row0.shape[0]
    n_rows = xs.shape[0]
    n_f = D_FF // FF_TILE
    assert n_f >= 2

    def ff(i, f, nb):
        return jnp.where(nb[i] > 0, f, n_f - 1)

    max_rows = EXPERT_MAX_BLOCKS * ROW_BLOCK
    grid_spec = pltpu.PrefetchScalarGridSpec(
        num_scalar_prefetch=4,
        grid=(n_items, n_f),
        in_specs=[
            pl.BlockSpec(memory_space=pl.ANY),
            pl.BlockSpec((None, D_MODEL, FF_TILE), lambda i, f, r0, nb, ex, tot: (ex[i], 0, ff(i, f, nb))),
            pl.BlockSpec((None, D_MODEL, FF_TILE), lambda i, f, r0, nb, ex, tot: (ex[i], 0, n_f + ff(i, f, nb))),
            pl.BlockSpec((None, FF_TILE, D_MODEL), lambda i, f, r0, nb, ex, tot: (ex[i], ff(i, f, nb), 0)),
            pl.BlockSpec((None, 1, FF_TILE), lambda i, f, r0, nb, ex, tot: (ex[i], 0, ff(i, f, nb))),
            pl.BlockSpec((None, 1, FF_TILE), lambda i, f, r0, nb, ex, tot: (ex[i], 0, n_f + ff(i, f, nb))),
            pl.BlockSpec((None, 1, D_MODEL), lambda i, f, r0, nb, ex, tot: (ex[i], 0, 0)),
        ],
        out_specs=pl.BlockSpec(memory_space=pl.ANY),
        scratch_shapes=[pltpu.VMEM((max_rows, D_MODEL), F32), pltpu.VMEM((max_rows, D_MODEL), F32),
                        pltpu.SemaphoreType.DMA((EXPERT_MAX_BLOCKS,)), pltpu.SemaphoreType.DMA((EXPERT_MAX_BLOCKS,))],
    )
    return pl.pallas_call(
        _experts_kernel,
        grid_spec=grid_spec,
        out_shape=jax.ShapeDtypeStruct((n_rows, D_MODEL), F32),
        compiler_params=_params(2),
        name="experts",
    )(row0, nblk, exp, total_rows.reshape(1), xs, w_up, w_up, w_down,
      b_up.reshape(N_EXPERTS, 1, 2 * D_FF), b_up.reshape(N_EXPERTS, 1, 2 * D_FF), b_down.reshape(N_EXPERTS, 1, D_MODEL))


def _combine_kernel(n_prompt_tiles, dcur_ref, dnxt_ref, gates_ref, x1p_ref, x1s_ref, g2p_ref, g2s_ref, lng_ref, lnb_ref,
                    ys_ref, yp_ref, ysm_ref, buf, sem):
    i = pl.program_id(0)
    n = pl.num_programs(0)
    slot = i % 2

    def gather(idx_ref, s):
        for j in range(TOP_K * TOKEN_TILE):
            d = idx_ref[0, 0, j]
            pltpu.make_async_copy(ys_ref.at[pl.ds(d, 1), :],
                                  buf.at[s, j // TOKEN_TILE, pl.ds(j % TOKEN_TILE, 1), :],
                                  sem.at[s]).start(priority=j % 2)

    @pl.when(i == 0)
    def _():
        gather(dcur_ref, 0)

    for s in range(2):
        @pl.when((i + 1 < n) & (slot == 1 - s))
        def _():
            gather(dnxt_ref, s)

    for k in range(TOP_K):
        pltpu.make_async_copy(ys_ref.at[pl.ds(0, TOKEN_TILE), :], buf.at[slot, k], sem.at[slot]).wait()

    def finish(x1_ref, g2_ref, out_ref):
        gates = gates_ref[...]
        half = TOKEN_TILE // 2
        for r in (slice(0, half), slice(half, TOKEN_TILE)):
            ffn = buf[slot, 0, r, :] * gates[r, 0:1]
            for k in range(1, TOP_K):
                ffn = ffn + buf[slot, k, r, :] * gates[r, k:k + 1]
            gate2 = g2_ref[r, :] if g2_ref.shape[0] == TOKEN_TILE else g2_ref[...]
            out_ref[r, :] = (_standardise(DEEPNORM_ALPHA * x1_ref[r, :] + gate2 * ffn) * lng_ref[...]
                             + lnb_ref[...])

    @pl.when(i < n_prompt_tiles)
    def _():
        finish(x1p_ref, g2p_ref, yp_ref)

    @pl.when(i >= n_prompt_tiles)
    def _():
        finish(x1s_ref, g2s_ref, ysm_ref)


def _combine(dest_tiles, gates_tok, x1_p, x1_s, gate2_p, gate2_s, ln2_g, ln2_b, ys):
    n_p = x1_p.shape[0] // TOKEN_TILE
    n_s = x1_s.shape[0] // TOKEN_TILE
    n = n_p + n_s

    def p_idx(i):
        return jnp.minimum(i, n_p - 1)

    def s_idx(i):
        return jnp.maximum(i - n_p, 0)

    smem_tile = (1, 1, TOP_K * TOKEN_TILE)
    return pl.pallas_call(
        functools.partial(_combine_kernel, n_p),
        grid=(n,),
        in_specs=[
            pl.BlockSpec(smem_tile, lambda i: (i, 0, 0), memory_space=pltpu.SMEM),
            pl.BlockSpec(smem_tile, lambda i: (jnp.minimum(i + 1, n - 1), 0, 0), memory_space=pltpu.SMEM),
            pl.BlockSpec((TOKEN_TILE, TOP_K), lambda i: (i, 0)),
            pl.BlockSpec((TOKEN_TILE, D_MODEL), lambda i: (p_idx(i), 0)),
            pl.BlockSpec((TOKEN_TILE, D_MODEL), lambda i: (s_idx(i), 0)),
            pl.BlockSpec((1, D_MODEL), lambda i: (0, 0)),
            pl.BlockSpec((TOKEN_TILE, D_MODEL), lambda i: (s_idx(i), 0)),
            pl.BlockSpec((1, D_MODEL), lambda i: (0, 0)),
            pl.BlockSpec((1, D_MODEL), lambda i: (0, 0)),
            pl.BlockSpec(memory_space=pl.ANY),
        ],
        out_specs=[pl.BlockSpec((TOKEN_TILE, D_MODEL), lambda i: (p_idx(i), 0)),
                   pl.BlockSpec((TOKEN_TILE, D_MODEL), lambda i: (s_idx(i), 0))],
        out_shape=[jax.ShapeDtypeStruct(x1_p.shape, F32), jax.ShapeDtypeStruct(x1_s.shape, F32)],
        scratch_shapes=[pltpu.VMEM((2, TOP_K, TOKEN_TILE, D_MODEL), F32), pltpu.SemaphoreType.DMA((2,))],
        compiler_params=_params(1),
        name="combine",
    )(dest_tiles, dest_tiles, gates_tok, x1_p, x1_s, gate2_p, gate2_s,
      ln2_g.reshape(1, D_MODEL), ln2_b.reshape(1, D_MODEL), ys)


def _rope_tables(pos):
    half = HEAD_DIM // 2
    inv_freq = ROPE_THETA ** (-jnp.arange(half, dtype=F32) / half)
    ang = pos.astype(F32)[:, None] * inv_freq
    cos, sin = jnp.cos(ang), jnp.sin(ang)
    reps = LANES // HEAD_DIM
    return (jnp.tile(jnp.concatenate([cos, cos], axis=1), (1, reps)),
            jnp.tile(jnp.concatenate([-sin, sin], axis=1), (1, reps)))


def _work_items(group_start, group_size, n_items):
    chunk_rows = EXPERT_MAX_BLOCKS * ROW_BLOCK
    ex = jnp.arange(N_EXPERTS, dtype=jnp.int32)
    n_chunks = (group_size + chunk_rows - 1) // chunk_rows
    chunk_end = jnp.sum(jnp.where(ex[None, :] <= ex[:, None], n_chunks[None, :], 0), axis=1)
    chunk_start = chunk_end - n_chunks
    item = jnp.arange(n_items, dtype=jnp.int32)
    live = item < chunk_end[-1]
    it = jnp.minimum(item, chunk_end[-1] - 1)
    exp = jnp.minimum(jnp.sum((chunk_end[None, :] <= it[:, None]).astype(jnp.int32), axis=1), N_EXPERTS - 1)
    own = exp[:, None] == ex[None, :]

    def pick(per_expert):
        return jnp.sum(jnp.where(own, per_expert[None, :], 0), axis=1)

    chunk = it - pick(chunk_start)
    blocks = pick(group_size) // ROW_BLOCK
    per_chunk = (blocks + jnp.maximum(pick(n_chunks), 1) - 1) // jnp.maximum(pick(n_chunks), 1)
    row0 = pick(group_start) + chunk * per_chunk * ROW_BLOCK
    nblk = jnp.clip(blocks - chunk * per_chunk, 0, per_chunk)
    nblk = jnp.where(live, nblk, 0)
    return row0.astype(jnp.int32), nblk.astype(jnp.int32), exp.astype(jnp.int32)


def _moe(u2_p, u2_s, logits_t, x1_p, x1_s, gate2_p, gate2_s, ln2_g, ln2_b, w_up, b_up, w_down, b_down):
    n_tok = logits_t.shape[1]
    eidx, gates, rank, counts = _route(logits_t)
    count = counts[:, 0]
    group_size = (count + ROW_BLOCK - 1) // ROW_BLOCK * ROW_BLOCK
    ex = jnp.arange(N_EXPERTS, dtype=jnp.int32)
    group_end = jnp.sum(jnp.where(ex[None, :] <= ex[:, None], group_size[None, :], 0), axis=1)
    group_start = group_end - group_size
    dest = _dest(group_start, eidx, rank)
    n_tiles = n_tok // TOKEN_TILE
    dest_tiles = dest.reshape(TOP_K, n_tiles, TOKEN_TILE).transpose(1, 0, 2).reshape(n_tiles, 1, TOP_K * TOKEN_TILE)
    max_rows = n_tok * TOP_K + N_EXPERTS * (ROW_BLOCK - 1)
    n_rows = (max_rows + ROW_BLOCK - 1) // ROW_BLOCK * ROW_BLOCK
    n_items = N_EXPERTS + max_rows // (EXPERT_MAX_BLOCKS * ROW_BLOCK)
    xs = _dispatch(group_end, group_size, dest_tiles, u2_p, u2_s, n_rows)
    ys = _experts(_work_items(group_start, group_size, n_items), group_end[-1], xs, w_up, b_up, w_down, b_down)
    return _combine(dest_tiles, gates.T, x1_p, x1_s, gate2_p, gate2_s, ln2_g, ln2_b, ys)


def _layer(x_p, x_s, cache_k, cache_v, state_conv, c_p, c_s,
           w_ada, b_ada, w_in, b_in, conv_w, sinks, norm_attn_g, norm_conv_g, w_out, b_out,
           ln1_g, ln1_b, w_router, b_router, w_up, b_up, w_down, b_down, ln2_g, ln2_b, past_len):
    T = x_p.shape[0]
    B = x_s.shape[0]
    n_c = 1 + B
    pad_c = (-n_c) % SUBLANES
    c_all = jnp.concatenate([c_p, c_s, jnp.zeros((pad_c, D_MODEL), F32)], axis=0)
    mod = _ada(c_all, w_ada, b_ada)
    shift1, scale1, gate1, shift2, scale2, gate2 = [mod[:, j * D_MODEL:(j + 1) * D_MODEL] for j in range(6)]

    def prompt(a):
        return a[0:1]

    def sample(a):
        return a[1:n_c]

    w_in_bf = w_in.astype(BF16)
    w_out_bf = w_out.astype(BF16)
    w_router_t_bf = w_router.T.astype(BF16)

    cos_p, sin_p = _rope_tables(jnp.arange(T, dtype=jnp.int32))
    q_p, k_p, v_p, gb_p, uc_p = _inproj(x_p, prompt(shift1), prompt(scale1), cos_p, sin_p, w_in_bf, b_in,
                                        min(INPROJ_TILE_M, T))
    attn_p = _attn_prompt(q_p, k_p, v_p, sinks, norm_attn_g)
    x1_p, u2_p, lg_p = _mix(attn_p, gb_p, uc_p, jnp.zeros((CONV_WIDTH - 1, CONV_DIM), F32), conv_w, norm_conv_g,
                            w_out_bf, b_out, x_p, prompt(gate1), ln1_g, ln1_b, prompt(shift2), prompt(scale2),
                            w_router_t_bf, b_router, min(MIX_TILE_M, T), True)

    cos_s, sin_s = _rope_tables(jnp.full((1,), past_len, jnp.int32))
    q_s, k_s, v_s, gb_s, uc_s = _inproj(x_s, sample(shift1), sample(scale1), cos_s, sin_s, w_in_bf, b_in, B)
    attn_s, new_k_s, new_v_s = _attn_decode(q_s, k_s, v_s, cache_k, cache_v, sinks, norm_attn_g)
    x1_s, u2_s, lg_s = _mix(attn_s.reshape(B, ATTN_DIM), gb_s, uc_s, (state_conv[:, 1], state_conv[:, 0]), conv_w,
                            norm_conv_g, w_out_bf, b_out, x_s, sample(gate1), ln1_g, ln1_b, sample(shift2),
                            sample(scale2), w_router_t_bf, b_router, B, False)

    y_p, y_s = _moe(u2_p, u2_s, jnp.concatenate([lg_p, lg_s], axis=1), x1_p, x1_s, prompt(gate2), sample(gate2),
                    ln2_g, ln2_b, w_up, b_up, w_down, b_down)

    new_k_p = k_p[T - WINDOW:].reshape(WINDOW, N_KV_HEADS, HEAD_DIM)
    new_v_p = v_p[T - WINDOW:].reshape(WINDOW, N_KV_HEADS, HEAD_DIM)
    new_conv_p = uc_p[T - (CONV_WIDTH - 1):]
    new_conv_s = jnp.stack([state_conv[:, 1], uc_s], axis=1)
    return (y_p, y_s, new_k_p, new_v_p, new_conv_p,
            new_k_s.reshape(B, WINDOW, N_KV_HEADS, HEAD_DIM), new_v_s.reshape(B, WINDOW, N_KV_HEADS, HEAD_DIM), new_conv_s)


def kernel(x_prompt, x_sample, cache_k, cache_v, state_conv, c_prompt, c_sample, w_ada, b_ada, w_in, b_in, conv_w, sinks, norm_attn_g, norm_conv_g, w_out, b_out, ln1_g, ln1_b, w_router, b_router, w_up, b_up, w_down, b_down, ln2_g, ln2_b):
    assert x_prompt.shape[0] == 1 and x_sample.shape[1] == 1 and w_ada.shape[0] == DEPTH == 1
    B = x_sample.shape[0]
    (y_p, y_s, nk_p, nv_p, nc_p, nk_s, nv_s, nc_s) = _layer(
        x_prompt[0], x_sample[:, 0], cache_k[0], cache_v[0], state_conv[0], c_prompt, c_sample,
        w_ada[0], b_ada[0], w_in[0], b_in[0], conv_w[0], sinks[0], norm_attn_g[0], norm_conv_g[0], w_out[0], b_out[0],
        ln1_g[0], ln1_b[0], w_router[0], b_router[0], w_up[0], b_up[0], w_down[0], b_down[0], ln2_g[0], ln2_b[0],
        PAST_LEN)
    return (y_p[None], y_s.reshape(B, 1, D_MODEL), nk_p[None, None], nv_p[None, None], nc_p[None, None],
            nk_s[None], nv_s[None], nc_s[None])
```
